```python
import jax, jax.numpy as jnp
from jax import lax
import numpy as np

D_MODEL = 1024
BATCH = 2
SEQ = 16384
DEPTH = 1
DEC_BATCH = 32
DEC_SEQ = 32
PAST_LEN = 4096

CHUNK = 64
BAND_CHUNKS = 8
ATT_BAND = BAND_CHUNKS * CHUNK
A_HEADS = 8
A_DH = 64
A_WIDTH = A_HEADS * A_DH
M_HEADS = 4
M_DH = 128
M_WIDTH = M_HEADS * M_DH
D_MIX = A_WIDTH + M_WIDTH
REL_CLIP = 128
N_REL = 2 * REL_CLIP + 1
CONV_W = 4
D_FF = 2816
EPS = 1e-6
NEG = -1e30

OFF_AK = A_WIDTH
OFF_AV = 2 * A_WIDTH
OFF_MQK = 3 * A_WIDTH
OFF_MV = OFF_MQK + 2 * M_WIDTH
OFF_MO = OFF_MV + M_WIDTH
OFF_MG = OFF_MO + M_WIDTH
D_IN = OFF_MG + 2 * M_HEADS

kernel_name = 'hybrid_chunkband_attn_mlstm_stream_step'


def rmsnorm(x, g):
    xf = x.astype(jnp.float32)
    y = xf * lax.rsqrt(jnp.mean(xf * xf, axis=-1, keepdims=True) + EPS)
    return (y * g.astype(jnp.float32)).astype(x.dtype)


def head_rmsnorm(x, g, n_heads):
    B, L, W = x.shape
    y = rmsnorm(x.reshape(B, L, n_heads, W // n_heads), g.reshape(n_heads, W // n_heads))
    return y.reshape(B, L, W)


def ffn_half(x, g, w1, w3, w2):
    h = rmsnorm(x, g)
    return x + 0.5 * ((jax.nn.silu(h @ w1) * (h @ w3)) @ w2)


def causal_conv(u, buf, w, b):
    L = u.shape[1]
    full = jnp.concatenate([buf.astype(u.dtype), u], axis=1)
    y = b
    for j in range(CONV_W):
        y = y + full[:, j:j + L] * w[j]
    return y, full[:, full.shape[1] - (CONV_W - 1):]


def mixer_inputs(h, w_in, conv_w, conv_b, gate_bias, conv_buf):
    f32 = jnp.float32
    B, L = h.shape[:2]
    z = h @ w_in
    aq = z[..., :OFF_AK].reshape(B, L, A_HEADS, A_DH)
    ak = z[..., OFF_AK:OFF_AV].reshape(B, L, A_HEADS, A_DH)
    av = z[..., OFF_AV:OFF_MQK].reshape(B, L, A_HEADS, A_DH)
    qk, new_buf = causal_conv(z[..., OFF_MQK:OFF_MV], conv_buf, conv_w, conv_b)
    qk = jax.nn.silu(qk)
    to_heads = lambda t: t.reshape(B, L, M_HEADS, M_DH).transpose(0, 2, 1, 3).astype(f32)
    mq = to_heads(qk[..., :M_WIDTH])
    mk = to_heads(qk[..., M_WIDTH:]) * (M_DH ** -0.5)
    mv = to_heads(z[..., OFF_MV:OFF_MO])
    og = jax.nn.sigmoid(z[..., OFF_MO:OFF_MG])
    g = z[..., OFF_MG:D_IN].astype(f32) + gate_bias.astype(f32)
    ig = g[..., :M_HEADS].transpose(0, 2, 1)
    lf = jax.nn.log_sigmoid(g[..., M_HEADS:]).transpose(0, 2, 1)
    return (aq, ak, av), (mq, mk, mv, ig, lf), og, new_buf


def band_attention(q, k, v, qpos, kpos, kvalid, rel_bias):
    s = jnp.einsum('bqhd,bkhd->bhqk', q, k).astype(jnp.float32) * (A_DH ** -0.5)
    idx = jnp.clip(qpos[:, None] - kpos[None, :], -REL_CLIP, REL_CLIP) + REL_CLIP
    s = s + rel_bias[:, idx].astype(jnp.float32)[None]
    s = jnp.where(kvalid[None, None, None, :], s, NEG)
    p = jax.nn.softmax(s, axis=-1).astype(v.dtype)
    return jnp.einsum('bhqk,bkhd->bqhd', p, v)


def attn_prompt(q, k, v, rel_bias):
    B, S = q.shape[:2]
    nc = S // CHUNK
    pad = ((0, 0), (ATT_BAND, 0), (0, 0), (0, 0))
    kp = jnp.pad(k, pad)
    vp = jnp.pad(v, pad)
    qc = jnp.moveaxis(q.reshape(B, nc, CHUNK, A_HEADS, A_DH), 1, 0)
    band = ATT_BAND + CHUNK

    def one_chunk(args):
        c, qb = args
        start = c * CHUNK
        kb = lax.dynamic_slice_in_dim(kp, start, band, axis=1)
        vb = lax.dynamic_slice_in_dim(vp, start, band, axis=1)
        qpos = start + jnp.arange(CHUNK)
        kpos = start - ATT_BAND + jnp.arange(band)
        return band_attention(qb, kb, vb, qpos, kpos, kpos >= 0, rel_bias)

    out = lax.map(one_chunk, (jnp.arange(nc), qc))
    return jnp.moveaxis(out, 0, 1).reshape(B, S, A_WIDTH)


def attn_sample(q, k, v, ck, cv, rel_bias):
    Lc, T = ck.shape[1], q.shape[1]
    kk = jnp.concatenate([ck.astype(k.dtype), k], axis=1)
    vv = jnp.concatenate([cv.astype(v.dtype), v], axis=1)
    qpos = PAST_LEN + jnp.arange(T)
    kpos = PAST_LEN - Lc + jnp.arange(Lc + T)
    out = band_attention(q, kk, vv, qpos, kpos, jnp.ones((Lc + T,), bool), rel_bias)
    return out.reshape(q.shape[0], T, A_WIDTH)


def mlstm_chunk(carry, inp):
    C, n, m = carry
    q, k, v, ig, lf = inp
    L = q.shape[2]
    b = jnp.cumsum(lf, axis=-1)
    causal = jnp.tril(jnp.ones((L, L), bool))
    D = jnp.where(causal, b[..., :, None] - b[..., None, :] + ig[..., None, :], -jnp.inf)
    inter = b + m[..., None]
    mt = jnp.maximum(inter, jnp.max(D, axis=-1))
    Dw = jnp.exp(D - mt[..., None])
    iw = jnp.exp(inter - mt)
    s = jnp.einsum('bhtd,bhsd->bhts', q, k) * Dw
    num = iw[..., None] * jnp.einsum('bhtd,bhde->bhte', q, C) + jnp.einsum('bhts,bhse->bhte', s, v)
    den = iw * jnp.einsum('bhtd,bhd->bht', q, n) + jnp.sum(s, axis=-1)
    h = num / jnp.maximum(jnp.abs(den), jnp.exp(-mt))[..., None]
    bL = b[..., -1]
    m_new = mt[..., -1]
    wk = jnp.exp(bL[..., None] - b + ig - m_new[..., None])
    decay = jnp.exp(bL + m - m_new)
    C_new = decay[..., None, None] * C + jnp.einsum('bhs,bhsd,bhse->bhde', wk, k, v)
    n_new = decay[..., None] * n + jnp.einsum('bhs,bhsd->bhd', wk, k)
    return (C_new, n_new, m_new), h


def mlstm_prompt(mq, mk, mv, ig, lf):
    B, H, S, _ = mq.shape
    nc = S // CHUNK
    to_chunks = lambda t: jnp.moveaxis(t.reshape((B, H, nc, CHUNK) + t.shape[3:]), 2, 0)
    f32 = jnp.float32
    init = (jnp.zeros((B, H, M_DH, M_DH), f32), jnp.zeros((B, H, M_DH), f32), jnp.zeros((B, H), f32))
    carry, hs = lax.scan(mlstm_chunk, init, (to_chunks(mq), to_chunks(mk), to_chunks(mv), to_chunks(ig), to_chunks(lf)))
    h = jnp.moveaxis(hs, 0, 2).reshape(B, H, S, M_DH)
    return h, carry


def mixer_output(a_out, m_h, og, norm_m, w_out):
    B, L = a_out.shape[:2]
    m_h = m_h.transpose(0, 2, 1, 3).reshape(B, L, M_WIDTH).astype(a_out.dtype) * og
    m_h = head_rmsnorm(m_h, norm_m, M_HEADS)
    return jnp.concatenate([a_out, m_h], axis=-1) @ w_out


def setup_inputs(seed: int = 0) -> dict:
    key = jax.random.key(seed)
    ks = iter(jax.random.split(key, 40))
    f32 = jnp.float32

    def nrm(shape, scale):
        return jax.random.normal(next(ks), shape, f32) * scale

    def gain(shape):
        return 1.0 + nrm(shape, 0.02)

    att_cache = min(ATT_BAND, PAST_LEN)
    x_prompt = nrm((BATCH, SEQ, D_MODEL), 1.0)
    x_sample = nrm((DEC_BATCH, DEC_SEQ, D_MODEL), 1.0)
    cache_attn_k = nrm((DEPTH, DEC_BATCH, att_cache, A_HEADS, A_DH), 1.0)
    cache_attn_v = nrm((DEPTH, DEC_BATCH, att_cache, A_HEADS, A_DH), 1.0)
    state_mlstm_C = nrm((DEPTH, DEC_BATCH, M_HEADS, M_DH, M_DH), 0.3)
    state_mlstm_n = nrm((DEPTH, DEC_BATCH, M_HEADS, M_DH), 1.0)
    state_mlstm_m = nrm((DEPTH, DEC_BATCH, M_HEADS), 0.5)
    state_mlstm_conv = nrm((DEPTH, DEC_BATCH, CONV_W - 1, 2 * M_WIDTH), 1.0)
    fbias = jnp.linspace(3.0, 6.0, M_HEADS).astype(f32)
    gate_bias = jnp.concatenate([nrm((DEPTH, M_HEADS), 0.1), fbias + nrm((DEPTH, M_HEADS), 0.1)], axis=-1)
    return {
        'x_prompt': x_prompt,
        'x_sample': x_sample,
        'cache_attn_k': cache_attn_k,
        'cache_attn_v': cache_attn_v,
        'state_mlstm_C': state_mlstm_C,
        'state_mlstm_n': state_mlstm_n,
        'state_mlstm_m': state_mlstm_m,
        'state_mlstm_conv': state_mlstm_conv,
        'norm_ffn1': gain((DEPTH, D_MODEL)),
        'w1_ffn1': nrm((DEPTH, D_MODEL, D_FF), D_MODEL ** -0.5),
        'w3_ffn1': nrm((DEPTH, D_MODEL, D_FF), D_MODEL ** -0.5),
        'w2_ffn1': nrm((DEPTH, D_FF, D_MODEL), D_FF ** -0.5),
        'norm_mix': gain((DEPTH, D_MODEL)),
        'w_in': nrm((DEPTH, D_MODEL, D_IN), D_MODEL ** -0.5),
        'conv_w': nrm((DEPTH, CONV_W, 2 * M_WIDTH), CONV_W ** -0.5),
        'conv_b': nrm((DEPTH, 2 * M_WIDTH), 0.02),
        'gate_bias': gate_bias,
        'rel_bias': nrm((DEPTH, A_HEADS, N_REL), 0.2),
        'norm_mlstm_out': gain((DEPTH, M_WIDTH)),
        'w_out': nrm((DEPTH, D_MIX, D_MODEL), D_MIX ** -0.5),
        'norm_ffn2': gain((DEPTH, D_MODEL)),
        'w1_ffn2': nrm((DEPTH, D_MODEL, D_FF), D_MODEL ** -0.5),
        'w3_ffn2': nrm((DEPTH, D_MODEL, D_FF), D_MODEL ** -0.5),
        'w2_ffn2': nrm((DEPTH, D_FF, D_MODEL), D_FF ** -0.5),
        'norm_final': gain((D_MODEL,)),
    }


def reference(x_prompt, x_sample, cache_attn_k, cache_attn_v, state_mlstm_C, state_mlstm_n,
              state_mlstm_m, state_mlstm_conv, norm_ffn1, w1_ffn1, w3_ffn1, w2_ffn1,
              norm_mix, w_in, conv_w, conv_b, gate_bias, rel_bias, norm_mlstm_out, w_out,
              norm_ffn2, w1_ffn2, w3_ffn2, w2_ffn2, norm_final):
    f32 = jnp.float32
    B, S = x_prompt.shape[:2]
    keep = min(ATT_BAND, S)
    xp, xs = x_prompt, x_sample
    kp_l, vp_l, Cp_l, np_l, mp_l, cp_l = [], [], [], [], [], []
    ks_l, vs_l, Cs_l, ns_l, ms_l, cs_l = [], [], [], [], [], []
    for l in range(DEPTH):
        xp = ffn_half(xp, norm_ffn1[l], w1_ffn1[l], w3_ffn1[l], w2_ffn1[l])
        xs = ffn_half(xs, norm_ffn1[l], w1_ffn1[l], w3_ffn1[l], w2_ffn1[l])

        zero_buf = jnp.zeros((B, CONV_W - 1, 2 * M_WIDTH), xp.dtype)
        (aq, ak, av), mins, og, buf = mixer_inputs(rmsnorm(xp, norm_mix[l]), w_in[l], conv_w[l],
                                                   conv_b[l], gate_bias[l], zero_buf)
        a_out = attn_prompt(aq, ak, av, rel_bias[l])
        m_h, (C, n, m) = mlstm_prompt(*mins)
        xp = xp + mixer_output(a_out, m_h, og, norm_mlstm_out[l], w_out[l])
        kp_l.append(ak[:, S - keep:])
        vp_l.append(av[:, S - keep:])
        Cp_l.append(C)
        np_l.append(n)
        mp_l.append(m)
        cp_l.append(buf)

        (aq, ak, av), mins, og, buf = mixer_inputs(rmsnorm(xs, norm_mix[l]), w_in[l], conv_w[l],
                                                   conv_b[l], gate_bias[l], state_mlstm_conv[l])
        a_out = attn_sample(aq, ak, av, cache_attn_k[l], cache_attn_v[l], rel_bias[l])
        carry0 = (state_mlstm_C[l].astype(f32), state_mlstm_n[l].astype(f32), state_mlstm_m[l].astype(f32))
        (C, n, m), m_h = mlstm_chunk(carry0, mins)
        xs = xs + mixer_output(a_out, m_h, og, norm_mlstm_out[l], w_out[l])
        ks_l.append(ak)
        vs_l.append(av)
        Cs_l.append(C)
        ns_l.append(n)
        ms_l.append(m)
        cs_l.append(buf)

        xp = ffn_half(xp, norm_ffn2[l], w1_ffn2[l], w3_ffn2[l], w2_ffn2[l])
        xs = ffn_half(xs, norm_ffn2[l], w1_ffn2[l], w3_ffn2[l], w2_ffn2[l])

    y_prompt = rmsnorm(xp, norm_final)
    y_sample = rmsnorm(xs, norm_final)
    return (y_prompt, y_sample,
            jnp.stack(kp_l), jnp.stack(vp_l), jnp.stack(Cp_l), jnp.stack(np_l), jnp.stack(mp_l), jnp.stack(cp_l),
            jnp.stack(ks_l), jnp.stack(vs_l), jnp.stack(Cs_l), jnp.stack(ns_l), jnp.stack(ms_l), jnp.stack(cs_l))
```

```python
import functools

import numpy as np
import jax
import jax.numpy as jnp
from jax import lax
from jax.experimental import pallas as pl
from jax.experimental.pallas import tpu as pltpu

F32 = jnp.float32
BF16 = jnp.bfloat16

CHUNK = 64
ATT_BAND = 8 * CHUNK
A_HEADS = 8
A_DH = 64
A_WIDTH = A_HEADS * A_DH
M_HEADS = 4
M_DH = 128
M_WIDTH = M_HEADS * M_DH
REL_CLIP = 128
CONV_W = 4
PAST_LEN = 4096
EPS = 1e-6
NEG = -1e30

LANES = 128
SUBLANES = 8
MXU_DIM = 256
VMEM_LIMIT_BYTES = 56 * 1024 * 1024

OFF_AK = A_WIDTH
OFF_AV = 2 * A_WIDTH
OFF_MQK = 3 * A_WIDTH
OFF_MV = OFF_MQK + 2 * M_WIDTH
OFF_MO = OFF_MV + M_WIDTH
OFF_MG = OFF_MO + M_WIDTH
N_GATES = 2 * M_HEADS
D_IN_PAD = OFF_MG + LANES
GATE_ROWS = 16

FFN_ROWS = 512
FFN_CHUNK = MXU_DIM
PROMPT_TILE = 256


def _rms(x, g):
    return x * lax.rsqrt(jnp.mean(x * x, axis=-1, keepdims=True) + EPS) * g


def _sigmoid(x):
    return 1.0 / (1.0 + jnp.exp(-x))


def _log_sigmoid(x):
    return jnp.minimum(x, 0.0) - jnp.log1p(jnp.exp(-jnp.abs(x)))


def _dot(a, b):
    return jnp.dot(a, b, preferred_element_type=F32)


def _dot_nt(a, b):
    return lax.dot_general(a, b, (((1,), (1,)), ((), ())), preferred_element_type=F32)


def _split3(x):
    p1 = x.astype(BF16)
    r1 = x - p1.astype(F32)
    p2 = r1.astype(BF16)
    p3 = (r1 - p2.astype(F32)).astype(BF16)
    return p1, p2, p3


def _resident(shape):
    nd = len(shape)
    return pl.BlockSpec(shape, lambda *_: (0,) * nd, pipeline_mode=pl.Buffered(1))


def _ffn_kernel(x_ref, g_ref, w1_ref, w3_ref, w2_ref, *rest, n_chunks, final_norm):
    if final_norm:
        gf_ref, o_ref, h_ref, acc_ref = rest
    else:
        o_ref, h_ref, acc_ref = rest
    x = x_ref[...]
    h_ref[...] = _rms(x, g_ref[...]).astype(BF16)
    acc_ref[...] = jnp.zeros_like(acc_ref)

    def body(c, carry):
        h = h_ref[...]
        a = _dot(h, w1_ref[c])
        b = _dot(h, w3_ref[c])
        u = (a * _sigmoid(a) * b).astype(BF16)
        acc_ref[...] += _dot(u, w2_ref[c])
        return carry

    lax.fori_loop(0, n_chunks, body, 0)
    y = x + 0.5 * acc_ref[...]
    if final_norm:
        y = _rms(y, gf_ref[...])
    o_ref[...] = y


def _ffn_weights(w1, w3, w2):
    d, f = w1.shape
    nc = f // FFN_CHUNK
    w1c = w1.astype(BF16).reshape(d, nc, FFN_CHUNK).transpose(1, 0, 2)
    w3c = w3.astype(BF16).reshape(d, nc, FFN_CHUNK).transpose(1, 0, 2)
    w2c = w2.astype(BF16).reshape(nc, FFN_CHUNK, d)
    return w1c, w3c, w2c


def _ffn(x2d, g, weights, gf=None):
    n, d = x2d.shape
    w1c, w3c, w2c = weights
    nc = w1c.shape[0]
    rows = min(FFN_ROWS, n)
    final_norm = gf is not None
    row_spec = pl.BlockSpec((rows, d), lambda i: (i, 0))
    in_specs = [row_spec, _resident((1, d)), _resident(w1c.shape), _resident(w3c.shape), _resident(w2c.shape)]
    args = [x2d, g.reshape(1, d), w1c, w3c, w2c]
    if final_norm:
        in_specs.append(_resident((1, d)))
        args.append(gf.reshape(1, d))
    return pl.pallas_call(
        functools.partial(_ffn_kernel, n_chunks=nc, final_norm=final_norm),
        grid=(n // rows,),
        in_specs=in_specs,
        out_specs=row_spec,
        out_shape=jax.ShapeDtypeStruct((n, d), F32),
        scratch_shapes=[pltpu.VMEM((rows, d), BF16), pltpu.VMEM((rows, d), F32)],
        compiler_params=pltpu.CompilerParams(dimension_semantics=("arbitrary",),
                                             vmem_limit_bytes=VMEM_LIMIT_BYTES),
        name="ffn_final" if final_norm else "ffn",
    )(*args)


def _mixer_kernel(x_ref, gmix_ref, win_ref, wgt_ref, convw_ref, convb_ref, gbrow_ref, gbcol_ref,
                  tab_ref, normm_ref, wout_ref, k0_ref, v0_ref, c0_ref, n0_ref, m0_ref, conv0_ref,
                  y_ref, ko_ref, vo_ref, co_ref, no_ref, mo_ref, convo_ref,
                  kband, vband, ubuf, c_s, n_s, m_s, mix_s,
                  *, tile, chunk, band_rows, pos0, n_tiles):
    t = pl.program_id(1)

    @pl.when(t == 0)
    def _load_state():
        kband[0:ATT_BAND, :] = k0_ref[0].astype(BF16)
        vband[0:ATT_BAND, :] = v0_ref[0].astype(BF16)
        if band_rows > ATT_BAND + tile:
            pad = band_rows - ATT_BAND - tile
            kband[ATT_BAND + tile:band_rows, :] = jnp.zeros((pad, A_WIDTH), BF16)
            vband[ATT_BAND + tile:band_rows, :] = jnp.zeros((pad, A_WIDTH), BF16)
        ubuf[0:SUBLANES, :] = conv0_ref[0]
        c_s[...] = c0_ref[0]
        n_s[...] = n0_ref[0]
        m_s[...] = m0_ref[0]

    x = x_ref[0]
    h = _rms(x, gmix_ref[...]).astype(BF16)
    z = _dot(h, win_ref[...])

    ak = z[:, OFF_AK:OFF_AV]
    av = z[:, OFF_AV:OFF_MQK]
    ko_ref[0] = ak
    vo_ref[0] = av
    kband[ATT_BAND:ATT_BAND + tile, :] = ak.astype(BF16)
    vband[ATT_BAND:ATT_BAND + tile, :] = av.astype(BF16)
    lane = lax.broadcasted_iota(jnp.int32, (1, LANES), 1)
    even = lane < A_DH
    if pos0 < ATT_BAND:
        col = lax.broadcasted_iota(jnp.int32, (1, band_rows), 1)
        in_stream = col >= (ATT_BAND - pos0) - t * tile
    for grp in range(A_HEADS // 2):
        lo, hi = grp * LANES, (grp + 1) * LANES
        q = z[:, lo:hi] * (A_DH ** -0.5)
        kp = kband[:, lo:hi]
        vp = vband[:, lo:hi]
        out = None
        for par in range(2):
            mine = even if par == 0 else jnp.logical_not(even)
            qh = jnp.where(mine, q, 0.0).astype(BF16)
            s = _dot_nt(qh, kp) + tab_ref[2 * grp + par]
            if pos0 < ATT_BAND:
                s = jnp.where(in_stream, s, NEG)
            e = jnp.exp(s - jnp.max(s, axis=-1, keepdims=True))
            p = (e * (1.0 / jnp.sum(e, axis=-1, keepdims=True))).astype(BF16)
            o = _dot(p, jnp.where(mine, vp, jnp.zeros_like(vp)))
            out = o if out is None else out + o
        mix_s[:, lo:hi] = out

    ubuf[SUBLANES:SUBLANES + tile, :] = z[:, OFF_MQK:OFF_MV]
    qk = convb_ref[...]
    for j in range(CONV_W):
        start = SUBLANES - (CONV_W - 1) + j
        qk = qk + ubuf[start:start + tile, :] * convw_ref[j:j + 1, :]
    qk = qk * _sigmoid(qk)
    last_rows = ubuf[tile:tile + SUBLANES, :]
    convo_ref[0] = last_rows
    ubuf[0:SUBLANES, :] = last_rows
    mq = qk[:, :M_WIDTH]
    mk = qk[:, M_WIDTH:] * (M_DH ** -0.5)
    mv = z[:, OFF_MV:OFF_MO]

    g_col = z[:, OFF_MG:OFF_MG + LANES] + gbrow_ref[...]
    g_row = _dot_nt(wgt_ref[...], h) + gbcol_ref[...]
    lf_col = _log_sigmoid(g_col)
    lf_row = _log_sigmoid(g_row)
    ri = lax.broadcasted_iota(jnp.int32, (tile, tile), 0)
    ci = lax.broadcasted_iota(jnp.int32, (tile, tile), 1)
    assert chunk & (chunk - 1) == 0
    chunk_shift = chunk.bit_length() - 1
    same_chunk = (ri >> chunk_shift) == (ci >> chunk_shift)
    tri = jnp.where(jnp.logical_and(same_chunk, ci <= ri), 1.0, 0.0).astype(BF16)
    tri_t = jnp.where(jnp.logical_and(same_chunk, ri <= ci), 1.0, 0.0).astype(BF16)
    b_col = sum(_dot(tri, part) for part in _split3(lf_col))
    b_row = sum(_dot(part, tri_t) for part in _split3(lf_row))

    causal = (lax.broadcasted_iota(jnp.int32, (chunk, chunk), 1)
              <= lax.broadcasted_iota(jnp.int32, (chunk, chunk), 0))
    og = _sigmoid(z[:, OFF_MO:OFF_MG])
    for hd in range(M_HEADS):
        lo, hi = hd * M_DH, (hd + 1) * M_DH
        cmat = c_s[hd]
        nrow = n_s[hd:hd + 1, :]
        m_prev = m_s[hd:hd + 1, 0:1]
        h_rows = []
        for c in range(tile // chunk):
            r0, r1 = c * chunk, (c + 1) * chunk
            bc = b_col[r0:r1, M_HEADS + hd:M_HEADS + hd + 1]
            igc = g_col[r0:r1, hd:hd + 1]
            br = b_row[M_HEADS + hd:M_HEADS + hd + 1, r0:r1]
            igr = g_row[hd:hd + 1, r0:r1]
            dmat = jnp.where(causal, bc + (igr - br), NEG)
            inter = bc + m_prev
            mt = jnp.maximum(inter, jnp.max(dmat, axis=-1, keepdims=True))
            dw = jnp.exp(dmat - mt)
            iw = jnp.exp(inter - mt)
            q = mq[r0:r1, lo:hi]
            k = mk[r0:r1, lo:hi]
            vb = mv[r0:r1, lo:hi].astype(BF16)
            qb = q.astype(BF16)
            s = _dot_nt(qb, k.astype(BF16)) * dw
            num = iw * _dot(qb, cmat.astype(BF16)) + _dot(s.astype(BF16), vb)
            den = iw * jnp.sum(q * nrow, axis=-1, keepdims=True) + jnp.sum(s, axis=-1, keepdims=True)
            h_rows.append(num / jnp.maximum(jnp.abs(den), jnp.exp(-mt)))
            b_last = bc[chunk - 1:chunk, :]
            m_new = mt[chunk - 1:chunk, :]
            kw = k * jnp.exp(b_last - bc + igc - m_new)
            decay = jnp.exp(b_last + m_prev - m_new)
            cmat = decay * cmat + _dot(kw.T.astype(BF16), vb)
            nrow = decay * nrow + jnp.sum(kw, axis=0, keepdims=True)
            m_prev = m_new
        c_s[hd] = cmat
        n_s[hd:hd + 1, :] = nrow
        m_s[hd:hd + 1, :] = jnp.broadcast_to(m_prev, (1, LANES))
        mh = jnp.concatenate(h_rows, axis=0) * og[:, lo:hi]
        mix_s[:, A_WIDTH + lo:A_WIDTH + hi] = _rms(mh, normm_ref[:, lo:hi])

    co_ref[0] = c_s[...]
    no_ref[0] = n_s[...]
    mo_ref[0] = m_s[...]
    y_ref[0] = x + _dot(mix_s[...].astype(BF16), wout_ref[...])

    if n_tiles > 1:
        kband[0:ATT_BAND, :] = kband[tile:tile + ATT_BAND, :]
        vband[0:ATT_BAND, :] = vband[tile:tile + ATT_BAND, :]


def _bias_table(rel_bias, tile, band_rows, chunked):
    i = np.arange(tile)[:, None]
    j = np.arange(band_rows)[None, :]
    idx = np.clip(i + ATT_BAND - j, -REL_CLIP, REL_CLIP) + REL_CLIP
    visible = j < ATT_BAND + tile
    if chunked:
        qc = i // CHUNK
        kc = (j - ATT_BAND) // CHUNK
        visible = visible & (kc <= qc) & (kc >= qc - ATT_BAND // CHUNK)
    return rel_bias[:, idx] + jnp.asarray(np.where(visible, 0.0, NEG), F32)


def _mixer(x, params, state, *, tile, chunk, pos0, chunked):
    nb, frames, d = x.shape
    gmix, w_in, conv_w, conv_b, gate_bias, rel_bias, norm_m, w_out = params
    k0, v0, c0, n0, m0, conv0 = state
    n_tiles = frames // tile
    band_rows = ATT_BAND + -(-tile // LANES) * LANES
    keep_tiles = min(ATT_BAND, frames) // tile
    width2 = 2 * M_WIDTH

    win = jnp.concatenate([w_in, jnp.zeros((d, D_IN_PAD - w_in.shape[1]), F32)], axis=1).astype(BF16)
    wgt = jnp.zeros((GATE_ROWS, d), F32).at[:N_GATES].set(w_in[:, OFF_MG:].T).astype(BF16)
    gb_row = jnp.zeros((1, LANES), F32).at[0, :N_GATES].set(gate_bias)
    gb_col = jnp.zeros((GATE_ROWS, 1), F32).at[:N_GATES, 0].set(gate_bias)
    tab = _bias_table(rel_bias, tile, band_rows, chunked)
    conv0p = jnp.concatenate([jnp.zeros((nb, SUBLANES - (CONV_W - 1), width2), F32), conv0], axis=1)
    m0p = jnp.broadcast_to(m0[:, :, None], (nb, M_HEADS, LANES))

    per_stream = lambda *dims: pl.BlockSpec((1,) + dims, lambda b, t: (b,) + (0,) * len(dims))
    tile_spec = pl.BlockSpec((1, tile, d), lambda b, t: (b, t, 0))
    kv_spec = pl.BlockSpec((1, tile, A_WIDTH),
                           lambda b, t: (b, jnp.maximum(t - (n_tiles - keep_tiles), 0), 0))
    in_specs = [
        tile_spec, _resident((1, d)), _resident(win.shape), _resident(wgt.shape),
        _resident((CONV_W, width2)), _resident((1, width2)), _resident((1, LANES)), _resident((GATE_ROWS, 1)),
        _resident(tab.shape), _resident((1, M_WIDTH)), _resident((d, d)),
        per_stream(ATT_BAND, A_WIDTH), per_stream(ATT_BAND, A_WIDTH), per_stream(M_HEADS, M_DH, M_DH),
        per_stream(M_HEADS, M_DH), per_stream(M_HEADS, LANES), per_stream(SUBLANES, width2),
    ]
    out_specs = [
        tile_spec, kv_spec, kv_spec, per_stream(M_HEADS, M_DH, M_DH), per_stream(M_HEADS, M_DH),
        per_stream(M_HEADS, LANES), per_stream(SUBLANES, width2),
    ]
    out_shape = [
        jax.ShapeDtypeStruct((nb, frames, d), F32),
        jax.ShapeDtypeStruct((nb, keep_tiles * tile, A_WIDTH), F32),
        jax.ShapeDtypeStruct((nb, keep_tiles * tile, A_WIDTH), F32),
        jax.ShapeDtypeStruct((nb, M_HEADS, M_DH, M_DH), F32),
        jax.ShapeDtypeStruct((nb, M_HEADS, M_DH), F32),
        jax.ShapeDtypeStruct((nb, M_HEADS, LANES), F32),
        jax.ShapeDtypeStruct((nb, SUBLANES, width2), F32),
    ]
    scratch = [
        pltpu.VMEM((band_rows, A_WIDTH), BF16), pltpu.VMEM((band_rows, A_WIDTH), BF16),
        pltpu.VMEM((tile + SUBLANES, width2), F32),
        pltpu.VMEM((M_HEADS, M_DH, M_DH), F32), pltpu.VMEM((M_HEADS, M_DH), F32), pltpu.VMEM((M_HEADS, LANES), F32),
        pltpu.VMEM((tile, d), F32),
    ]
    y, ko, vo, c1, n1, m1, conv1 = pl.pallas_call(
        functools.partial(_mixer_kernel, tile=tile, chunk=chunk, band_rows=band_rows, pos0=pos0,
                          n_tiles=n_tiles),
        grid=(nb, n_tiles),
        in_specs=in_specs,
        out_specs=out_specs,
        out_shape=out_shape,
        scratch_shapes=scratch,
        compiler_params=pltpu.CompilerParams(dimension_semantics=("arbitrary", "arbitrary"),
                                             vmem_limit_bytes=VMEM_LIMIT_BYTES),
        name="mixer_chunked" if chunked else "mixer_step",
    )(x, gmix.reshape(1, d), win, wgt, conv_w, conv_b.reshape(1, width2), gb_row, gb_col, tab,
      norm_m.reshape(1, M_WIDTH), w_out.astype(BF16), k0, v0, c0, n0, m0p, conv0p)
    keep = min(ATT_BAND, frames)
    new_state = (ko.reshape(nb, keep, A_HEADS, A_DH), vo.reshape(nb, keep, A_HEADS, A_DH), c1, n1,
                 m1[:, :, 0], conv1[:, SUBLANES - (CONV_W - 1):, :])
    return y, new_state


def kernel(x_prompt, x_sample, cache_attn_k, cache_attn_v, state_mlstm_C, state_mlstm_n, state_mlstm_m, state_mlstm_conv, norm_ffn1, w1_ffn1, w3_ffn1, w2_ffn1, norm_mix, w_in, conv_w, conv_b, gate_bias, rel_bias, norm_mlstm_out, w_out, norm_ffn2, w1_ffn2, w3_ffn2, w2_ffn2, norm_final):
    depth = norm_ffn1.shape[0]
    nbp, seq, d = x_prompt.shape
    nbs, dec, _ = x_sample.shape
    xp = x_prompt.reshape(nbp * seq, d)
    xs = x_sample.reshape(nbs * dec, d)
    prompt_tile = min(PROMPT_TILE, seq)
    new_p, new_s = [], []
    for l in range(depth):
        last = l == depth - 1
        ffn1 = _ffn_weights(w1_ffn1[l], w3_ffn1[l], w2_ffn1[l])
        ffn2 = _ffn_weights(w1_ffn2[l], w3_ffn2[l], w2_ffn2[l])
        mix = (norm_mix[l], w_in[l], conv_w[l], conv_b[l], gate_bias[l], rel_bias[l], norm_mlstm_out[l], w_out[l])
        gf = norm_final if last else None

        xp = _ffn(xp, norm_ffn1[l], ffn1)
        zero_state = (jnp.zeros((nbp, ATT_BAND, A_WIDTH), F32), jnp.zeros((nbp, ATT_BAND, A_WIDTH), F32),
                      jnp.zeros((nbp, M_HEADS, M_DH, M_DH), F32), jnp.zeros((nbp, M_HEADS, M_DH), F32),
                      jnp.zeros((nbp, M_HEADS), F32), jnp.zeros((nbp, CONV_W - 1, 2 * M_WIDTH), F32))
        xp3, st = _mixer(xp.reshape(nbp, seq, d), mix, zero_state, tile=prompt_tile, chunk=CHUNK, pos0=0,
                         chunked=True)
        new_p.append(st)
        xp = _ffn(xp3.reshape(nbp * seq, d), norm_ffn2[l], ffn2, gf)

        xs = _ffn(xs, norm_ffn1[l], ffn1)
        cache = (cache_attn_k[l].reshape(nbs, -1, A_WIDTH), cache_attn_v[l].reshape(nbs, -1, A_WIDTH),
                 state_mlstm_C[l], state_mlstm_n[l], state_mlstm_m[l], state_mlstm_conv[l])
        xs3, st = _mixer(xs.reshape(nbs, dec, d), mix, cache, tile=dec, chunk=dec, pos0=PAST_LEN, chunked=False)
        new_s.append(st)
        xs = _ffn(xs3.reshape(nbs * dec, d), norm_ffn2[l], ffn2, gf)

    stack = lambda states, i: jnp.stack([s[i] for s in states])
    return ((xp.reshape(nbp, seq, d), xs.reshape(nbs, dec, d))
            + tuple(stack(new_p, i) for i in range(6)) + tuple(stack(new_s, i) for i in range(6)))
```

```python
import functools

import numpy as np
import jax
import jax.numpy as jnp
from jax import lax
from jax.experimental import pallas as pl
from jax.experimental.pallas import tpu as pltpu

F32 = jnp.float32
BF16 = jnp.bfloat16

CHUNK = 64
ATT_BAND = 8 * CHUNK
A_HEADS = 8
A_DH = 64
A_WIDTH = A_HEADS * A_DH
M_HEADS = 4
M_DH = 128
M_WIDTH = M_HEADS * M_DH
REL_CLIP = 128
CONV_W = 4
PAST_LEN = 4096
EPS = 1e-6
NEG = -1e30

LANES = 128
SUBLANES = 8
MXU_DIM = 256
VMEM_LIMIT_BYTES = 56 * 1024 * 1024

OFF_AK = A_WIDTH
OFF_AV = 2 * A_WIDTH
OFF_MQK = 3 * A_WIDTH
OFF_MV = OFF_MQK + 2 * M_WIDTH
OFF_MO = OFF_MV + M_WIDTH
OFF_MG = OFF_MO + M_WIDTH
N_GATES = 2 * M_HEADS
D_IN_PAD = OFF_MG + LANES
GATE_ROWS = 16

FFN_ROWS = 512
FFN_CHUNK = MXU_DIM
PROMPT_TILE = 256


def _rms(x, g):
    return x * lax.rsqrt(jnp.mean(x * x, axis=-1, keepdims=True) + EPS) * g


def _sigmoid(x):
    return 1.0 / (1.0 + jnp.exp(-x))


def _log_sigmoid(x):
    return jnp.minimum(x, 0.0) - jnp.log1p(jnp.exp(-jnp.abs(x)))


def _dot(a, b):
    return jnp.dot(a, b, preferred_element_type=F32)


def _dot_nt(a, b):
    return lax.dot_general(a, b, (((1,), (1,)), ((), ())), preferred_element_type=F32)


def _split3(x):
    p1 = x.astype(BF16)
    r1 = x - p1.astype(F32)
    p2 = r1.astype(BF16)
    p3 = (r1 - p2.astype(F32)).astype(BF16)
    return p1, p2, p3


def _resident(shape):
    nd = len(shape)
    return pl.BlockSpec(shape, lambda *_: (0,) * nd, pipeline_mode=pl.Buffered(1))


def _ffn_kernel(x_ref, g_ref, w1_ref, w3_ref, w2_ref, *rest, final_norm):
    if final_norm:
        gf_ref, o_ref, h_ref, u_ref = rest
    else:
        o_ref, h_ref, u_ref = rest
    h_ref[...] = _rms(x_ref[...], g_ref[...]).astype(BF16)
    for c in range(0, u_ref.shape[1], FFN_CHUNK):
        h = h_ref[...]
        a = _dot(h, w1_ref[:, c:c + FFN_CHUNK])
        b = _dot(h, w3_ref[:, c:c + FFN_CHUNK])
        u_ref[:, c:c + FFN_CHUNK] = (a * _sigmoid(a) * b).astype(BF16)
    y = x_ref[...] + 0.5 * _dot(u_ref[...], w2_ref[...])
    if final_norm:
        y = _rms(y, gf_ref[...])
    o_ref[...] = y


def _ffn_weights(w1, w3, w2):
    return w1.astype(BF16), w3.astype(BF16), w2.astype(BF16)


def _ffn(x2d, g, weights, gf=None):
    n, d = x2d.shape
    w1, w3, w2 = weights
    f = w1.shape[1]
    assert f % FFN_CHUNK == 0
    rows = min(FFN_ROWS, n)
    final_norm = gf is not None
    row_spec = pl.BlockSpec((rows, d), lambda i: (i, 0))
    in_specs = [row_spec, _resident((1, d)), _resident(w1.shape), _resident(w3.shape), _resident(w2.shape)]
    args = [x2d, g.reshape(1, d), w1, w3, w2]
    if final_norm:
        in_specs.append(_resident((1, d)))
        args.append(gf.reshape(1, d))
    return pl.pallas_call(
        functools.partial(_ffn_kernel, final_norm=final_norm),
        grid=(n // rows,),
        in_specs=in_specs,
        out_specs=row_spec,
        out_shape=jax.ShapeDtypeStruct((n, d), F32),
        scratch_shapes=[pltpu.VMEM((rows, d), BF16), pltpu.VMEM((rows, f), BF16)],
        compiler_params=pltpu.CompilerParams(dimension_semantics=("arbitrary",),
                                             vmem_limit_bytes=VMEM_LIMIT_BYTES),
        name="ffn_final" if final_norm else "ffn",
    )(*args)


def _mixer_kernel(x_ref, gmix_ref, win_ref, wgt_ref, convw_ref, convb_ref, gbrow_ref, gbcol_ref,
                  tab_ref, normm_ref, wout_ref, k0_ref, v0_ref, c0_ref, n0_ref, m0_ref, conv0_ref,
                  y_ref, ko_ref, vo_ref, co_ref, no_ref, mo_ref, convo_ref,
                  kband, vband, ubuf, c_s, n_s, m_s, mix_s,
                  *, tile, chunk, band_rows, pos0, n_tiles):
    t = pl.program_id(1)

    @pl.when(t == 0)
    def _load_state():
        kband[0:ATT_BAND, :] = k0_ref[0].astype(BF16)
        vband[0:ATT_BAND, :] = v0_ref[0].astype(BF16)
        if band_rows > ATT_BAND + tile:
            pad = band_rows - ATT_BAND - tile
            kband[ATT_BAND + tile:band_rows, :] = jnp.zeros((pad, A_WIDTH), BF16)
            vband[ATT_BAND + tile:band_rows, :] = jnp.zeros((pad, A_WIDTH), BF16)
        ubuf[0:SUBLANES, :] = conv0_ref[0]
        c_s[...] = c0_ref[0]
        n_s[...] = n0_ref[0]
        m_s[...] = m0_ref[0]

    x = x_ref[0]
    h = _rms(x, gmix_ref[...]).astype(BF16)
    z = _dot(h, win_ref[...])

    ak = z[:, OFF_AK:OFF_AV]
    av = z[:, OFF_AV:OFF_MQK]
    ko_ref[0] = ak
    vo_ref[0] = av
    kband[ATT_BAND:ATT_BAND + tile, :] = ak.astype(BF16)
    vband[ATT_BAND:ATT_BAND + tile, :] = av.astype(BF16)
    lane = lax.broadcasted_iota(jnp.int32, (1, LANES), 1)
    even = lane < A_DH
    if pos0 < ATT_BAND:
        col = lax.broadcasted_iota(jnp.int32, (1, band_rows), 1)
        in_stream = col >= (ATT_BAND - pos0) - t * tile
    for grp in range(A_HEADS // 2):
        lo, hi = grp * LANES, (grp + 1) * LANES
        q = z[:, lo:hi] * (A_DH ** -0.5)
        kp = kband[:, lo:hi]
        vp = vband[:, lo:hi]
        out = None
        for par in range(2):
            mine = even if par == 0 else jnp.logical_not(even)
            qh = jnp.where(mine, q, 0.0).astype(BF16)
            s = _dot_nt(qh, kp) + tab_ref[2 * grp + par]
            if pos0 < ATT_BAND:
                s = jnp.where(in_stream, s, NEG)
            e = jnp.exp(s - jnp.max(s, axis=-1, keepdims=True))
            o = _dot(e.astype(BF16), jnp.where(mine, vp, jnp.zeros_like(vp)))
            o = o * (1.0 / jnp.sum(e, axis=-1, keepdims=True))
            out = o if out is None else out + o
        mix_s[:, lo:hi] = out

    ubuf[SUBLANES:SUBLANES + tile, :] = z[:, OFF_MQK:OFF_MV]
    qk = convb_ref[...]
    for j in range(CONV_W):
        start = SUBLANES - (CONV_W - 1) + j
        qk = qk + ubuf[start:start + tile, :] * convw_ref[j:j + 1, :]
    qk = qk * _sigmoid(qk)
    last_rows = ubuf[tile:tile + SUBLANES, :]
    convo_ref[0] = last_rows
    ubuf[0:SUBLANES, :] = last_rows
    mq = qk[:, :M_WIDTH]
    mk = qk[:, M_WIDTH:] * (M_DH ** -0.5)
    mv = z[:, OFF_MV:OFF_MO]

    g_col = z[:, OFF_MG:OFF_MG + LANES] + gbrow_ref[...]
    g_row = _dot_nt(wgt_ref[...], h) + gbcol_ref[...]
    lf_col = _log_sigmoid(g_col)
    lf_row = _log_sigmoid(g_row)
    ri = lax.broadcasted_iota(jnp.int32, (tile, tile), 0)
    ci = lax.broadcasted_iota(jnp.int32, (tile, tile), 1)
    assert chunk & (chunk - 1) == 0
    chunk_shift = chunk.bit_length() - 1
    same_chunk = (ri >> chunk_shift) == (ci >> chunk_shift)
    tri = jnp.where(jnp.logical_and(same_chunk, ci <= ri), 1.0, 0.0).astype(BF16)
    tri_t = jnp.where(jnp.logical_and(same_chunk, ri <= ci), 1.0, 0.0).astype(BF16)
    b_col = sum(_dot(tri, part) for part in _split3(lf_col))
    b_row = sum(_dot(part, tri_t) for part in _split3(lf_row))

    causal = (lax.broadcasted_iota(jnp.int32, (chunk, chunk), 1)
              <= lax.broadcasted_iota(jnp.int32, (chunk, chunk), 0))
    og = _sigmoid(z[:, OFF_MO:OFF_MG])
    for hd in range(M_HEADS):
        lo, hi = hd * M_DH, (hd + 1) * M_DH
        cmat = c_s[hd]
        nrow = n_s[hd:hd + 1, :]
        m_prev = m_s[hd:hd + 1, 0:1]
        h_rows = []
        for c in range(tile // chunk):
            r0, r1 = c * chunk, (c + 1) * chunk
            bc = b_col[r0:r1, M_HEADS + hd:M_HEADS + hd + 1]
            igc = g_col[r0:r1, hd:hd + 1]
            br = b_row[M_HEADS + hd:M_HEADS + hd + 1, r0:r1]
            igr = g_row[hd:hd + 1, r0:r1]
            dmat = jnp.where(causal, bc + (igr - br), NEG)
            inter = bc + m_prev
            mt = jnp.maximum(inter, jnp.max(dmat, axis=-1, keepdims=True))
            dw = jnp.exp(dmat - mt)
            iw = jnp.exp(inter - mt)
            q = mq[r0:r1, lo:hi]
            k = mk[r0:r1, lo:hi]
            vb = mv[r0:r1, lo:hi].astype(BF16)
            qb = q.astype(BF16)
            s = _dot_nt(qb, k.astype(BF16)) * dw
            num = iw * _dot(qb, cmat.astype(BF16)) + _dot(s.astype(BF16), vb)
            den = iw * jnp.sum(q * nrow, axis=-1, keepdims=True) + jnp.sum(s, axis=-1, keepdims=True)
            h_rows.append(num / jnp.maximum(jnp.abs(den), jnp.exp(-mt)))
            b_last = bc[chunk - 1:chunk, :]
            m_new = mt[chunk - 1:chunk, :]
            kw = k * jnp.exp(b_last - bc + igc - m_new)
            decay = jnp.exp(b_last + m_prev - m_new)
            cmat = decay * cmat + _dot(kw.T.astype(BF16), vb)
            nrow = decay * nrow + jnp.sum(kw, axis=0, keepdims=True)
            m_prev = m_new
        c_s[hd] = cmat
        n_s[hd:hd + 1, :] = nrow
        m_s[hd:hd + 1, :] = jnp.broadcast_to(m_prev, (1, LANES))
        mh = jnp.concatenate(h_rows, axis=0) * og[:, lo:hi]
        mix_s[:, A_WIDTH + lo:A_WIDTH + hi] = _rms(mh, normm_ref[:, lo:hi])

    co_ref[0] = c_s[...]
    no_ref[0] = n_s[...]
    mo_ref[0] = m_s[...]
    y_ref[0] = x + _dot(mix_s[...].astype(BF16), wout_ref[...])

    if n_tiles > 1:
        kband[0:ATT_BAND, :] = kband[tile:tile + ATT_BAND, :]
        vband[0:ATT_BAND, :] = vband[tile:tile + ATT_BAND, :]


def _bias_table(rel_bias, tile, band_rows, chunked):
    heads = rel_bias.shape[0]
    span = band_rows + tile - 1
    n_far = ATT_BAND + tile - REL_CLIP
    n_near = span - n_far - (2 * REL_CLIP - 1)
    diag = jnp.concatenate([jnp.broadcast_to(rel_bias[:, 2 * REL_CLIP:], (heads, n_far)),
                            rel_bias[:, 2 * REL_CLIP - 1:0:-1],
                            jnp.broadcast_to(rel_bias[:, :1], (heads, n_near + 1))], axis=1)
    skew = jnp.broadcast_to(diag[:, None, :], (heads, tile, span + 1)).reshape(heads, tile * (span + 1))
    skew = skew[:, :tile * span].reshape(heads, tile, span)
    i = np.arange(tile)[:, None]
    j = np.arange(band_rows)[None, :]
    visible = j < ATT_BAND + tile
    if chunked:
        qc = i // CHUNK
        kc = (j - ATT_BAND) // CHUNK
        visible = visible & (kc <= qc) & (kc >= qc - ATT_BAND // CHUNK)
    return skew[:, :, tile - 1:tile - 1 + band_rows] + jnp.asarray(np.where(visible, 0.0, NEG), F32)


def _mixer(x, params, state, *, tile, chunk, pos0, chunked):
    nb, frames, d = x.shape
    gmix, w_in, conv_w, conv_b, gate_bias, rel_bias, norm_m, w_out = params
    k0, v0, c0, n0, m0, conv0 = state
    n_tiles = frames // tile
    band_rows = ATT_BAND + -(-tile // LANES) * LANES
    keep_tiles = min(ATT_BAND, frames) // tile
    width2 = 2 * M_WIDTH

    win = jnp.concatenate([w_in, jnp.zeros((d, D_IN_PAD - w_in.shape[1]), F32)], axis=1).astype(BF16)
    wgt = jnp.zeros((GATE_ROWS, d), F32).at[:N_GATES].set(w_in[:, OFF_MG:].T).astype(BF16)
    gb_row = jnp.zeros((1, LANES), F32).at[0, :N_GATES].set(gate_bias)
    gb_col = jnp.zeros((GATE_ROWS, 1), F32).at[:N_GATES, 0].set(gate_bias)
    tab = _bias_table(rel_bias, tile, band_rows, chunked)
    conv0p = jnp.concatenate([jnp.zeros((nb, SUBLANES - (CONV_W - 1), width2), F32), conv0], axis=1)
    m0p = jnp.broadcast_to(m0[:, :, None], (nb, M_HEADS, LANES))

    per_stream = lambda *dims: pl.BlockSpec((1,) + dims, lambda b, t: (b,) + (0,) * len(dims))
    tile_spec = pl.BlockSpec((1, tile, d), lambda b, t: (b, t, 0))
    kv_spec = pl.BlockSpec((1, tile, A_WIDTH),
                           lambda b, t: (b, jnp.maximum(t - (n_tiles - keep_tiles), 0), 0))
    in_specs = [
        tile_spec, _resident((1, d)), _resident(win.shape), _resident(wgt.shape),
        _resident((CONV_W, width2)), _resident((1, width2)), _resident((1, LANES)), _resident((GATE_ROWS, 1)),
        _resident(tab.shape), _resident((1, M_WIDTH)), _resident((d, d)),
        per_stream(ATT_BAND, A_WIDTH), per_stream(ATT_BAND, A_WIDTH), per_stream(M_HEADS, M_DH, M_DH),
        per_stream(M_HEADS, M_DH), per_stream(M_HEADS, LANES), per_stream(SUBLANES, width2),
    ]
    out_specs = [
        tile_spec, kv_spec, kv_spec, per_stream(M_HEADS, M_DH, M_DH), per_stream(M_HEADS, M_DH),
        per_stream(M_HEADS, LANES), per_stream(SUBLANES, width2),
    ]
    out_shape = [
        jax.ShapeDtypeStruct((nb, frames, d), F32),
        jax.ShapeDtypeStruct((nb, keep_tiles * tile, A_WIDTH), F32),
        jax.ShapeDtypeStruct((nb, keep_tiles * tile, A_WIDTH), F32),
        jax.ShapeDtypeStruct((nb, M_HEADS, M_DH, M_DH), F32),
        jax.ShapeDtypeStruct((nb, M_HEADS, M_DH), F32),
        jax.ShapeDtypeStruct((nb, M_HEADS, LANES), F32),
        jax.ShapeDtypeStruct((nb, SUBLANES, width2), F32),
    ]
    scratch = [
        pltpu.VMEM((band_rows, A_WIDTH), BF16), pltpu.VMEM((band_rows, A_WIDTH), BF16),
        pltpu.VMEM((tile + SUBLANES, width2), F32),
        pltpu.VMEM((M_HEADS, M_DH, M_DH), F32), pltpu.VMEM((M_HEADS, M_DH), F32), pltpu.VMEM((M_HEADS, LANES), F32),
        pltpu.VMEM((tile, d), F32),
    ]
    y, ko, vo, c1, n1, m1, conv1 = pl.pallas_call(
        functools.partial(_mixer_kernel, tile=tile, chunk=chunk, band_rows=band_rows, pos0=pos0,
                          n_tiles=n_tiles),
        grid=(nb, n_tiles),
        in_specs=in_specs,
        out_specs=out_specs,
        out_shape=out_shape,
        scratch_shapes=scratch,
        compiler_params=pltpu.CompilerParams(dimension_semantics=("arbitrary", "arbitrary"),
                                             vmem_limit_bytes=VMEM_LIMIT_BYTES),
        name="mixer_chunked" if chunked else "mixer_step",
    )(x, gmix.reshape(1, d), win, wgt, conv_w, conv_b.reshape(1, width2), gb_row, gb_col, tab,
      norm_m.reshape(1, M_WIDTH), w_out.astype(BF16), k0, v0, c0, n0, m0p, conv0p)
    keep = min(ATT_BAND, frames)
    new_state = (ko.reshape(nb, keep, A_HEADS, A_DH), vo.reshape(nb, keep, A_HEADS, A_DH), c1, n1,
                 m1[:, :, 0], conv1[:, SUBLANES - (CONV_W - 1):, :])
    return y, new_state


def kernel(x_prompt, x_sample, cache_attn_k, cache_attn_v, state_mlstm_C, state_mlstm_n, state_mlstm_m, state_mlstm_conv, norm_ffn1, w1_ffn1, w3_ffn1, w2_ffn1, norm_mix, w_in, conv_w, conv_b, gate_bias, rel_bias, norm_mlstm_out, w_out, norm_ffn2, w1_ffn2, w3_ffn2, w2_ffn2, norm_final):
    depth = norm_ffn1.shape[0]
    nbp, seq, d = x_prompt.shape
    nbs, dec, _ = x_sample.shape
    xp = x_prompt.reshape(nbp * seq, d)
    xs = x_sample.reshape(nbs * dec, d)
    prompt_tile = min(PROMPT_TILE, seq)
    new_p, new_s = [], []
    for l in range(depth):
        last = l == depth - 1
        ffn1 = _ffn_weights(w1_ffn1[l], w3_ffn1[l], w2_ffn1[l])
        ffn2 = _ffn_weights(w1_ffn2[l], w3_ffn2[l], w2_ffn2[l])
        mix = (norm_mix[l], w_in[l], conv_w[l], conv_b[l], gate_bias[l], rel_bias[l], norm_mlstm_out[l], w_out[l])
        gf = norm_final if last else None

        xp = _ffn(xp, norm_ffn1[l], ffn1)
        zero_state = (jnp.zeros((nbp, ATT_BAND, A_WIDTH), F32), jnp.zeros((nbp, ATT_BAND, A_WIDTH), F32),
                      jnp.zeros((nbp, M_HEADS, M_DH, M_DH), F32), jnp.zeros((nbp, M_HEADS, M_DH), F32),
                      jnp.zeros((nbp, M_HEADS), F32), jnp.zeros((nbp, CONV_W - 1, 2 * M_WIDTH), F32))
        xp3, st = _mixer(xp.reshape(nbp, seq, d), mix, zero_state, tile=prompt_tile, chunk=prompt_tile, pos0=0,
                         chunked=True)
        new_p.append(st)
        xp = _ffn(xp3.reshape(nbp * seq, d), norm_ffn2[l], ffn2, gf)

        xs = _ffn(xs, norm_ffn1[l], ffn1)
        cache = (cache_attn_k[l].reshape(nbs, -1, A_WIDTH), cache_attn_v[l].reshape(nbs, -1, A_WIDTH),
                 state_mlstm_C[l], state_mlstm_n[l], state_mlstm_m[l], state_mlstm_conv[l])
        xs3, st = _mixer(xs.reshape(nbs, dec, d), mix, cache, tile=dec, chunk=dec, pos0=PAST_LEN, chunked=False)
        new_s.append(st)
        xs = _ffn(xs3.reshape(nbs * dec, d), norm_ffn2[l], ffn2, gf)

    stack = lambda states, i: jnp.stack([s[i] for s in states])
    return ((xp.reshape(nbp, seq, d), xs.reshape(nbs, dec, d))
            + tuple(stack(new_p, i) for i in range(6)) + tuple(stack(new_s, i) for i in range(6)))
```

```python
import functools

import numpy as np
import jax
import jax.numpy as jnp
from jax import lax
from jax.experimental import pallas as pl
from jax.experimental.pallas import tpu as pltpu

F32 = jnp.float32
BF16 = jnp.bfloat16

CHUNK = 64
ATT_BAND = 8 * CHUNK
A_HEADS = 8
A_DH = 64
A_WIDTH = A_HEADS * A_DH
M_HEADS = 4
M_DH = 128
M_WIDTH = M_HEADS * M_DH
REL_CLIP = 128
CONV_W = 4
PAST_LEN = 4096
EPS = 1e-6
NEG = -1e30
LOG2E = 1.4426950408889634

LANES = 128
SUBLANES = 8
MXU_DIM = 256
VMEM_LIMIT_BYTES = 56 * 1024 * 1024

OFF_AK = A_WIDTH
OFF_AV = 2 * A_WIDTH
OFF_MQK = 3 * A_WIDTH
OFF_MV = OFF_MQK + 2 * M_WIDTH
OFF_MO = OFF_MV + M_WIDTH
OFF_MG = OFF_MO + M_WIDTH
N_GATES = 2 * M_HEADS
D_IN_PAD = OFF_MG + LANES
GATE_ROWS = 16

FFN_ROWS = 512
FFN_CHUNK = MXU_DIM
PROMPT_TILE = 256


def _rms(x, g):
    return x * lax.rsqrt(jnp.mean(x * x, axis=-1, keepdims=True) + EPS) * g


def _sigmoid(x):
    return 1.0 / (1.0 + jnp.exp(-x))


def _log_sigmoid(x):
    return jnp.minimum(x, 0.0) - jnp.log1p(jnp.exp(-jnp.abs(x)))


def _dot(a, b):
    return jnp.dot(a, b, preferred_element_type=F32)


def _dot_nt(a, b):
    return lax.dot_general(a, b, (((1,), (1,)), ((), ())), preferred_element_type=F32)


def _split3(x):
    p1 = x.astype(BF16)
    r1 = x - p1.astype(F32)
    p2 = r1.astype(BF16)
    p3 = (r1 - p2.astype(F32)).astype(BF16)
    return p1, p2, p3


def _resident(shape):
    nd = len(shape)
    return pl.BlockSpec(shape, lambda *_: (0,) * nd, pipeline_mode=pl.Buffered(1))


def _ffn_kernel(x_ref, g_ref, w1_ref, w3_ref, w2_ref, *rest, final_norm):
    if final_norm:
        gf_ref, o_ref, h_ref, u_ref = rest
    else:
        o_ref, h_ref, u_ref = rest
    h_ref[...] = _rms(x_ref[...], g_ref[...]).astype(BF16)
    for c in range(0, u_ref.shape[1], FFN_CHUNK):
        h = h_ref[...]
        a = _dot(h, w1_ref[:, c:c + FFN_CHUNK])
        b = _dot(h, w3_ref[:, c:c + FFN_CHUNK])
        u_ref[:, c:c + FFN_CHUNK] = (a * _sigmoid(a) * b).astype(BF16)
    y = x_ref[...] + 0.5 * _dot(u_ref[...], w2_ref[...])
    if final_norm:
        y = _rms(y, gf_ref[...])
    o_ref[...] = y


def _ffn_weights(w1, w3, w2):
    return w1.astype(BF16), w3.astype(BF16), w2.astype(BF16)


def _ffn(x2d, g, weights, gf=None):
    n, d = x2d.shape
    w1, w3, w2 = weights
    f = w1.shape[1]
    assert f % FFN_CHUNK == 0
    rows = min(FFN_ROWS, n)
    final_norm = gf is not None
    row_spec = pl.BlockSpec((rows, d), lambda i: (i, 0))
    in_specs = [row_spec, _resident((1, d)), _resident(w1.shape), _resident(w3.shape), _resident(w2.shape)]
    args = [x2d, g.reshape(1, d), w1, w3, w2]
    if final_norm:
        in_specs.append(_resident((1, d)))
        args.append(gf.reshape(1, d))
    return pl.pallas_call(
        functools.partial(_ffn_kernel, final_norm=final_norm),
        grid=(n // rows,),
        in_specs=in_specs,
        out_specs=row_spec,
        out_shape=jax.ShapeDtypeStruct((n, d), F32),
        scratch_shapes=[pltpu.VMEM((rows, d), BF16), pltpu.VMEM((rows, f), BF16)],
        compiler_params=pltpu.CompilerParams(dimension_semantics=("arbitrary",),
                                             vmem_limit_bytes=VMEM_LIMIT_BYTES),
        name="ffn_final" if final_norm else "ffn",
    )(*args)


def _mixer_kernel(xa_ref, xb_ref, gmix_ref, win_ref, wgt_ref, convw_ref, convb_ref, gbrow_ref, gbcol_ref,
                  tab_ref, normm_ref, wout_ref, k0_ref, v0_ref, c0_ref, n0_ref, m0_ref, conv0_ref,
                  y_ref, ko_ref, vo_ref, co_ref, no_ref, mo_ref, convo_ref,
                  z_new, z_cur, g_new, g_cur, kband, vband, ubuf, c_s, n_s, m_s, mix_s,
                  *, tile, chunk, band_rows, pos0, n_tiles):
    p = pl.program_id(0)
    t = lax.rem(jnp.maximum(p - 1, 0), n_tiles)

    @pl.when(p == 0)
    def _first_step():
        z_cur[...] = jnp.zeros_like(z_cur)
        g_cur[...] = jnp.zeros_like(g_cur)
        kband[ATT_BAND:band_rows, :] = jnp.zeros((band_rows - ATT_BAND, A_WIDTH), BF16)
        vband[ATT_BAND:band_rows, :] = jnp.zeros((band_rows - ATT_BAND, A_WIDTH), BF16)
        ubuf[SUBLANES:SUBLANES + tile, :] = jnp.zeros((tile, 2 * M_WIDTH), F32)

    @pl.when(t == 0)
    def _load_state():
        kband[0:ATT_BAND, :] = k0_ref[0].astype(BF16)
        vband[0:ATT_BAND, :] = v0_ref[0].astype(BF16)
        ubuf[0:SUBLANES, :] = conv0_ref[0]
        c_s[...] = c0_ref[0]
        n_s[...] = n0_ref[0]
        m_s[...] = m0_ref[0]

    h = _rms(xa_ref[0], gmix_ref[...]).astype(BF16)
    g_new[...] = _dot_nt(wgt_ref[...], h)
    slabs = iter(range(0, D_IN_PAD, MXU_DIM))

    def project_next(count=1):
        for _ in range(count):
            c0 = next(slabs, None)
            if c0 is not None:
                c1 = min(c0 + MXU_DIM, D_IN_PAD)
                z_new[:, c0:c1] = _dot(h, win_ref[:, c0:c1])

    lane = lax.broadcasted_iota(jnp.int32, (1, LANES), 1)
    even = lane < A_DH
    if pos0 < ATT_BAND:
        col = lax.broadcasted_iota(jnp.int32, (1, band_rows), 1)
        in_stream = col >= (ATT_BAND - pos0) - t * tile
    def scores(head):
        lo = (head // 2) * LANES
        mine = even if head % 2 == 0 else jnp.logical_not(even)
        qh = jnp.where(mine, z_cur[:, lo:lo + LANES] * (LOG2E * A_DH ** -0.5), 0.0).astype(BF16)
        s = _dot_nt(qh, kband[:, lo:lo + LANES]) + tab_ref[head]
        if pos0 < ATT_BAND:
            s = jnp.where(in_stream, s, NEG)
        return s

    def attend(head, s):
        lo = (head // 2) * LANES
        mine = even if head % 2 == 0 else jnp.logical_not(even)
        vp = vband[:, lo:lo + LANES]
        e = jnp.exp2(s - jnp.max(s, axis=-1, keepdims=True))
        o = _dot(e.astype(BF16), jnp.where(mine, vp, jnp.zeros_like(vp)))
        o = o * (1.0 / jnp.sum(e, axis=-1, keepdims=True))
        if head % 2 == 0:
            mix_s[:, lo:lo + LANES] = o
        else:
            mix_s[:, lo:lo + LANES] += o

    pending = None
    for head in range(A_HEADS):
        project_next()
        s = scores(head)
        if pending is not None:
            attend(*pending)
        pending = (head, s)
    attend(*pending)

    project_next()
    qk = convb_ref[...]
    for j in range(CONV_W):
        start = SUBLANES - (CONV_W - 1) + j
        qk = qk + ubuf[start:start + tile, :] * convw_ref[j:j + 1, :]
    qk = qk * _sigmoid(qk)
    last_rows = ubuf[tile:tile + SUBLANES, :]
    convo_ref[0] = last_rows
    mq = qk[:, :M_WIDTH]
    mk = qk[:, M_WIDTH:] * (M_DH ** -0.5)

    project_next()
    g_col = z_cur[:, OFF_MG:OFF_MG + LANES] + gbrow_ref[...]
    g_row = g_cur[...] + gbcol_ref[...]
    lf_col = _log_sigmoid(g_col)
    lf_row = _log_sigmoid(g_row)
    ri = lax.broadcasted_iota(jnp.int32, (tile, tile), 0)
    ci = lax.broadcasted_iota(jnp.int32, (tile, tile), 1)
    assert chunk & (chunk - 1) == 0
    chunk_shift = chunk.bit_length() - 1
    same_chunk = (ri >> chunk_shift) == (ci >> chunk_shift)
    tri = jnp.where(jnp.logical_and(same_chunk, ci <= ri), 1.0, 0.0).astype(BF16)
    tri_t = jnp.where(jnp.logical_and(same_chunk, ri <= ci), 1.0, 0.0).astype(BF16)
    b_col = sum(_dot(tri, part) for part in _split3(lf_col))
    b_row = sum(_dot(part, tri_t) for part in _split3(lf_row))

    causal = (lax.broadcasted_iota(jnp.int32, (chunk, chunk), 1)
              <= lax.broadcasted_iota(jnp.int32, (chunk, chunk), 0))
    for hd in range(M_HEADS):
        lo, hi = hd * M_DH, (hd + 1) * M_DH
        project_next()
        cmat = c_s[hd]
        nrow = n_s[hd:hd + 1, :]
        m_prev = m_s[hd:hd + 1, 0:1]
        h_rows = []
        for c in range(tile // chunk):
            r0, r1 = c * chunk, (c + 1) * chunk
            bc = b_col[r0:r1, M_HEADS + hd:M_HEADS + hd + 1]
            igc = g_col[r0:r1, hd:hd + 1]
            br = b_row[M_HEADS + hd:M_HEADS + hd + 1, r0:r1]
            igr = g_row[hd:hd + 1, r0:r1]
            dmat = jnp.where(causal, bc + (igr - br), NEG)
            inter = bc + m_prev
            mt = jnp.maximum(inter, jnp.max(dmat, axis=-1, keepdims=True))
            dw = jnp.exp(dmat - mt)
            iw = jnp.exp(inter - mt)
            q = mq[r0:r1, lo:hi]
            k = mk[r0:r1, lo:hi]
            vb = z_cur[r0:r1, OFF_MV + lo:OFF_MV + hi].astype(BF16)
            qb = q.astype(BF16)
            s = _dot_nt(qb, k.astype(BF16)) * dw
            num = iw * _dot(qb, cmat.astype(BF16)) + _dot(s.astype(BF16), vb)
            den = iw * jnp.sum(q * nrow, axis=-1, keepdims=True) + jnp.sum(s, axis=-1, keepdims=True)
            h_rows.append(num / jnp.maximum(jnp.abs(den), jnp.exp(-mt)))
            b_last = bc[chunk - 1:chunk, :]
            m_new = mt[chunk - 1:chunk, :]
            kw = k * jnp.exp(b_last - bc + igc - m_new)
            decay = jnp.exp(b_last + m_prev - m_new)
            cmat = decay * cmat + _dot(kw.T.astype(BF16), vb)
            nrow = decay * nrow + jnp.sum(kw, axis=0, keepdims=True)
            m_prev = m_new
        c_s[hd] = cmat
        n_s[hd:hd + 1, :] = nrow
        m_s[hd:hd + 1, :] = jnp.broadcast_to(m_prev, (1, LANES))
        mh = jnp.concatenate(h_rows, axis=0) * _sigmoid(z_cur[:, OFF_MO + lo:OFF_MO + hi])
        mix_s[:, A_WIDTH + lo:A_WIDTH + hi] = _rms(mh, normm_ref[:, lo:hi])

    project_next(D_IN_PAD // MXU_DIM + 1)
    co_ref[0] = c_s[...]
    no_ref[0] = n_s[...]
    mo_ref[0] = m_s[...]
    y_ref[0] = xb_ref[0] + _dot(mix_s[...].astype(BF16), wout_ref[...])

    ko_ref[0] = z_new[:, OFF_AK:OFF_AV]
    vo_ref[0] = z_new[:, OFF_AV:OFF_MQK]
    kband[0:ATT_BAND, :] = kband[tile:tile + ATT_BAND, :]
    vband[0:ATT_BAND, :] = vband[tile:tile + ATT_BAND, :]
    kband[ATT_BAND:ATT_BAND + tile, :] = z_new[:, OFF_AK:OFF_AV].astype(BF16)
    vband[ATT_BAND:ATT_BAND + tile, :] = z_new[:, OFF_AV:OFF_MQK].astype(BF16)
    ubuf[0:SUBLANES, :] = last_rows
    ubuf[SUBLANES:SUBLANES + tile, :] = z_new[:, OFF_MQK:OFF_MV]
    z_cur[:, 0:OFF_AK] = z_new[:, 0:OFF_AK]
    z_cur[:, OFF_MV:D_IN_PAD] = z_new[:, OFF_MV:D_IN_PAD]
    g_cur[...] = g_new[...]


def _bias_table(rel_bias, tile, band_rows, chunked):
    heads = rel_bias.shape[0]
    span = band_rows + tile - 1
    n_far = ATT_BAND + tile - REL_CLIP
    n_near = span - n_far - (2 * REL_CLIP - 1)
    diag = jnp.concatenate([jnp.broadcast_to(rel_bias[:, 2 * REL_CLIP:], (heads, n_far)),
                            rel_bias[:, 2 * REL_CLIP - 1:0:-1],
                            jnp.broadcast_to(rel_bias[:, :1], (heads, n_near + 1))], axis=1)
    skew = jnp.broadcast_to(diag[:, None, :], (heads, tile, span + 1)).reshape(heads, tile * (span + 1))
    skew = skew[:, :tile * span].reshape(heads, tile, span)
    i = np.arange(tile)[:, None]
    j = np.arange(band_rows)[None, :]
    visible = j < ATT_BAND + tile
    if chunked:
        qc = i // CHUNK
        kc = (j - ATT_BAND) // CHUNK
        visible = visible & (kc <= qc) & (kc >= qc - ATT_BAND // CHUNK)
    tab = skew[:, :, tile - 1:tile - 1 + band_rows] + jnp.asarray(np.where(visible, 0.0, NEG), F32)
    return tab * LOG2E


def _mixer(x, params, state, *, tile, chunk, pos0, chunked):
    nb, frames, d = x.shape
    gmix, w_in, conv_w, conv_b, gate_bias, rel_bias, norm_m, w_out = params
    k0, v0, c0, n0, m0, conv0 = state
    n_tiles = frames // tile
    band_rows = ATT_BAND + -(-tile // LANES) * LANES
    keep_tiles = min(ATT_BAND, frames) // tile
    width2 = 2 * M_WIDTH

    win = jnp.concatenate([w_in, jnp.zeros((d, D_IN_PAD - w_in.shape[1]), F32)], axis=1).astype(BF16)
    wgt = jnp.zeros((GATE_ROWS, d), F32).at[:N_GATES].set(w_in[:, OFF_MG:].T).astype(BF16)
    gb_row = jnp.zeros((1, LANES), F32).at[0, :N_GATES].set(gate_bias)
    gb_col = jnp.zeros((GATE_ROWS, 1), F32).at[:N_GATES, 0].set(gate_bias)
    tab = _bias_table(rel_bias, tile, band_rows, chunked)
    conv0p = jnp.concatenate([jnp.zeros((nb, SUBLANES - (CONV_W - 1), width2), F32), conv0], axis=1)
    m0p = jnp.broadcast_to(m0[:, :, None], (nb, M_HEADS, LANES))

    n_steps = nb * n_tiles + 1
    split = lambda q: (lax.div(q, n_tiles), lax.rem(q, n_tiles))
    proj = lambda p: split(jnp.minimum(p, n_steps - 2))
    mixd = lambda p: split(jnp.maximum(p - 1, 0))
    per_stream = lambda *dims: pl.BlockSpec((1,) + dims, lambda p: (mixd(p)[0],) + (0,) * len(dims))
    proj_spec = pl.BlockSpec((1, tile, d), lambda p: proj(p) + (0,))
    mix_spec = pl.BlockSpec((1, tile, d), lambda p: mixd(p) + (0,))
    kv_spec = pl.BlockSpec(
        (1, tile, A_WIDTH),
        lambda p: (proj(p)[0], jnp.maximum(proj(p)[1] - (n_tiles - keep_tiles), 0), 0))
    in_specs = [
        proj_spec, mix_spec, _resident((1, d)), _resident(win.shape), _resident(wgt.shape),
        _resident((CONV_W, width2)), _resident((1, width2)), _resident((1, LANES)), _resident((GATE_ROWS, 1)),
        _resident(tab.shape), _resident((1, M_WIDTH)), _resident((d, d)),
        per_stream(ATT_BAND, A_WIDTH), per_stream(ATT_BAND, A_WIDTH), per_stream(M_HEADS, M_DH, M_DH),
        per_stream(M_HEADS, M_DH), per_stream(M_HEADS, LANES), per_stream(SUBLANES, width2),
    ]
    out_specs = [
        mix_spec, kv_spec, kv_spec, per_stream(M_HEADS, M_DH, M_DH), per_stream(M_HEADS, M_DH),
        per_stream(M_HEADS, LANES), per_stream(SUBLANES, width2),
    ]
    out_shape = [
        jax.ShapeDtypeStruct((nb, frames, d), F32),
        jax.ShapeDtypeStruct((nb, keep_tiles * tile, A_WIDTH), F32),
        jax.ShapeDtypeStruct((nb, keep_tiles * tile, A_WIDTH), F32),
        jax.ShapeDtypeStruct((nb, M_HEADS, M_DH, M_DH), F32),
        jax.ShapeDtypeStruct((nb, M_HEADS, M_DH), F32),
        jax.ShapeDtypeStruct((nb, M_HEADS, LANES), F32),
        jax.ShapeDtypeStruct((nb, SUBLANES, width2), F32),
    ]
    scratch = [
        pltpu.VMEM((tile, D_IN_PAD), F32), pltpu.VMEM((tile, D_IN_PAD), F32),
        pltpu.VMEM((GATE_ROWS, tile), F32), pltpu.VMEM((GATE_ROWS, tile), F32),
        pltpu.VMEM((band_rows, A_WIDTH), BF16), pltpu.VMEM((band_rows, A_WIDTH), BF16),
        pltpu.VMEM((tile + SUBLANES, width2), F32),
        pltpu.VMEM((M_HEADS, M_DH, M_DH), F32), pltpu.VMEM((M_HEADS, M_DH), F32), pltpu.VMEM((M_HEADS, LANES), F32),
        pltpu.VMEM((tile, d), F32),
    ]
    y, ko, vo, c1, n1, m1, conv1 = pl.pallas_call(
        functools.partial(_mixer_kernel, tile=tile, chunk=chunk, band_rows=band_rows, pos0=pos0,
                          n_tiles=n_tiles),
        grid=(n_steps,),
        in_specs=in_specs,
        out_specs=out_specs,
        out_shape=out_shape,
        scratch_shapes=scratch,
        compiler_params=pltpu.CompilerParams(dimension_semantics=("arbitrary",),
                                             vmem_limit_bytes=VMEM_LIMIT_BYTES),
        name="mixer_chunked" if chunked else "mixer_step",
    )(x, x, gmix.reshape(1, d), win, wgt, conv_w, conv_b.reshape(1, width2), gb_row, gb_col, tab,
      norm_m.reshape(1, M_WIDTH), w_out.astype(BF16), k0, v0, c0, n0, m0p, conv0p)
    keep = min(ATT_BAND, frames)
    new_state = (ko.reshape(nb, keep, A_HEADS, A_DH), vo.reshape(nb, keep, A_HEADS, A_DH), c1, n1,
                 m1[:, :, 0], conv1[:, SUBLANES - (CONV_W - 1):, :])
    return y, new_state


def kernel(x_prompt, x_sample, cache_attn_k, cache_attn_v, state_mlstm_C, state_mlstm_n, state_mlstm_m, state_mlstm_conv, norm_ffn1, w1_ffn1, w3_ffn1, w2_ffn1, norm_mix, w_in, conv_w, conv_b, gate_bias, rel_bias, norm_mlstm_out, w_out, norm_ffn2, w1_ffn2, w3_ffn2, w2_ffn2, norm_final):
    depth = norm_ffn1.shape[0]
    nbp, seq, d = x_prompt.shape
    nbs, dec, _ = x_sample.shape
    xp = x_prompt.reshape(nbp * seq, d)
    xs = x_sample.reshape(nbs * dec, d)
    prompt_tile = min(PROMPT_TILE, seq)
    new_p, new_s = [], []
    for l in range(depth):
        last = l == depth - 1
        ffn1 = _ffn_weights(w1_ffn1[l], w3_ffn1[l], w2_ffn1[l])
        ffn2 = _ffn_weights(w1_ffn2[l], w3_ffn2[l], w2_ffn2[l])
        mix = (norm_mix[l], w_in[l], conv_w[l], conv_b[l], gate_bias[l], rel_bias[l], norm_mlstm_out[l], w_out[l])
        gf = norm_final if last else None

        xp = _ffn(xp, norm_ffn1[l], ffn1)
        zero_state = (jnp.zeros((nbp, ATT_BAND, A_WIDTH), F32), jnp.zeros((nbp, ATT_BAND, A_WIDTH), F32),
                      jnp.zeros((nbp, M_HEADS, M_DH, M_DH), F32), jnp.zeros((nbp, M_HEADS, M_DH), F32),
                      jnp.zeros((nbp, M_HEADS), F32), jnp.zeros((nbp, CONV_W - 1, 2 * M_WIDTH), F32))
        xp3, st = _mixer(xp.reshape(nbp, seq, d), mix, zero_state, tile=prompt_tile, chunk=prompt_tile, pos0=0,
                         chunked=True)
        new_p.append(st)
        xp = _ffn(xp3.reshape(nbp * seq, d), norm_ffn2[l], ffn2, gf)

        xs = _ffn(xs, norm_ffn1[l], ffn1)
        cache = (cache_attn_k[l].reshape(nbs, -1, A_WIDTH), cache_attn_v[l].reshape(nbs, -1, A_WIDTH),
                 state_mlstm_C[l], state_mlstm_n[l], state_mlstm_m[l], state_mlstm_conv[l])
        xs3, st = _mixer(xs.reshape(nbs, dec, d), mix, cache, tile=dec, chunk=dec, pos0=PAST_LEN, chunked=False)
        new_s.append(st)
        xs = _ffn(xs3.reshape(nbs * dec, d), norm_ffn2[l], ffn2, gf)

    stack = lambda states, i: jnp.stack([s[i] for s in states])
    return ((xp.reshape(nbp, seq, d), xs.reshape(nbs, dec, d))
            + tuple(stack(new_p, i) for i in range(6)) + tuple(stack(new_s, i) for i in range(6)))
```

```python
import functools

import numpy as np
import jax
import jax.numpy as jnp
from jax import lax
from jax.experimental import pallas as pl
from jax.experimental.pallas import tpu as pltpu

F32 = jnp.float32
BF16 = jnp.bfloat16

CHUNK = 64
ATT_BAND = 8 * CHUNK
A_HEADS = 8
A_DH = 64
A_WIDTH = A_HEADS * A_DH
M_HEADS = 4
M_DH = 128
M_WIDTH = M_HEADS * M_DH
REL_CLIP = 128
CONV_W = 4
PAST_LEN = 4096
EPS = 1e-6
NEG = -1e30
LOG2E = 1.4426950408889634

LANES = 128
SUBLANES = 8
MXU_DIM = 256
VMEM_LIMIT_BYTES = 60 * 1024 * 1024

OFF_AK = A_WIDTH
OFF_AV = 2 * A_WIDTH
OFF_MQK = 3 * A_WIDTH
OFF_MV = OFF_MQK + 2 * M_WIDTH
OFF_MO = OFF_MV + M_WIDTH
OFF_MG = OFF_MO + M_WIDTH
N_GATES = 2 * M_HEADS
D_IN_PAD = OFF_MG + LANES
GATE_ROWS = 16

FFN_ROWS = 512
FFN_CHUNK = MXU_DIM
PROMPT_TILE = 256


def _rms(x, g):
    return x * lax.rsqrt(jnp.mean(x * x, axis=-1, keepdims=True) + EPS) * g


def _sigmoid(x):
    return 1.0 / (1.0 + jnp.exp(-x))


def _log_sigmoid(x):
    return jnp.minimum(x, 0.0) - jnp.log1p(jnp.exp(-jnp.abs(x)))


def _dot(a, b):
    return jnp.dot(a, b, preferred_element_type=F32)


def _dot_nt(a, b):
    return lax.dot_general(a, b, (((1,), (1,)), ((), ())), preferred_element_type=F32)


def _split3(x):
    p1 = x.astype(BF16)
    r1 = x - p1.astype(F32)
    p2 = r1.astype(BF16)
    p3 = (r1 - p2.astype(F32)).astype(BF16)
    return p1, p2, p3


def _resident(shape):
    nd = len(shape)
    return pl.BlockSpec(shape, lambda *_: (0,) * nd, pipeline_mode=pl.Buffered(1))


def _ffn_kernel(x_ref, g_ref, w1_ref, w3_ref, w2_ref, *rest, final_norm):
    if final_norm:
        gf_ref, o_ref, h_ref, u_ref = rest
    else:
        o_ref, h_ref, u_ref = rest
    h_ref[...] = _rms(x_ref[...], g_ref[...]).astype(BF16)
    for c in range(0, u_ref.shape[1], FFN_CHUNK):
        h = h_ref[...]
        a = _dot(h, w1_ref[:, c:c + FFN_CHUNK])
        b = _dot(h, w3_ref[:, c:c + FFN_CHUNK])
        u_ref[:, c:c + FFN_CHUNK] = (a * _sigmoid(a) * b).astype(BF16)
    y = x_ref[...] + 0.5 * _dot(u_ref[...], w2_ref[...])
    if final_norm:
        y = _rms(y, gf_ref[...])
    o_ref[...] = y


def _ffn_weights(w1, w3, w2):
    return w1.astype(BF16), w3.astype(BF16), w2.astype(BF16)


def _ffn(x2d, g, weights, gf=None):
    n, d = x2d.shape
    w1, w3, w2 = weights
    f = w1.shape[1]
    assert f % FFN_CHUNK == 0
    rows = min(FFN_ROWS, n)
    final_norm = gf is not None
    row_spec = pl.BlockSpec((rows, d), lambda i: (i, 0))
    in_specs = [row_spec, _resident((1, d)), _resident(w1.shape), _resident(w3.shape), _resident(w2.shape)]
    args = [x2d, g.reshape(1, d), w1, w3, w2]
    if final_norm:
        in_specs.append(_resident((1, d)))
        args.append(gf.reshape(1, d))
    return pl.pallas_call(
        functools.partial(_ffn_kernel, final_norm=final_norm),
        grid=(n // rows,),
        in_specs=in_specs,
        out_specs=row_spec,
        out_shape=jax.ShapeDtypeStruct((n, d), F32),
        scratch_shapes=[pltpu.VMEM((rows, d), BF16), pltpu.VMEM((rows, f), BF16)],
        compiler_params=pltpu.CompilerParams(dimension_semantics=("arbitrary",),
                                             vmem_limit_bytes=VMEM_LIMIT_BYTES),
        name="ffn_final" if final_norm else "ffn",
    )(*args)


def _mixer_kernel(*refs, tile, chunk, band_rows, pos0, n_tiles, fused_ffn, has_state):
    refs = iter(refs)
    take = lambda n: [next(refs) for _ in range(n)]
    (xa_ref,) = take(1)
    if fused_ffn:
        g1_ref, w1_ref, w3_ref, w2_ref = take(4)
    else:
        (xb_ref,) = take(1)
    gmix_ref, win_ref, wgt_ref, convw_ref, convb_ref, gbrow_ref, gbcol_ref, tab_ref, normm_ref, wout_ref = take(10)
    if has_state:
        k0_ref, v0_ref, c0_ref, n0_ref, m0_ref, conv0_ref = take(6)
    y_ref, ko_ref, vo_ref, co_ref, no_ref, mo_ref, convo_ref = take(7)
    z_new, z_cur, g_new, g_cur, kband, vband, ubuf, c_s, n_s, m_s, mix_s, h_s = take(12)
    if fused_ffn:
        u_s, x1_new, x1_cur = take(3)

    p = pl.program_id(0)
    t = lax.rem(jnp.maximum(p - 1, 0), n_tiles)

    @pl.when(p == 0)
    def _first_step():
        z_cur[...] = jnp.zeros_like(z_cur)
        g_cur[...] = jnp.zeros_like(g_cur)
        kband[ATT_BAND:band_rows, :] = jnp.zeros((band_rows - ATT_BAND, A_WIDTH), BF16)
        vband[ATT_BAND:band_rows, :] = jnp.zeros((band_rows - ATT_BAND, A_WIDTH), BF16)
        ubuf[SUBLANES:SUBLANES + tile, :] = jnp.zeros((tile, 2 * M_WIDTH), F32)
        if fused_ffn:
            x1_cur[...] = jnp.zeros_like(x1_cur)

    @pl.when(t == 0)
    def _load_state():
        if has_state:
            kband[0:ATT_BAND, :] = k0_ref[0].astype(BF16)
            vband[0:ATT_BAND, :] = v0_ref[0].astype(BF16)
            ubuf[0:SUBLANES, :] = conv0_ref[0]
            c_s[...] = c0_ref[0]
            n_s[...] = n0_ref[0]
            m_s[...] = m0_ref[0]
        else:
            kband[0:ATT_BAND, :] = jnp.zeros((ATT_BAND, A_WIDTH), BF16)
            vband[0:ATT_BAND, :] = jnp.zeros((ATT_BAND, A_WIDTH), BF16)
            ubuf[0:SUBLANES, :] = jnp.zeros((SUBLANES, 2 * M_WIDTH), F32)
            c_s[...] = jnp.zeros_like(c_s)
            n_s[...] = jnp.zeros_like(n_s)
            m_s[...] = jnp.zeros_like(m_s)

    def stage_a():
        if fused_ffn:
            h_s[...] = _rms(xa_ref[0], g1_ref[...]).astype(BF16)
            yield
            for c in range(0, u_s.shape[1], FFN_CHUNK):
                hf = h_s[...]
                a = _dot(hf, w1_ref[:, c:c + FFN_CHUNK])
                b = _dot(hf, w3_ref[:, c:c + FFN_CHUNK])
                u_s[:, c:c + FFN_CHUNK] = (a * _sigmoid(a) * b).astype(BF16)
                yield
            for c in range(0, x1_new.shape[1], MXU_DIM):
                x1_new[:, c:c + MXU_DIM] = (xa_ref[0, :, c:c + MXU_DIM]
                                            + 0.5 * _dot(u_s[...], w2_ref[:, c:c + MXU_DIM]))
                yield
            h_s[...] = _rms(x1_new[...], gmix_ref[...]).astype(BF16)
        else:
            h_s[...] = _rms(xa_ref[0], gmix_ref[...]).astype(BF16)
        g_new[...] = _dot_nt(wgt_ref[...], h_s[...])
        yield
        for c in range(0, D_IN_PAD, MXU_DIM):
            c1 = min(c + MXU_DIM, D_IN_PAD)
            z_new[:, c:c1] = _dot(h_s[...], win_ref[:, c:c1])
            yield

    units = stage_a()

    def project_next(count=1):
        for _ in range(count):
            next(units, None)

    lane = lax.broadcasted_iota(jnp.int32, (1, LANES), 1)
    even = lane < A_DH
    if pos0 < ATT_BAND:
        col = lax.broadcasted_iota(jnp.int32, (1, band_rows), 1)
        in_stream = col >= (ATT_BAND - pos0) - t * tile
    def scores(head):
        lo = (head // 2) * LANES
        mine = even if head % 2 == 0 else jnp.logical_not(even)
        qh = jnp.where(mine, z_cur[:, lo:lo + LANES] * (LOG2E * A_DH ** -0.5), 0.0).astype(BF16)
        s = _dot_nt(qh, kband[:, lo:lo + LANES]) + tab_ref[head]
        if pos0 < ATT_BAND:
            s = jnp.where(in_stream, s, NEG)
        return s

    def attend(head, s):
        lo = (head // 2) * LANES
        mine = even if head % 2 == 0 else jnp.logical_not(even)
        vp = vband[:, lo:lo + LANES]
        e = jnp.exp2(s - jnp.max(s, axis=-1, keepdims=True))
        o = _dot(e.astype(BF16), jnp.where(mine, vp, jnp.zeros_like(vp)))
        o = o * (1.0 / jnp.sum(e, axis=-1, keepdims=True))
        if head % 2 == 0:
            mix_s[:, lo:lo + LANES] = o
        else:
            mix_s[:, lo:lo + LANES] += o

    pending = None
    for head in range(A_HEADS):
        project_next()
        s = scores(head)
        if pending is not None:
            attend(*pending)
        pending = (head, s)
    attend(*pending)

    project_next(2)
    qk = convb_ref[...]
    for j in range(CONV_W):
        start = SUBLANES - (CONV_W - 1) + j
        qk = qk + ubuf[start:start + tile, :] * convw_ref[j:j + 1, :]
    qk = qk * _sigmoid(qk)
    last_rows = ubuf[tile:tile + SUBLANES, :]
    convo_ref[0] = last_rows
    mq = qk[:, :M_WIDTH]
    mk = qk[:, M_WIDTH:] * (M_DH ** -0.5)

    project_next(2)
    g_col = z_cur[:, OFF_MG:OFF_MG + LANES] + gbrow_ref[...]
    g_row = g_cur[...] + gbcol_ref[...]
    lf_col = _log_sigmoid(g_col)
    lf_row = _log_sigmoid(g_row)
    ri = lax.broadcasted_iota(jnp.int32, (tile, tile), 0)
    ci = lax.broadcasted_iota(jnp.int32, (tile, tile), 1)
    assert chunk & (chunk - 1) == 0
    chunk_shift = chunk.bit_length() - 1
    same_chunk = (ri >> chunk_shift) == (ci >> chunk_shift)
    tri = jnp.where(jnp.logical_and(same_chunk, ci <= ri), 1.0, 0.0).astype(BF16)
    tri_t = jnp.where(jnp.logical_and(same_chunk, ri <= ci), 1.0, 0.0).astype(BF16)
    b_col = sum(_dot(tri, part) for part in _split3(lf_col))
    b_row = sum(_dot(part, tri_t) for part in _split3(lf_row))

    causal = (lax.broadcasted_iota(jnp.int32, (chunk, chunk), 1)
              <= lax.broadcasted_iota(jnp.int32, (chunk, chunk), 0))
    for hd in range(M_HEADS):
        lo, hi = hd * M_DH, (hd + 1) * M_DH
        project_next(3)
        cmat = c_s[hd]
        nrow = n_s[hd:hd + 1, :]
        m_prev = m_s[hd:hd + 1, 0:1]
        h_rows = []
        for c in range(tile // chunk):
            r0, r1 = c * chunk, (c + 1) * chunk
            bc = b_col[r0:r1, M_HEADS + hd:M_HEADS + hd + 1]
            igc = g_col[r0:r1, hd:hd + 1]
            br = b_row[M_HEADS + hd:M_HEADS + hd + 1, r0:r1]
            igr = g_row[hd:hd + 1, r0:r1]
            dmat = jnp.where(causal, bc + (igr - br), NEG)
            inter = bc + m_prev
            mt = jnp.maximum(inter, jnp.max(dmat, axis=-1, keepdims=True))
            dw = jnp.exp(dmat - mt)
            iw = jnp.exp(inter - mt)
            q = mq[r0:r1, lo:hi]
            k = mk[r0:r1, lo:hi]
            vb = z_cur[r0:r1, OFF_MV + lo:OFF_MV + hi].astype(BF16)
            qb = q.astype(BF16)
            s = _dot_nt(qb, k.astype(BF16)) * dw
            num = iw * _dot(qb, cmat.astype(BF16)) + _dot(s.astype(BF16), vb)
            den = iw * jnp.sum(q * nrow, axis=-1, keepdims=True) + jnp.sum(s, axis=-1, keepdims=True)
            h_rows.append(num / jnp.maximum(jnp.abs(den), jnp.exp(-mt)))
            b_last = bc[chunk - 1:chunk, :]
            m_new = mt[chunk - 1:chunk, :]
            kw = k * jnp.exp(b_last - bc + igc - m_new)
            decay = jnp.exp(b_last + m_prev - m_new)
            cmat = decay * cmat + _dot(kw.T.astype(BF16), vb)
            nrow = decay * nrow + jnp.sum(kw, axis=0, keepdims=True)
            m_prev = m_new
        c_s[hd] = cmat
        n_s[hd:hd + 1, :] = nrow
        m_s[hd:hd + 1, :] = jnp.broadcast_to(m_prev, (1, LANES))
        mh = jnp.concatenate(h_rows, axis=0) * _sigmoid(z_cur[:, OFF_MO + lo:OFF_MO + hi])
        mix_s[:, A_WIDTH + lo:A_WIDTH + hi] = _rms(mh, normm_ref[:, lo:hi])

    for _ in units:
        pass
    co_ref[0] = c_s[...]
    no_ref[0] = n_s[...]
    mo_ref[0] = m_s[...]
    resid = x1_cur[...] if fused_ffn else xb_ref[0]
    y_ref[0] = resid + _dot(mix_s[...].astype(BF16), wout_ref[...])

    if fused_ffn:
        x1_cur[...] = x1_new[...]
    ko_ref[0] = z_new[:, OFF_AK:OFF_AV]
    vo_ref[0] = z_new[:, OFF_AV:OFF_MQK]
    kband[0:ATT_BAND, :] = kband[tile:tile + ATT_BAND, :]
    vband[0:ATT_BAND, :] = vband[tile:tile + ATT_BAND, :]
    kband[ATT_BAND:ATT_BAND + tile, :] = z_new[:, OFF_AK:OFF_AV].astype(BF16)
    vband[ATT_BAND:ATT_BAND + tile, :] = z_new[:, OFF_AV:OFF_MQK].astype(BF16)
    ubuf[0:SUBLANES, :] = last_rows
    ubuf[SUBLANES:SUBLANES + tile, :] = z_new[:, OFF_MQK:OFF_MV]
    z_cur[:, 0:OFF_AK] = z_new[:, 0:OFF_AK]
    z_cur[:, OFF_MV:D_IN_PAD] = z_new[:, OFF_MV:D_IN_PAD]
    g_cur[...] = g_new[...]


def _bias_table(rel_bias, tile, band_rows, chunked):
    heads = rel_bias.shape[0]
    span = band_rows + tile - 1
    n_far = ATT_BAND + tile - REL_CLIP
    n_near = span - n_far - (2 * REL_CLIP - 1)
    diag = jnp.concatenate([jnp.broadcast_to(rel_bias[:, 2 * REL_CLIP:], (heads, n_far)),
                            rel_bias[:, 2 * REL_CLIP - 1:0:-1],
                            jnp.broadcast_to(rel_bias[:, :1], (heads, n_near + 1))], axis=1)
    skew = jnp.broadcast_to(diag[:, None, :], (heads, tile, span + 1)).reshape(heads, tile * (span + 1))
    skew = skew[:, :tile * span].reshape(heads, tile, span)
    i = np.arange(tile)[:, None]
    j = np.arange(band_rows)[None, :]
    visible = j < ATT_BAND + tile
    if chunked:
        qc = i // CHUNK
        kc = (j - ATT_BAND) // CHUNK
        visible = visible & (kc <= qc) & (kc >= qc - ATT_BAND // CHUNK)
    tab = skew[:, :, tile - 1:tile - 1 + band_rows] + jnp.asarray(np.where(visible, 0.0, NEG), F32)
    return tab * LOG2E


def _mixer(x, params, state, *, tile, chunk, pos0, chunked, ffn=None):
    nb, frames, d = x.shape
    gmix, w_in, conv_w, conv_b, gate_bias, rel_bias, norm_m, w_out = params
    n_tiles = frames // tile
    band_rows = ATT_BAND + -(-tile // LANES) * LANES
    keep_tiles = min(ATT_BAND, frames) // tile
    width2 = 2 * M_WIDTH
    fused_ffn = ffn is not None
    has_state = state is not None

    win = jnp.concatenate([w_in, jnp.zeros((d, D_IN_PAD - w_in.shape[1]), F32)], axis=1).astype(BF16)
    wgt = jnp.zeros((GATE_ROWS, d), F32).at[:N_GATES].set(w_in[:, OFF_MG:].T).astype(BF16)
    gb_row = jnp.zeros((1, LANES), F32).at[0, :N_GATES].set(gate_bias)
    gb_col = jnp.zeros((GATE_ROWS, 1), F32).at[:N_GATES, 0].set(gate_bias)
    tab = _bias_table(rel_bias, tile, band_rows, chunked)

    n_steps = nb * n_tiles + 1
    split = lambda q: (lax.div(q, n_tiles), lax.rem(q, n_tiles))
    proj = lambda p: split(jnp.minimum(p, n_steps - 2))
    mixd = lambda p: split(jnp.maximum(p - 1, 0))
    per_stream = lambda *dims: pl.BlockSpec((1,) + dims, lambda p: (mixd(p)[0],) + (0,) * len(dims))
    proj_spec = pl.BlockSpec((1, tile, d), lambda p: proj(p) + (0,))
    mix_spec = pl.BlockSpec((1, tile, d), lambda p: mixd(p) + (0,))
    kv_spec = pl.BlockSpec(
        (1, tile, A_WIDTH),
        lambda p: (proj(p)[0], jnp.maximum(proj(p)[1] - (n_tiles - keep_tiles), 0), 0))

    args, in_specs = [x], [proj_spec]
    scratch_ffn = []
    if fused_ffn:
        g1, w1, w3, w2 = ffn
        args += [g1.reshape(1, d), w1, w3, w2]
        in_specs += [_resident((1, d)), _resident(w1.shape), _resident(w3.shape), _resident(w2.shape)]
        scratch_ffn = [pltpu.VMEM((tile, w1.shape[1]), BF16), pltpu.VMEM((tile, d), F32), pltpu.VMEM((tile, d), F32)]
    else:
        args.append(x)
        in_specs.append(mix_spec)
    args += [gmix.reshape(1, d), win, wgt, conv_w, conv_b.reshape(1, width2), gb_row, gb_col, tab,
             norm_m.reshape(1, M_WIDTH), w_out.astype(BF16)]
    in_specs += [_resident((1, d)), _resident(win.shape), _resident(wgt.shape),
                 _resident((CONV_W, width2)), _resident((1, width2)), _resident((1, LANES)),
                 _resident((GATE_ROWS, 1)), _resident(tab.shape), _resident((1, M_WIDTH)), _resident((d, d))]
    if has_state:
        k0, v0, c0, n0, m0, conv0 = state
        conv0p = jnp.concatenate([jnp.zeros((nb, SUBLANES - (CONV_W - 1), width2), F32), conv0], axis=1)
        m0p = jnp.broadcast_to(m0[:, :, None], (nb, M_HEADS, LANES))
        args += [k0, v0, c0, n0, m0p, conv0p]
        in_specs += [per_stream(ATT_BAND, A_WIDTH), per_stream(ATT_BAND, A_WIDTH),
                     per_stream(M_HEADS, M_DH, M_DH), per_stream(M_HEADS, M_DH), per_stream(M_HEADS, LANES),
                     per_stream(SUBLANES, width2)]
    out_specs = [
        mix_spec, kv_spec, kv_spec, per_stream(M_HEADS, M_DH, M_DH), per_stream(M_HEADS, M_DH),
        per_stream(M_HEADS, LANES), per_stream(SUBLANES, width2),
    ]
    out_shape = [
        jax.ShapeDtypeStruct((nb, frames, d), F32),
        jax.ShapeDtypeStruct((nb, keep_tiles * tile, A_WIDTH), F32),
        jax.ShapeDtypeStruct((nb, keep_tiles * tile, A_WIDTH), F32),
        jax.ShapeDtypeStruct((nb, M_HEADS, M_DH, M_DH), F32),
        jax.ShapeDtypeStruct((nb, M_HEADS, M_DH), F32),
        jax.ShapeDtypeStruct((nb, M_HEADS, LANES), F32),
        jax.ShapeDtypeStruct((nb, SUBLANES, width2), F32),
    ]
    scratch = [
        pltpu.VMEM((tile, D_IN_PAD), F32), pltpu.VMEM((tile, D_IN_PAD), F32),
        pltpu.VMEM((GATE_ROWS, tile), F32), pltpu.VMEM((GATE_ROWS, tile), F32),
        pltpu.VMEM((band_rows, A_WIDTH), BF16), pltpu.VMEM((band_rows, A_WIDTH), BF16),
        pltpu.VMEM((tile + SUBLANES, width2), F32),
        pltpu.VMEM((M_HEADS, M_DH, M_DH), F32), pltpu.VMEM((M_HEADS, M_DH), F32), pltpu.VMEM((M_HEADS, LANES), F32),
        pltpu.VMEM((tile, d), F32), pltpu.VMEM((tile, d), BF16),
    ] + scratch_ffn
    y, ko, vo, c1, n1, m1, conv1 = pl.pallas_call(
        functools.partial(_mixer_kernel, tile=tile, chunk=chunk, band_rows=band_rows, pos0=pos0,
                          n_tiles=n_tiles, fused_ffn=fused_ffn, has_state=has_state),
        grid=(n_steps,),
        in_specs=in_specs,
        out_specs=out_specs,
        out_shape=out_shape,
        scratch_shapes=scratch,
        compiler_params=pltpu.CompilerParams(dimension_semantics=("arbitrary",),
                                             vmem_limit_bytes=VMEM_LIMIT_BYTES),
        name="mixer_chunked" if chunked else "mixer_step",
    )(*args)
    keep = min(ATT_BAND, frames)
    new_state = (ko.reshape(nb, keep, A_HEADS, A_DH), vo.reshape(nb, keep, A_HEADS, A_DH), c1, n1,
                 m1[:, :, 0], conv1[:, SUBLANES - (CONV_W - 1):, :])
    return y, new_state


def kernel(x_prompt, x_sample, cache_attn_k, cache_attn_v, state_mlstm_C, state_mlstm_n, state_mlstm_m, state_mlstm_conv, norm_ffn1, w1_ffn1, w3_ffn1, w2_ffn1, norm_mix, w_in, conv_w, conv_b, gate_bias, rel_bias, norm_mlstm_out, w_out, norm_ffn2, w1_ffn2, w3_ffn2, w2_ffn2, norm_final):
    depth = norm_ffn1.shape[0]
    nbp, seq, d = x_prompt.shape
    nbs, dec, _ = x_sample.shape
    xp = x_prompt
    xs = x_sample.reshape(nbs * dec, d)
    prompt_tile = min(PROMPT_TILE, seq)
    new_p, new_s = [], []
    for l in range(depth):
        last = l == depth - 1
        ffn1 = _ffn_weights(w1_ffn1[l], w3_ffn1[l], w2_ffn1[l])
        ffn2 = _ffn_weights(w1_ffn2[l], w3_ffn2[l], w2_ffn2[l])
        mix = (norm_mix[l], w_in[l], conv_w[l], conv_b[l], gate_bias[l], rel_bias[l], norm_mlstm_out[l], w_out[l])
        gf = norm_final if last else None

        xp, st = _mixer(xp, mix, None, tile=prompt_tile, chunk=prompt_tile, pos0=0, chunked=True,
                        ffn=(norm_ffn1[l],) + ffn1)
        new_p.append(st)
        xp = _ffn(xp.reshape(nbp * seq, d), norm_ffn2[l], ffn2, gf).reshape(nbp, seq, d)

        xs = _ffn(xs, norm_ffn1[l], ffn1)
        cache = (cache_attn_k[l].reshape(nbs, -1, A_WIDTH), cache_attn_v[l].reshape(nbs, -1, A_WIDTH),
                 state_mlstm_C[l], state_mlstm_n[l], state_mlstm_m[l], state_mlstm_conv[l])
        xs3, st = _mixer(xs.reshape(nbs, dec, d), mix, cache, tile=dec, chunk=dec, pos0=PAST_LEN, chunked=False)
        new_s.append(st)
        xs = _ffn(xs3.reshape(nbs * dec, d), norm_ffn2[l], ffn2, gf)

    stack = lambda states, i: jnp.stack([s[i] for s in states])
    return ((xp, xs.reshape(nbs, dec, d))
            + tuple(stack(new_p, i) for i in range(6)) + tuple(stack(new_s, i) for i in range(6)))
```

```python
import functools

import numpy as np
import jax
import jax.numpy as jnp
from jax import lax
from jax.experimental import pallas as pl
from jax.experimental.pallas import tpu as pltpu

F32 = jnp.float32
BF16 = jnp.bfloat16

CHUNK = 64
ATT_BAND = 8 * CHUNK
A_HEADS = 8
A_DH = 64
A_WIDTH = A_HEADS * A_DH
M_HEADS = 4
M_DH = 128
M_WIDTH = M_HEADS * M_DH
REL_CLIP = 128
CONV_W = 4
PAST_LEN = 4096
EPS = 1e-6
NEG = -1e30
LOG2E = 1.4426950408889634

LANES = 128
SUBLANES = 8
MXU_DIM = 256
VMEM_LIMIT_BYTES = 60 * 1024 * 1024

OFF_AK = A_WIDTH
OFF_AV = 2 * A_WIDTH
OFF_MQK = 3 * A_WIDTH
OFF_MV = OFF_MQK + 2 * M_WIDTH
OFF_MO = OFF_MV + M_WIDTH
OFF_MG = OFF_MO + M_WIDTH
N_GATES = 2 * M_HEADS
D_IN_PAD = OFF_MG + LANES
GATE_ROWS = 16
ZC_MV = A_WIDTH
ZC_MO = ZC_MV + M_WIDTH
ZC_MG = ZC_MO + M_WIDTH
ZC_WIDTH = ZC_MG + LANES

FFN_ROWS = 512
FFN_CHUNK = MXU_DIM
PROMPT_TILE = 256
STEP_ROWS = 128
ATTENTION_AHEAD = 4


def _rms(x, g):
    return x * lax.rsqrt(jnp.mean(x * x, axis=-1, keepdims=True) + EPS) * g


def _sigmoid(x):
    return 1.0 / (1.0 + jnp.exp(-x))


def _log_sigmoid(x):
    return jnp.minimum(x, 0.0) - jnp.log1p(jnp.exp(-jnp.abs(x)))


def _dot(a, b):
    return jnp.dot(a, b, preferred_element_type=F32)


def _dot_nt(a, b):
    return lax.dot_general(a, b, (((1,), (1,)), ((), ())), preferred_element_type=F32)


def _split3(x):
    p1 = x.astype(BF16)
    r1 = x - p1.astype(F32)
    p2 = r1.astype(BF16)
    p3 = (r1 - p2.astype(F32)).astype(BF16)
    return p1, p2, p3


def _resident(shape):
    nd = len(shape)
    return pl.BlockSpec(shape, lambda *_: (0,) * nd, pipeline_mode=pl.Buffered(1))


def _ffn_kernel(x_ref, g_ref, w1_ref, w3_ref, w2_ref, *rest, final_norm):
    if final_norm:
        gf_ref, o_ref, h_ref, u_ref = rest
    else:
        o_ref, h_ref, u_ref = rest
    h_ref[...] = _rms(x_ref[...], g_ref[...]).astype(BF16)
    for c in range(0, u_ref.shape[1], FFN_CHUNK):
        h = h_ref[...]
        a = _dot(h, w1_ref[:, c:c + FFN_CHUNK])
        b = _dot(h, w3_ref[:, c:c + FFN_CHUNK])
        u_ref[:, c:c + FFN_CHUNK] = (a * _sigmoid(a) * b).astype(BF16)
    y = x_ref[...] + 0.5 * _dot(u_ref[...], w2_ref[...])
    if final_norm:
        y = _rms(y, gf_ref[...])
    o_ref[...] = y


def _ffn_weights(w1, w3, w2):
    return w1.astype(BF16), w3.astype(BF16), w2.astype(BF16)


def _ffn(x2d, g, weights, gf=None):
    n, d = x2d.shape
    w1, w3, w2 = weights
    f = w1.shape[1]
    assert f % FFN_CHUNK == 0
    rows = min(FFN_ROWS, n)
    final_norm = gf is not None
    row_spec = pl.BlockSpec((rows, d), lambda i: (i, 0))
    in_specs = [row_spec, _resident((1, d)), _resident(w1.shape), _resident(w3.shape), _resident(w2.shape)]
    args = [x2d, g.reshape(1, d), w1, w3, w2]
    if final_norm:
        in_specs.append(_resident((1, d)))
        args.append(gf.reshape(1, d))
    return pl.pallas_call(
        functools.partial(_ffn_kernel, final_norm=final_norm),
        grid=(n // rows,),
        in_specs=in_specs,
        out_specs=row_spec,
        out_shape=jax.ShapeDtypeStruct((n, d), F32),
        scratch_shapes=[pltpu.VMEM((rows, d), BF16), pltpu.VMEM((rows, f), BF16)],
        compiler_params=pltpu.CompilerParams(dimension_semantics=("arbitrary",),
                                             vmem_limit_bytes=VMEM_LIMIT_BYTES),
        name="ffn_final" if final_norm else "ffn",
    )(*args)


def _mixer_kernel(*refs, tile, group, chunk, band_rows, pos0, n_tiles, fused_ffn, has_state, ahead):
    refs = iter(refs)
    take = lambda n: [next(refs) for _ in range(n)]
    (xa_ref,) = take(1)
    if fused_ffn:
        g1_ref, w1_ref, w3_ref, w2_ref = take(4)
    else:
        (xb_ref,) = take(1)
    gmix_ref, win_ref, wgt_ref, convw_ref, convb_ref, gbrow_ref, gbcol_ref, tab_ref, normm_ref, wout_ref = take(10)
    if has_state:
        k0_ref, v0_ref, c0_ref, n0_ref, m0_ref, conv0_ref = take(6)
    y_ref, ko_ref, vo_ref, co_ref, no_ref, mo_ref, convo_ref = take(7)
    z_new, z_cur, g_new, g_cur, kband, vband, ubuf, c_s, n_s, m_s, mix_s, h_s = take(12)
    if fused_ffn:
        u_s, h1_s, x1_new, x1_mid, x1_cur = take(5)
    assert group == 1 or n_tiles == 1
    streams = range(group)
    span = lambda g: slice(g * tile, (g + 1) * tile)

    p = pl.program_id(0)
    lag = 2 if fused_ffn else 1
    t = lax.rem(jnp.maximum(p - lag, 0), n_tiles)

    @pl.when(p == 0)
    def _first_step():
        z_cur[...] = jnp.zeros_like(z_cur)
        g_cur[...] = jnp.zeros_like(g_cur)
        for g in streams:
            kband[g, ATT_BAND:band_rows, :] = jnp.zeros((band_rows - ATT_BAND, A_WIDTH), BF16)
            vband[g, ATT_BAND:band_rows, :] = jnp.zeros((band_rows - ATT_BAND, A_WIDTH), BF16)
            ubuf[g, SUBLANES:SUBLANES + tile, :] = jnp.zeros((tile, 2 * M_WIDTH), F32)
        if fused_ffn:
            x1_cur[...] = jnp.zeros_like(x1_cur)
            x1_mid[...] = jnp.zeros_like(x1_mid)

    @pl.when(t == 0)
    def _load_state():
        if has_state:
            for g in streams:
                kband[g, 0:ATT_BAND, :] = k0_ref[g]
                vband[g, 0:ATT_BAND, :] = v0_ref[g]
                ubuf[g, 0:SUBLANES, :] = conv0_ref[g]
            c_s[...] = c0_ref[...]
            n_s[...] = n0_ref[...]
            m_s[...] = m0_ref[...]
        else:
            for g in streams:
                kband[g, 0:ATT_BAND, :] = jnp.zeros((ATT_BAND, A_WIDTH), BF16)
                vband[g, 0:ATT_BAND, :] = jnp.zeros((ATT_BAND, A_WIDTH), BF16)
                ubuf[g, 0:SUBLANES, :] = jnp.zeros((SUBLANES, 2 * M_WIDTH), F32)
            c_s[...] = jnp.zeros_like(c_s)
            n_s[...] = jnp.zeros_like(n_s)
            m_s[...] = jnp.zeros_like(m_s)

    def stage_a1():
        h1_s[...] = _rms(xa_ref[...], g1_ref[...]).astype(BF16)
        yield
        for c in range(0, u_s.shape[1], FFN_CHUNK):
            hf = h1_s[...]
            a = _dot(hf, w1_ref[:, c:c + FFN_CHUNK])
            b = _dot(hf, w3_ref[:, c:c + FFN_CHUNK])
            u_s[:, c:c + FFN_CHUNK] = (a * _sigmoid(a) * b).astype(BF16)
            yield
        for c in range(0, x1_new.shape[1], MXU_DIM):
            x1_new[:, c:c + MXU_DIM] = (xa_ref[:, c:c + MXU_DIM]
                                        + 0.5 * _dot(u_s[...], w2_ref[:, c:c + MXU_DIM]))
            yield

    def stage_a2():
        x_in = x1_mid[...] if fused_ffn else xa_ref[...]
        h_s[...] = _rms(x_in, gmix_ref[...]).astype(BF16)
        g_new[...] = _dot_nt(wgt_ref[...], h_s[...])
        yield
        for c in range(0, D_IN_PAD, MXU_DIM):
            c1 = min(c + MXU_DIM, D_IN_PAD)
            z_new[:, c:c1] = _dot(h_s[...], win_ref[:, c:c1])
            yield

    def alternate(*gens):
        gens = list(gens)
        while gens:
            for gen in list(gens):
                try:
                    next(gen)
                    yield
                except StopIteration:
                    gens.remove(gen)

    units = alternate(stage_a1(), stage_a2()) if fused_ffn else stage_a2()

    def project_next(count=1):
        for _ in range(count):
            next(units, None)

    lane = lax.broadcasted_iota(jnp.int32, (1, LANES), 1)
    even = lane < A_DH
    if pos0 < ATT_BAND:
        col = lax.broadcasted_iota(jnp.int32, (1, band_rows), 1)
        in_stream = col >= (ATT_BAND - pos0) - t * tile

    def scores(g, head):
        lo = (head // 2) * LANES
        mine = even if head % 2 == 0 else jnp.logical_not(even)
        qh = jnp.where(mine, z_cur[span(g), lo:lo + LANES] * (LOG2E * A_DH ** -0.5), 0.0).astype(BF16)
        s = _dot_nt(qh, kband[g, :, lo:lo + LANES]) + tab_ref[head]
        if pos0 < ATT_BAND:
            s = jnp.where(in_stream, s, NEG)
        return s

    def attend(g, head, s):
        lo = (head // 2) * LANES
        mine = even if head % 2 == 0 else jnp.logical_not(even)
        vp = vband[g, :, lo:lo + LANES]
        e = jnp.exp2(s - jnp.max(s, axis=-1, keepdims=True))
        o = _dot(e.astype(BF16), jnp.where(mine, vp, jnp.zeros_like(vp)))
        o = o * (1.0 / jnp.sum(e, axis=-1, keepdims=True))
        if head % 2 == 0:
            mix_s[span(g), lo:lo + LANES] = o
        else:
            mix_s[span(g), lo:lo + LANES] += o

    pending = []
    for g in streams:
        for head in range(A_HEADS):
            project_next(lag)
            pending.append((g, head, scores(g, head)))
            if len(pending) > ahead:
                attend(*pending.pop(0))
    for item in pending:
        attend(*item)

    project_next(2)
    mq, mk, last_rows = [], [], []
    for g in streams:
        qk = convb_ref[...]
        for j in range(CONV_W):
            start = SUBLANES - (CONV_W - 1) + j
            qk = qk + ubuf[g, start:start + tile, :] * convw_ref[j:j + 1, :]
        qk = qk * _sigmoid(qk)
        mq.append(qk[:, :M_WIDTH])
        mk.append(qk[:, M_WIDTH:] * (M_DH ** -0.5))
        last_rows.append(ubuf[g, tile:tile + SUBLANES, :])
        convo_ref[g] = last_rows[g]

    project_next(2)
    rows = group * tile
    g_col = z_cur[:, ZC_MG:ZC_MG + LANES] + gbrow_ref[...]
    g_row = g_cur[...] + gbcol_ref[...]
    lf_col = _log_sigmoid(g_col)
    lf_row = _log_sigmoid(g_row)
    ri = lax.broadcasted_iota(jnp.int32, (rows, rows), 0)
    ci = lax.broadcasted_iota(jnp.int32, (rows, rows), 1)
    assert chunk & (chunk - 1) == 0
    chunk_shift = chunk.bit_length() - 1
    same_chunk = (ri >> chunk_shift) == (ci >> chunk_shift)
    tri = jnp.where(jnp.logical_and(same_chunk, ci <= ri), 1.0, 0.0).astype(BF16)
    tri_t = jnp.where(jnp.logical_and(same_chunk, ri <= ci), 1.0, 0.0).astype(BF16)
    b_col = sum(_dot(tri, part) for part in _split3(lf_col))
    b_row = sum(_dot(part, tri_t) for part in _split3(lf_row))

    causal = (lax.broadcasted_iota(jnp.int32, (chunk, chunk), 1)
              <= lax.broadcasted_iota(jnp.int32, (chunk, chunk), 0))
    assert chunk == tile
    first = {}

    def block_scores(g, hd):
        lo, hi = hd * M_DH, (hd + 1) * M_DH
        cmat = c_s[g, hd]
        m_prev = m_s[g, hd:hd + 1, 0:1]
        bc = b_col[span(g), M_HEADS + hd:M_HEADS + hd + 1]
        br = b_row[M_HEADS + hd:M_HEADS + hd + 1, span(g)]
        igr = g_row[hd:hd + 1, span(g)]
        dmat = jnp.where(causal, bc + (igr - br), NEG)
        inter = bc + m_prev
        mt = jnp.maximum(inter, jnp.max(dmat, axis=-1, keepdims=True))
        q = mq[g][:, lo:hi]
        k = mk[g][:, lo:hi]
        vb = z_cur[span(g), ZC_MV + lo:ZC_MV + hi].astype(BF16)
        qb = q.astype(BF16)
        first[g, hd] = dict(bc=bc, mt=mt, inter=inter, m_prev=m_prev, cmat=cmat, q=q, k=k, vb=vb, dmat=dmat,
                            qk=_dot_nt(qb, k.astype(BF16)), qc=_dot(qb, cmat.astype(BF16)))

    def block_output(g, hd):
        lo, hi = hd * M_DH, (hd + 1) * M_DH
        f = first[g, hd]
        nrow = n_s[g, hd:hd + 1, :]
        s = f["qk"] * jnp.exp(f["dmat"] - f["mt"])
        iw = jnp.exp(f["inter"] - f["mt"])
        num = iw * f["qc"] + _dot(s.astype(BF16), f["vb"])
        den = iw * jnp.sum(f["q"] * nrow, axis=-1, keepdims=True) + jnp.sum(s, axis=-1, keepdims=True)
        mh = num / jnp.maximum(jnp.abs(den), jnp.exp(-f["mt"]))
        mh = mh * _sigmoid(z_cur[span(g), ZC_MO + lo:ZC_MO + hi])
        mix_s[span(g), A_WIDTH + lo:A_WIDTH + hi] = _rms(mh, normm_ref[:, lo:hi])

    def block_state(g, hd):
        f = first[g, hd]
        bc, mt, m_prev = f["bc"], f["mt"], f["m_prev"]
        igc = g_col[span(g), hd:hd + 1]
        b_last = bc[tile - 1:tile, :]
        m_new = mt[tile - 1:tile, :]
        kw = f["k"] * jnp.exp(b_last - bc + igc - m_new)
        decay = jnp.exp(b_last + m_prev - m_new)
        c_s[g, hd] = decay * f["cmat"] + _dot(kw.T.astype(BF16), f["vb"])
        n_s[g, hd:hd + 1, :] = decay * n_s[g, hd:hd + 1, :] + jnp.sum(kw, axis=0, keepdims=True)
        m_s[g, hd:hd + 1, :] = jnp.broadcast_to(m_new, (1, LANES))

    blocks = [(g, hd) for hd in range(M_HEADS) for g in streams]
    if group == 1:
        for blk in blocks:
            project_next(3)
            block_scores(*blk)
            block_output(*blk)
            block_state(*blk)
    else:
        for phase in (block_scores, block_output, block_state):
            for blk in blocks:
                project_next()
                phase(*blk)

    for _ in units:
        pass
    co_ref[...] = c_s[...]
    no_ref[...] = n_s[...]
    mo_ref[...] = m_s[...]
    resid = x1_cur[...] if fused_ffn else xb_ref[...]
    y_ref[...] = resid + _dot(mix_s[...].astype(BF16), wout_ref[...])

    if fused_ffn:
        x1_cur[...] = x1_mid[...]
        x1_mid[...] = x1_new[...]
    ko_ref[...] = z_new[:, OFF_AK:OFF_AV]
    vo_ref[...] = z_new[:, OFF_AV:OFF_MQK]
    for g in streams:
        if n_tiles > 1:
            kband[g, 0:ATT_BAND, :] = kband[g, tile:tile + ATT_BAND, :]
            vband[g, 0:ATT_BAND, :] = vband[g, tile:tile + ATT_BAND, :]
        kband[g, ATT_BAND:ATT_BAND + tile, :] = z_new[span(g), OFF_AK:OFF_AV].astype(BF16)
        vband[g, ATT_BAND:ATT_BAND + tile, :] = z_new[span(g), OFF_AV:OFF_MQK].astype(BF16)
        ubuf[g, 0:SUBLANES, :] = last_rows[g]
        ubuf[g, SUBLANES:SUBLANES + tile, :] = z_new[span(g), OFF_MQK:OFF_MV]
    z_cur[:, 0:OFF_AK] = z_new[:, 0:OFF_AK]
    z_cur[:, ZC_MV:ZC_WIDTH] = z_new[:, OFF_MV:D_IN_PAD]
    g_cur[...] = g_new[...]


def _bias_table(rel_bias, tile, band_rows, chunked):
    heads = rel_bias.shape[0]
    span = band_rows + tile - 1
    n_far = ATT_BAND + tile - REL_CLIP
    n_near = span - n_far - (2 * REL_CLIP - 1)
    diag = jnp.concatenate([jnp.broadcast_to(rel_bias[:, 2 * REL_CLIP:], (heads, n_far)),
                            rel_bias[:, 2 * REL_CLIP - 1:0:-1],
                            jnp.broadcast_to(rel_bias[:, :1], (heads, n_near + 1))], axis=1)
    skew = jnp.broadcast_to(diag[:, None, :], (heads, tile, span + 1)).reshape(heads, tile * (span + 1))
    skew = skew[:, :tile * span].reshape(heads, tile, span)
    i = np.arange(tile)[:, None]
    j = np.arange(band_rows)[None, :]
    visible = j < ATT_BAND + tile
    if chunked:
        qc = i // CHUNK
        kc = (j - ATT_BAND) // CHUNK
        visible = visible & (kc <= qc) & (kc >= qc - ATT_BAND // CHUNK)
    tab = skew[:, :, tile - 1:tile - 1 + band_rows] + jnp.asarray(np.where(visible, 0.0, NEG), F32)
    return tab * LOG2E


def _mixer(x, params, state, *, tile, chunk, pos0, chunked, ffn=None):
    nb, frames, d = x.shape
    gmix, w_in, conv_w, conv_b, gate_bias, rel_bias, norm_m, w_out = params
    n_tiles = frames // tile
    band_rows = ATT_BAND + -(-tile // LANES) * LANES
    keep_tiles = min(ATT_BAND, frames) // tile
    width2 = 2 * M_WIDTH
    fused_ffn = ffn is not None
    has_state = state is not None
    group = max(1, min(nb, STEP_ROWS // tile)) if n_tiles == 1 else 1
    assert nb % group == 0
    rows = group * tile

    win = jnp.concatenate([w_in, jnp.zeros((d, D_IN_PAD - w_in.shape[1]), F32)], axis=1).astype(BF16)
    wgt = jnp.zeros((GATE_ROWS, d), F32).at[:N_GATES].set(w_in[:, OFF_MG:].T).astype(BF16)
    gb_row = jnp.zeros((1, LANES), F32).at[0, :N_GATES].set(gate_bias)
    gb_col = jnp.zeros((GATE_ROWS, 1), F32).at[:N_GATES, 0].set(gate_bias)
    tab = _bias_table(rel_bias, tile, band_rows, chunked)

    lag = 2 if fused_ffn else 1
    n_tiles_all = (nb // group) * n_tiles
    n_steps = n_tiles_all + lag
    entering = lambda p: jnp.minimum(p, n_tiles_all - 1)
    projected = lambda p: jnp.clip(p - (lag - 1), 0, n_tiles_all - 1)
    mixed = lambda p: jnp.maximum(p - lag, 0)
    per_group = lambda *dims: pl.BlockSpec((group,) + dims,
                                           lambda p: (lax.div(mixed(p), n_tiles),) + (0,) * len(dims))
    enter_spec = pl.BlockSpec((rows, d), lambda p: (entering(p), 0))
    mix_spec = pl.BlockSpec((rows, d), lambda p: (mixed(p), 0))

    def kv_index(p):
        q = projected(p)
        return (lax.div(q, n_tiles) * keep_tiles + jnp.maximum(lax.rem(q, n_tiles) - (n_tiles - keep_tiles), 0), 0)

    kv_spec = pl.BlockSpec((rows, A_WIDTH), kv_index)

    x2d = x.reshape(nb * frames, d)
    args, in_specs = [x2d], [enter_spec]
    scratch_ffn = []
    if fused_ffn:
        g1, w1, w3, w2 = ffn
        args += [g1.reshape(1, d), w1, w3, w2]
        in_specs += [_resident((1, d)), _resident(w1.shape), _resident(w3.shape), _resident(w2.shape)]
        scratch_ffn = [pltpu.VMEM((rows, w1.shape[1]), BF16), pltpu.VMEM((rows, d), BF16)] + [pltpu.VMEM((rows, d), F32)] * 3
    else:
        args.append(x2d)
        in_specs.append(mix_spec)
    args += [gmix.reshape(1, d), win, wgt, conv_w, conv_b.reshape(1, width2), gb_row, gb_col, tab,
             norm_m.reshape(1, M_WIDTH), w_out.astype(BF16)]
    in_specs += [_resident((1, d)), _resident(win.shape), _resident(wgt.shape),
                 _resident((CONV_W, width2)), _resident((1, width2)), _resident((1, LANES)),
                 _resident((GATE_ROWS, 1)), _resident(tab.shape), _resident((1, M_WIDTH)), _resident((d, d))]
    if has_state:
        k0, v0, c0, n0, m0, conv0 = state
        conv0p = jnp.concatenate([jnp.zeros((nb, SUBLANES - (CONV_W - 1), width2), F32), conv0], axis=1)
        m0p = jnp.broadcast_to(m0[:, :, None], (nb, M_HEADS, LANES))
        args += [k0.astype(BF16), v0.astype(BF16), c0, n0, m0p, conv0p]
        in_specs += [per_group(ATT_BAND, A_WIDTH), per_group(ATT_BAND, A_WIDTH),
                     per_group(M_HEADS, M_DH, M_DH), per_group(M_HEADS, M_DH), per_group(M_HEADS, LANES),
                     per_group(SUBLANES, width2)]
    out_specs = [
        mix_spec, kv_spec, kv_spec, per_group(M_HEADS, M_DH, M_DH), per_group(M_HEADS, M_DH),
        per_group(M_HEADS, LANES), per_group(SUBLANES, width2),
    ]
    keep = keep_tiles * tile
    out_shape = [
        jax.ShapeDtypeStruct((nb * frames, d), F32),
        jax.ShapeDtypeStruct((nb * keep, A_WIDTH), F32),
        jax.ShapeDtypeStruct((nb * keep, A_WIDTH), F32),
        jax.ShapeDtypeStruct((nb, M_HEADS, M_DH, M_DH), F32),
        jax.ShapeDtypeStruct((nb, M_HEADS, M_DH), F32),
        jax.ShapeDtypeStruct((nb, M_HEADS, LANES), F32),
        jax.ShapeDtypeStruct((nb, SUBLANES, width2), F32),
    ]
    scratch = [
        pltpu.VMEM((rows, D_IN_PAD), F32), pltpu.VMEM((rows, ZC_WIDTH), F32),
        pltpu.VMEM((GATE_ROWS, rows), F32), pltpu.VMEM((GATE_ROWS, rows), F32),
        pltpu.VMEM((group, band_rows, A_WIDTH), BF16), pltpu.VMEM((group, band_rows, A_WIDTH), BF16),
        pltpu.VMEM((group, tile + SUBLANES, width2), F32),
        pltpu.VMEM((group, M_HEADS, M_DH, M_DH), F32), pltpu.VMEM((group, M_HEADS, M_DH), F32),
        pltpu.VMEM((group, M_HEADS, LANES), F32),
        pltpu.VMEM((rows, d), F32), pltpu.VMEM((rows, d), BF16),
    ] + scratch_ffn
    y, ko, vo, c1, n1, m1, conv1 = pl.pallas_call(
        functools.partial(_mixer_kernel, tile=tile, group=group, chunk=chunk, band_rows=band_rows, pos0=pos0,
                          n_tiles=n_tiles, fused_ffn=fused_ffn, has_state=has_state,
                          ahead=2 if group == 1 else ATTENTION_AHEAD),
        grid=(n_steps,),
        in_specs=in_specs,
        out_specs=out_specs,
        out_shape=out_shape,
        scratch_shapes=scratch,
        compiler_params=pltpu.CompilerParams(dimension_semantics=("arbitrary",),
                                             vmem_limit_bytes=VMEM_LIMIT_BYTES),
        name="mixer_chunked" if chunked else "mixer_step",
    )(*args)
    new_state = (ko.reshape(nb, keep, A_HEADS, A_DH), vo.reshape(nb, keep, A_HEADS, A_DH), c1, n1,
                 m1[:, :, 0], conv1[:, SUBLANES - (CONV_W - 1):, :])
    return y.reshape(nb, frames, d), new_state


def kernel(x_prompt, x_sample, cache_attn_k, cache_attn_v, state_mlstm_C, state_mlstm_n, state_mlstm_m, state_mlstm_conv, norm_ffn1, w1_ffn1, w3_ffn1, w2_ffn1, norm_mix, w_in, conv_w, conv_b, gate_bias, rel_bias, norm_mlstm_out, w_out, norm_ffn2, w1_ffn2, w3_ffn2, w2_ffn2, norm_final):
    depth = norm_ffn1.shape[0]
    nbp, seq, d = x_prompt.shape
    nbs, dec, _ = x_sample.shape
    xp = x_prompt
    xs = x_sample.reshape(nbs * dec, d)
    prompt_tile = min(PROMPT_TILE, seq)
    new_p, new_s = [], []
    for l in range(depth):
        last = l == depth - 1
        ffn1 = _ffn_weights(w1_ffn1[l], w3_ffn1[l], w2_ffn1[l])
        ffn2 = _ffn_weights(w1_ffn2[l], w3_ffn2[l], w2_ffn2[l])
        mix = (norm_mix[l], w_in[l], conv_w[l], conv_b[l], gate_bias[l], rel_bias[l], norm_mlstm_out[l], w_out[l])
        gf = norm_final if last else None

        xp, st = _mixer(xp, mix, None, tile=prompt_tile, chunk=prompt_tile, pos0=0, chunked=True,
                        ffn=(norm_ffn1[l],) + ffn1)
        new_p.append(st)
        xp = _ffn(xp.reshape(nbp * seq, d), norm_ffn2[l], ffn2, gf).reshape(nbp, seq, d)

        xs = _ffn(xs, norm_ffn1[l], ffn1)
        cache = (cache_attn_k[l].reshape(nbs, -1, A_WIDTH), cache_attn_v[l].reshape(nbs, -1, A_WIDTH),
                 state_mlstm_C[l], state_mlstm_n[l], state_mlstm_m[l], state_mlstm_conv[l])
        xs3, st = _mixer(xs.reshape(nbs, dec, d), mix, cache, tile=dec, chunk=dec, pos0=PAST_LEN, chunked=False)
        new_s.append(st)
        xs = _ffn(xs3.reshape(nbs * dec, d), norm_ffn2[l], ffn2, gf)

    stack = lambda states, i: jnp.stack([s[i] for s in states])
    return ((xp, xs.reshape(nbs, dec, d))
            + tuple(stack(new_p, i) for i in range(6)) + tuple(stack(new_s, i) for i in range(6)))
```

```python
import functools

import numpy as np
import jax
import jax.numpy as jnp
from jax import lax
from jax.experimental import pallas as pl
from jax.experimental.pallas import tpu as pltpu

F32 = jnp.float32
BF16 = jnp.bfloat16

CHUNK = 64
ATT_BAND = 8 * CHUNK
A_HEADS = 8
A_DH = 64
A_WIDTH = A_HEADS * A_DH
M_HEADS = 4
M_DH = 128
M_WIDTH = M_HEADS * M_DH
REL_CLIP = 128
CONV_W = 4
PAST_LEN = 4096
EPS = 1e-6
NEG = -1e30
LOG2E = 1.4426950408889634

LANES = 128
SUBLANES = 8
MXU_DIM = 256
VMEM_LIMIT_BYTES = 60 * 1024 * 1024

OFF_AK = A_WIDTH
OFF_AV = 2 * A_WIDTH
OFF_MQK = 3 * A_WIDTH
OFF_MV = OFF_MQK + 2 * M_WIDTH
OFF_MO = OFF_MV + M_WIDTH
OFF_MG = OFF_MO + M_WIDTH
N_GATES = 2 * M_HEADS
D_IN_PAD = OFF_MG + LANES
GATE_ROWS = 16
ZC_MV = A_WIDTH
ZC_MO = ZC_MV + M_WIDTH
ZC_MG = ZC_MO + M_WIDTH
ZC_WIDTH = ZC_MG + LANES

FFN_ROWS = 512
FFN_CHUNK = MXU_DIM
PROMPT_TILE = 256
STEP_ROWS = 128
ATTENTION_AHEAD = 4


def _rms(x, g):
    return x * lax.rsqrt(jnp.mean(x * x, axis=-1, keepdims=True) + EPS) * g


def _sigmoid(x):
    return 1.0 / (1.0 + jnp.exp(-x))


def _log_sigmoid(x):
    return jnp.minimum(x, 0.0) - jnp.log1p(jnp.exp(-jnp.abs(x)))


def _dot(a, b):
    return jnp.dot(a, b, preferred_element_type=F32)


def _dot_nt(a, b):
    return lax.dot_general(a, b, (((1,), (1,)), ((), ())), preferred_element_type=F32)


def _split3(x):
    p1 = x.astype(BF16)
    r1 = x - p1.astype(F32)
    p2 = r1.astype(BF16)
    p3 = (r1 - p2.astype(F32)).astype(BF16)
    return p1, p2, p3


def _resident(shape):
    nd = len(shape)
    return pl.BlockSpec(shape, lambda *_: (0,) * nd, pipeline_mode=pl.Buffered(1))


def _ffn_kernel(x_ref, g_ref, w1_ref, w3_ref, w2_ref, *rest, final_norm):
    if final_norm:
        gf_ref, o_ref, h_ref, u_ref = rest
    else:
        o_ref, h_ref, u_ref = rest
    h_ref[...] = _rms(x_ref[...], g_ref[...]).astype(BF16)
    for c in range(0, u_ref.shape[1], FFN_CHUNK):
        h = h_ref[...]
        a = _dot(h, w1_ref[:, c:c + FFN_CHUNK])
        b = _dot(h, w3_ref[:, c:c + FFN_CHUNK])
        u_ref[:, c:c + FFN_CHUNK] = (a * _sigmoid(a) * b).astype(BF16)
    y = x_ref[...] + 0.5 * _dot(u_ref[...], w2_ref[...])
    if final_norm:
        y = _rms(y, gf_ref[...])
    o_ref[...] = y


def _ffn_weights(w1, w3, w2):
    return w1.astype(BF16), w3.astype(BF16), w2.astype(BF16)


def _ffn(x2d, g, weights, gf=None):
    n, d = x2d.shape
    w1, w3, w2 = weights
    f = w1.shape[1]
    assert f % FFN_CHUNK == 0
    rows = min(FFN_ROWS, n)
    final_norm = gf is not None
    row_spec = pl.BlockSpec((rows, d), lambda i: (i, 0))
    in_specs = [row_spec, _resident((1, d)), _resident(w1.shape), _resident(w3.shape), _resident(w2.shape)]
    args = [x2d, g.reshape(1, d), w1, w3, w2]
    if final_norm:
        in_specs.append(_resident((1, d)))
        args.append(gf.reshape(1, d))
    return pl.pallas_call(
        functools.partial(_ffn_kernel, final_norm=final_norm),
        grid=(n // rows,),
        in_specs=in_specs,
        out_specs=row_spec,
        out_shape=jax.ShapeDtypeStruct((n, d), F32),
        scratch_shapes=[pltpu.VMEM((rows, d), BF16), pltpu.VMEM((rows, f), BF16)],
        compiler_params=pltpu.CompilerParams(dimension_semantics=("arbitrary",),
                                             vmem_limit_bytes=VMEM_LIMIT_BYTES),
        name="ffn_final" if final_norm else "ffn",
    )(*args)


def _mixer_kernel(*refs, tile, group, chunk, band_rows, pos0, n_tiles, fused_ffn, has_state, ahead):
    refs = iter(refs)
    take = lambda n: [next(refs) for _ in range(n)]
    (xa_ref,) = take(1)
    if fused_ffn:
        g1_ref, w1_ref, w3_ref, w2_ref = take(4)
    else:
        (xb_ref,) = take(1)
    gmix_ref, win_ref, wgt_ref, convw_ref, convb_ref, gbrow_ref, gbcol_ref, diag_ref, mask_ref, normm_ref, wout_ref = take(11)
    if has_state:
        k0_ref, v0_ref, c0_ref, n0_ref, m0_ref, conv0_ref = take(6)
    y_ref, ko_ref, vo_ref, co_ref, no_ref, mo_ref, convo_ref = take(7)
    z_new, z_cur, g_new, g_cur, kband, vband, ubuf, c_s, n_s, m_s, mix_s, h_s, tab_s = take(13)
    if fused_ffn:
        u_s, h1_s, x1_new, x1_mid, x1_cur = take(5)
    assert group == 1 or n_tiles == 1
    streams = range(group)
    span = lambda g: slice(g * tile, (g + 1) * tile)

    p = pl.program_id(0)
    lag = 2 if fused_ffn else 1
    t = lax.rem(jnp.maximum(p - lag, 0), n_tiles)

    @pl.when(p == 0)
    def _first_step():
        z_cur[...] = jnp.zeros_like(z_cur)
        g_cur[...] = jnp.zeros_like(g_cur)
        for g in streams:
            kband[g, ATT_BAND:band_rows, :] = jnp.zeros((band_rows - ATT_BAND, A_WIDTH), BF16)
            vband[g, ATT_BAND:band_rows, :] = jnp.zeros((band_rows - ATT_BAND, A_WIDTH), BF16)
            ubuf[g, SUBLANES:SUBLANES + tile, :] = jnp.zeros((tile, 2 * M_WIDTH), F32)
        if fused_ffn:
            x1_cur[...] = jnp.zeros_like(x1_cur)
            x1_mid[...] = jnp.zeros_like(x1_mid)
        width = diag_ref.shape[1]
        for head in range(A_HEADS):
            rows_of_diag = jnp.broadcast_to(diag_ref[head:head + 1, :], (tile, width))
            skew = pltpu.roll(rows_of_diag, width - (tile - 1), 1, stride=1, stride_axis=0)
            tab_s[head] = skew[:, :band_rows] + mask_ref[...]

    @pl.when(t == 0)
    def _load_state():
        if has_state:
            for g in streams:
                kband[g, 0:ATT_BAND, :] = k0_ref[g].astype(BF16)
                vband[g, 0:ATT_BAND, :] = v0_ref[g].astype(BF16)
                ubuf[g, 0:SUBLANES, :] = conv0_ref[g]
            c_s[...] = c0_ref[...]
            n_s[...] = n0_ref[...]
            m_s[...] = m0_ref[...]
        else:
            for g in streams:
                kband[g, 0:ATT_BAND, :] = jnp.zeros((ATT_BAND, A_WIDTH), BF16)
                vband[g, 0:ATT_BAND, :] = jnp.zeros((ATT_BAND, A_WIDTH), BF16)
                ubuf[g, 0:SUBLANES, :] = jnp.zeros((SUBLANES, 2 * M_WIDTH), F32)
            c_s[...] = jnp.zeros_like(c_s)
            n_s[...] = jnp.zeros_like(n_s)
            m_s[...] = jnp.zeros_like(m_s)

    def stage_a1():
        h1_s[...] = _rms(xa_ref[...], g1_ref[...]).astype(BF16)
        yield
        for c in range(0, u_s.shape[1], FFN_CHUNK):
            hf = h1_s[...]
            a = _dot(hf, w1_ref[:, c:c + FFN_CHUNK])
            b = _dot(hf, w3_ref[:, c:c + FFN_CHUNK])
            u_s[:, c:c + FFN_CHUNK] = (a * _sigmoid(a) * b).astype(BF16)
            yield
        for c in range(0, x1_new.shape[1], MXU_DIM):
            x1_new[:, c:c + MXU_DIM] = (xa_ref[:, c:c + MXU_DIM]
                                        + 0.5 * _dot(u_s[...], w2_ref[:, c:c + MXU_DIM]))
            yield

    def stage_a2():
        x_in = x1_mid[...] if fused_ffn else xa_ref[...]
        h_s[...] = _rms(x_in, gmix_ref[...]).astype(BF16)
        yield
        for c in range(0, D_IN_PAD, MXU_DIM):
            c1 = min(c + MXU_DIM, D_IN_PAD)
            z_new[:, c:c1] = _dot(h_s[...], win_ref[:, c:c1])
            projected_cols[0] = c1
            yield
        g_new[...] = _dot_nt(wgt_ref[...], h_s[...])

    projected_cols = [0]

    def alternate(*gens):
        gens = list(gens)
        while gens:
            for gen in list(gens):
                try:
                    next(gen)
                    yield
                except StopIteration:
                    gens.remove(gen)

    units = alternate(stage_a1(), stage_a2()) if fused_ffn else stage_a2()

    def project_next(count=1):
        for _ in range(count):
            next(units, None)

    lane = lax.broadcasted_iota(jnp.int32, (1, LANES), 1)
    even = lane < A_DH
    if pos0 < ATT_BAND:
        col = lax.broadcasted_iota(jnp.int32, (1, band_rows), 1)
        in_stream = col >= (ATT_BAND - pos0) - t * tile

    def scores(g, head):
        lo = (head // 2) * LANES
        mine = even if head % 2 == 0 else jnp.logical_not(even)
        qh = jnp.where(mine, z_cur[span(g), lo:lo + LANES] * (LOG2E * A_DH ** -0.5), 0.0).astype(BF16)
        s = _dot_nt(qh, kband[g, :, lo:lo + LANES]) + tab_s[head]
        if pos0 < ATT_BAND:
            s = jnp.where(in_stream, s, NEG)
        return s

    def attend(g, head, s):
        lo = (head // 2) * LANES
        mine = even if head % 2 == 0 else jnp.logical_not(even)
        vp = vband[g, :, lo:lo + LANES]
        e = jnp.exp2(s - jnp.max(s, axis=-1, keepdims=True))
        o = _dot(e.astype(BF16), jnp.where(mine, vp, jnp.zeros_like(vp)))
        o = o * (1.0 / jnp.sum(e, axis=-1, keepdims=True))
        if head % 2 == 0:
            mix_s[span(g), lo:lo + LANES] = o
        else:
            mix_s[span(g), lo:lo + LANES] += o

    pending = []
    for g in streams:
        for head in range(A_HEADS):
            pending.append((g, head, scores(g, head)))
            project_next(lag)
            if len(pending) > ahead:
                attend(*pending.pop(0))
    for item in pending:
        attend(*item)
    assert projected_cols[0] >= OFF_MQK
    ko_ref[...] = z_new[:, OFF_AK:OFF_AV]
    vo_ref[...] = z_new[:, OFF_AV:OFF_MQK]
    for g in streams:
        if n_tiles > 1:
            kband[g, 0:ATT_BAND, :] = kband[g, tile:tile + ATT_BAND, :]
            vband[g, 0:ATT_BAND, :] = vband[g, tile:tile + ATT_BAND, :]
        kband[g, ATT_BAND:ATT_BAND + tile, :] = z_new[span(g), OFF_AK:OFF_AV].astype(BF16)
        vband[g, ATT_BAND:ATT_BAND + tile, :] = z_new[span(g), OFF_AV:OFF_MQK].astype(BF16)
    z_cur[:, 0:OFF_AK] = z_new[:, 0:OFF_AK]

    project_next(2)
    mq, mk, last_rows = [], [], []
    for g in streams:
        qk = convb_ref[...]
        for j in range(CONV_W):
            start = SUBLANES - (CONV_W - 1) + j
            qk = qk + ubuf[g, start:start + tile, :] * convw_ref[j:j + 1, :]
        qk = qk * _sigmoid(qk)
        mq.append(qk[:, :M_WIDTH])
        mk.append(qk[:, M_WIDTH:] * (M_DH ** -0.5))
        last_rows.append(ubuf[g, tile:tile + SUBLANES, :])
        convo_ref[g] = last_rows[g]

    project_next(2)
    rows = group * tile
    g_col = z_cur[:, ZC_MG:ZC_MG + LANES] + gbrow_ref[...]
    g_row = g_cur[...] + gbcol_ref[...]
    lf_col = _log_sigmoid(g_col)
    lf_row = _log_sigmoid(g_row)
    ri = lax.broadcasted_iota(jnp.int32, (rows, rows), 0)
    ci = lax.broadcasted_iota(jnp.int32, (rows, rows), 1)
    assert chunk & (chunk - 1) == 0
    chunk_shift = chunk.bit_length() - 1
    same_chunk = (ri >> chunk_shift) == (ci >> chunk_shift)
    tri = jnp.where(jnp.logical_and(same_chunk, ci <= ri), 1.0, 0.0).astype(BF16)
    tri_t = jnp.where(jnp.logical_and(same_chunk, ri <= ci), 1.0, 0.0).astype(BF16)
    b_col = sum(_dot(tri, part) for part in _split3(lf_col))
    b_row = sum(_dot(part, tri_t) for part in _split3(lf_row))

    causal = (lax.broadcasted_iota(jnp.int32, (chunk, chunk), 1)
              <= lax.broadcasted_iota(jnp.int32, (chunk, chunk), 0))
    assert chunk == tile
    first = {}

    def block_scores(g, hd):
        lo, hi = hd * M_DH, (hd + 1) * M_DH
        cmat = c_s[g, hd]
        m_prev = m_s[g, hd:hd + 1, 0:1]
        bc = b_col[span(g), M_HEADS + hd:M_HEADS + hd + 1]
        br = b_row[M_HEADS + hd:M_HEADS + hd + 1, span(g)]
        igr = g_row[hd:hd + 1, span(g)]
        dmat = jnp.where(causal, bc + (igr - br), NEG)
        inter = bc + m_prev
        mt = jnp.maximum(inter, jnp.max(dmat, axis=-1, keepdims=True))
        q = mq[g][:, lo:hi]
        k = mk[g][:, lo:hi]
        vb = z_cur[span(g), ZC_MV + lo:ZC_MV + hi].astype(BF16)
        qb = q.astype(BF16)
        first[g, hd] = dict(bc=bc, mt=mt, inter=inter, m_prev=m_prev, cmat=cmat, q=q, k=k, vb=vb, dmat=dmat,
                            qk=_dot_nt(qb, k.astype(BF16)), qc=_dot(qb, cmat.astype(BF16)))

    def block_output(g, hd):
        lo, hi = hd * M_DH, (hd + 1) * M_DH
        f = first[g, hd]
        nrow = n_s[g, hd:hd + 1, :]
        s = f["qk"] * jnp.exp(f["dmat"] - f["mt"])
        iw = jnp.exp(f["inter"] - f["mt"])
        num = iw * f["qc"] + _dot(s.astype(BF16), f["vb"])
        den = iw * jnp.sum(f["q"] * nrow, axis=-1, keepdims=True) + jnp.sum(s, axis=-1, keepdims=True)
        mh = num / jnp.maximum(jnp.abs(den), jnp.exp(-f["mt"]))
        mh = mh * _sigmoid(z_cur[span(g), ZC_MO + lo:ZC_MO + hi])
        mix_s[span(g), A_WIDTH + lo:A_WIDTH + hi] = _rms(mh, normm_ref[:, lo:hi])

    def block_state(g, hd):
        f = first[g, hd]
        bc, mt, m_prev = f["bc"], f["mt"], f["m_prev"]
        igc = g_col[span(g), hd:hd + 1]
        b_last = bc[tile - 1:tile, :]
        m_new = mt[tile - 1:tile, :]
        kw = f["k"] * jnp.exp(b_last - bc + igc - m_new)
        decay = jnp.exp(b_last + m_prev - m_new)
        c_s[g, hd] = decay * f["cmat"] + _dot(kw.T.astype(BF16), f["vb"])
        n_s[g, hd:hd + 1, :] = decay * n_s[g, hd:hd + 1, :] + jnp.sum(kw, axis=0, keepdims=True)
        m_s[g, hd:hd + 1, :] = jnp.broadcast_to(m_new, (1, LANES))

    blocks = [(g, hd) for hd in range(M_HEADS) for g in streams]
    if group == 1:
        for blk in blocks:
            project_next(3)
            block_scores(*blk)
            block_output(*blk)
            block_state(*blk)
    else:
        for phase in (block_scores, block_output, block_state):
            for blk in blocks:
                project_next()
                phase(*blk)

    for _ in units:
        pass
    co_ref[...] = c_s[...]
    no_ref[...] = n_s[...]
    mo_ref[...] = m_s[...]
    resid = x1_cur[...] if fused_ffn else xb_ref[...]
    y_ref[...] = resid + _dot(mix_s[...].astype(BF16), wout_ref[...])

    if fused_ffn:
        x1_cur[...] = x1_mid[...]
        x1_mid[...] = x1_new[...]
    for g in streams:
        ubuf[g, 0:SUBLANES, :] = last_rows[g]
        ubuf[g, SUBLANES:SUBLANES + tile, :] = z_new[span(g), OFF_MQK:OFF_MV]
    z_cur[:, ZC_MV:ZC_WIDTH] = z_new[:, OFF_MV:D_IN_PAD]
    g_cur[...] = g_new[...]


def _bias_pieces(rel_bias, tile, band_rows, chunked):
    heads = rel_bias.shape[0]
    span = band_rows + tile - 1
    width = -(-span // LANES) * LANES
    n_far = ATT_BAND + tile - REL_CLIP
    n_near = width - n_far - (2 * REL_CLIP - 1)
    diag = jnp.concatenate([jnp.broadcast_to(rel_bias[:, 2 * REL_CLIP:], (heads, n_far)),
                            rel_bias[:, 2 * REL_CLIP - 1:0:-1],
                            jnp.broadcast_to(rel_bias[:, :1], (heads, n_near))], axis=1)
    i = np.arange(tile)[:, None]
    j = np.arange(band_rows)[None, :]
    visible = j < ATT_BAND + tile
    if chunked:
        qc = i // CHUNK
        kc = (j - ATT_BAND) // CHUNK
        visible = visible & (kc <= qc) & (kc >= qc - ATT_BAND // CHUNK)
    mask = jnp.asarray(np.where(visible, 0.0, NEG * LOG2E), F32)
    return diag * LOG2E, mask


def _mixer(x, params, state, *, tile, chunk, pos0, chunked, ffn=None):
    nb, frames, d = x.shape
    gmix, w_in, conv_w, conv_b, gate_bias, rel_bias, norm_m, w_out = params
    n_tiles = frames // tile
    band_rows = ATT_BAND + -(-tile // LANES) * LANES
    keep_tiles = min(ATT_BAND, frames) // tile
    width2 = 2 * M_WIDTH
    fused_ffn = ffn is not None
    has_state = state is not None
    group = max(1, min(nb, STEP_ROWS // tile)) if n_tiles == 1 else 1
    assert nb % group == 0
    rows = group * tile

    win = jnp.concatenate([w_in, jnp.zeros((d, D_IN_PAD - w_in.shape[1]), F32)], axis=1).astype(BF16)
    wgt = jnp.zeros((GATE_ROWS, d), F32).at[:N_GATES].set(w_in[:, OFF_MG:].T).astype(BF16)
    gb_row = jnp.zeros((1, LANES), F32).at[0, :N_GATES].set(gate_bias)
    gb_col = jnp.zeros((GATE_ROWS, 1), F32).at[:N_GATES, 0].set(gate_bias)
    diag, mask = _bias_pieces(rel_bias, tile, band_rows, chunked)

    lag = 2 if fused_ffn else 1
    n_tiles_all = (nb // group) * n_tiles
    n_steps = n_tiles_all + lag
    entering = lambda p: jnp.minimum(p, n_tiles_all - 1)
    projected = lambda p: jnp.clip(p - (lag - 1), 0, n_tiles_all - 1)
    mixed = lambda p: jnp.maximum(p - lag, 0)
    per_group = lambda *dims: pl.BlockSpec((group,) + dims,
                                           lambda p: (lax.div(mixed(p), n_tiles),) + (0,) * len(dims))
    enter_spec = pl.BlockSpec((rows, d), lambda p: (entering(p), 0))
    mix_spec = pl.BlockSpec((rows, d), lambda p: (mixed(p), 0))

    def kv_index(p):
        q = projected(p)
        return (lax.div(q, n_tiles) * keep_tiles + jnp.maximum(lax.rem(q, n_tiles) - (n_tiles - keep_tiles), 0), 0)

    kv_spec = pl.BlockSpec((rows, A_WIDTH), kv_index)

    x2d = x.reshape(nb * frames, d)
    args, in_specs = [x2d], [enter_spec]
    scratch_ffn = []
    if fused_ffn:
        g1, w1, w3, w2 = ffn
        args += [g1.reshape(1, d), w1, w3, w2]
        in_specs += [_resident((1, d)), _resident(w1.shape), _resident(w3.shape), _resident(w2.shape)]
        scratch_ffn = [pltpu.VMEM((rows, w1.shape[1]), BF16), pltpu.VMEM((rows, d), BF16)] + [pltpu.VMEM((rows, d), F32)] * 3
    else:
        args.append(x2d)
        in_specs.append(mix_spec)
    args += [gmix.reshape(1, d), win, wgt, conv_w, conv_b.reshape(1, width2), gb_row, gb_col, diag, mask,
             norm_m.reshape(1, M_WIDTH), w_out.astype(BF16)]
    in_specs += [_resident((1, d)), _resident(win.shape), _resident(wgt.shape),
                 _resident((CONV_W, width2)), _resident((1, width2)), _resident((1, LANES)),
                 _resident((GATE_ROWS, 1)), _resident(diag.shape), _resident(mask.shape), _resident((1, M_WIDTH)),
                 _resident((d, d))]
    if has_state:
        k0, v0, c0, n0, m0, conv0 = state
        conv0p = jnp.concatenate([jnp.zeros((nb, SUBLANES - (CONV_W - 1), width2), F32), conv0], axis=1)
        m0p = jnp.broadcast_to(m0[:, :, None], (nb, M_HEADS, LANES))
        args += [k0, v0, c0, n0, m0p, conv0p]
        in_specs += [per_group(ATT_BAND, A_WIDTH), per_group(ATT_BAND, A_WIDTH),
                     per_group(M_HEADS, M_DH, M_DH), per_group(M_HEADS, M_DH), per_group(M_HEADS, LANES),
                     per_group(SUBLANES, width2)]
    out_specs = [
        mix_spec, kv_spec, kv_spec, per_group(M_HEADS, M_DH, M_DH), per_group(M_HEADS, M_DH),
        per_group(M_HEADS, LANES), per_group(SUBLANES, width2),
    ]
    keep = keep_tiles * tile
    out_shape = [
        jax.ShapeDtypeStruct((nb * frames, d), F32),
        jax.ShapeDtypeStruct((nb * keep, A_WIDTH), F32),
        jax.ShapeDtypeStruct((nb * keep, A_WIDTH), F32),
        jax.ShapeDtypeStruct((nb, M_HEADS, M_DH, M_DH), F32),
        jax.ShapeDtypeStruct((nb, M_HEADS, M_DH), F32),
        jax.ShapeDtypeStruct((nb, M_HEADS, LANES), F32),
        jax.ShapeDtypeStruct((nb, SUBLANES, width2), F32),
    ]
    scratch = [
        pltpu.VMEM((rows, D_IN_PAD), F32), pltpu.VMEM((rows, ZC_WIDTH), F32),
        pltpu.VMEM((GATE_ROWS, rows), F32), pltpu.VMEM((GATE_ROWS, rows), F32),
        pltpu.VMEM((group, band_rows, A_WIDTH), BF16), pltpu.VMEM((group, band_rows, A_WIDTH), BF16),
        pltpu.VMEM((group, tile + SUBLANES, width2), F32),
        pltpu.VMEM((group, M_HEADS, M_DH, M_DH), F32), pltpu.VMEM((group, M_HEADS, M_DH), F32),
        pltpu.VMEM((group, M_HEADS, LANES), F32),
        pltpu.VMEM((rows, d), F32), pltpu.VMEM((rows, d), BF16),
        pltpu.VMEM((A_HEADS, tile, band_rows), F32),
    ] + scratch_ffn
    y, ko, vo, c1, n1, m1, conv1 = pl.pallas_call(
        functools.partial(_mixer_kernel, tile=tile, group=group, chunk=chunk, band_rows=band_rows, pos0=pos0,
                          n_tiles=n_tiles, fused_ffn=fused_ffn, has_state=has_state,
                          ahead=2 if group == 1 else ATTENTION_AHEAD),
        grid=(n_steps,),
        in_specs=in_specs,
        out_specs=out_specs,
        out_shape=out_shape,
        scratch_shapes=scratch,
        compiler_params=pltpu.CompilerParams(dimension_semantics=("arbitrary",),
                                             vmem_limit_bytes=VMEM_LIMIT_BYTES),
        name="mixer_chunked" if chunked else "mixer_step",
    )(*args)
    new_state = (ko.reshape(nb, keep, A_HEADS, A_DH), vo.reshape(nb, keep, A_HEADS, A_DH), c1, n1,
                 m1[:, :, 0], conv1[:, SUBLANES - (CONV_W - 1):, :])
    return y.reshape(nb, frames, d), new_state


def kernel(x_prompt, x_sample, cache_attn_k, cache_attn_v, state_mlstm_C, state_mlstm_n, state_mlstm_m, state_mlstm_conv, norm_ffn1, w1_ffn1, w3_ffn1, w2_ffn1, norm_mix, w_in, conv_w, conv_b, gate_bias, rel_bias, norm_mlstm_out, w_out, norm_ffn2, w1_ffn2, w3_ffn2, w2_ffn2, norm_final):
    depth = norm_ffn1.shape[0]
    nbp, seq, d = x_prompt.shape
    nbs, dec, _ = x_sample.shape
    xp = x_prompt
    xs = x_sample.reshape(nbs * dec, d)
    prompt_tile = min(PROMPT_TILE, seq)
    new_p, new_s = [], []
    for l in range(depth):
        last = l == depth - 1
        ffn1 = _ffn_weights(w1_ffn1[l], w3_ffn1[l], w2_ffn1[l])
        ffn2 = _ffn_weights(w1_ffn2[l], w3_ffn2[l], w2_ffn2[l])
        mix = (norm_mix[l], w_in[l], conv_w[l], conv_b[l], gate_bias[l], rel_bias[l], norm_mlstm_out[l], w_out[l])
        gf = norm_final if last else None

        xp, st = _mixer(xp, mix, None, tile=prompt_tile, chunk=prompt_tile, pos0=0, chunked=True,
                        ffn=(norm_ffn1[l],) + ffn1)
        new_p.append(st)
        xp = _ffn(xp.reshape(nbp * seq, d), norm_ffn2[l], ffn2, gf).reshape(nbp, seq, d)

        xs = _ffn(xs, norm_ffn1[l], ffn1)
        cache = (cache_attn_k[l].reshape(nbs, -1, A_WIDTH), cache_attn_v[l].reshape(nbs, -1, A_WIDTH),
                 state_mlstm_C[l], state_mlstm_n[l], state_mlstm_m[l], state_mlstm_conv[l])
        xs3, st = _mixer(xs.reshape(nbs, dec, d), mix, cache, tile=dec, chunk=dec, pos0=PAST_LEN, chunked=False)
        new_s.append(st)
        xs = _ffn(xs3.reshape(nbs * dec, d), norm_ffn2[l], ffn2, gf)

    stack = lambda states, i: jnp.stack([s[i] for s in states])
    return ((xp, xs.reshape(nbs, dec, d))
            + tuple(stack(new_p, i) for i in range(6)) + tuple(stack(new_s, i) for i in range(6)))
```

```python
import functools

import numpy as np
import jax
import jax.numpy as jnp
from jax import lax
from jax.experimental import pallas as pl
from jax.experimental.pallas import tpu as pltpu

F32 = jnp.float32
BF16 = jnp.bfloat16

CHUNK = 64
ATT_BAND = 8 * CHUNK
A_HEADS = 8
A_DH = 64
A_WIDTH = A_HEADS * A_DH
M_HEADS = 4
M_DH = 128
M_WIDTH = M_HEADS * M_DH
REL_CLIP = 128
CONV_W = 4
PAST_LEN = 4096
EPS = 1e-6
NEG = -1e30
LOG2E = 1.4426950408889634

LANES = 128
SUBLANES = 8
MXU_DIM = 256
VMEM_LIMIT_BYTES = 60 * 1024 * 1024

OFF_AK = A_WIDTH
OFF_AV = 2 * A_WIDTH
OFF_MQK = 3 * A_WIDTH
OFF_MV = OFF_MQK + 2 * M_WIDTH
OFF_MO = OFF_MV + M_WIDTH
OFF_MG = OFF_MO + M_WIDTH
N_GATES = 2 * M_HEADS
D_IN_PAD = OFF_MG + LANES
GATE_ROWS = 16
ZC_MV = A_WIDTH
ZC_MO = ZC_MV + M_WIDTH
ZC_MG = ZC_MO + M_WIDTH
ZC_WIDTH = ZC_MG + LANES

FFN_ROWS = 512
FFN_CHUNK = MXU_DIM
PROMPT_TILE = 256
STEP_ROWS = 64
ATTENTION_AHEAD = 4


def _rms(x, g):
    return x * lax.rsqrt(jnp.mean(x * x, axis=-1, keepdims=True) + EPS) * g


def _sigmoid(x):
    return 1.0 / (1.0 + jnp.exp(-x))


def _log_sigmoid(x):
    return jnp.minimum(x, 0.0) - jnp.log1p(jnp.exp(-jnp.abs(x)))


def _dot(a, b):
    return jnp.dot(a, b, preferred_element_type=F32)


def _dot_nt(a, b):
    return lax.dot_general(a, b, (((1,), (1,)), ((), ())), preferred_element_type=F32)


def _split3(x):
    p1 = x.astype(BF16)
    r1 = x - p1.astype(F32)
    p2 = r1.astype(BF16)
    p3 = (r1 - p2.astype(F32)).astype(BF16)
    return p1, p2, p3


def _resident(shape):
    nd = len(shape)
    return pl.BlockSpec(shape, lambda *_: (0,) * nd, pipeline_mode=pl.Buffered(1))


def _ffn_kernel(x_ref, g_ref, w1_ref, w3_ref, w2_ref, *rest, final_norm):
    if final_norm:
        gf_ref, o_ref, h_ref, u_ref = rest
    else:
        o_ref, h_ref, u_ref = rest
    h_ref[...] = _rms(x_ref[...], g_ref[...]).astype(BF16)
    for c in range(0, u_ref.shape[1], FFN_CHUNK):
        h = h_ref[...]
        a = _dot(h, w1_ref[:, c:c + FFN_CHUNK])
        b = _dot(h, w3_ref[:, c:c + FFN_CHUNK])
        u_ref[:, c:c + FFN_CHUNK] = (a * _sigmoid(a) * b).astype(BF16)
    y = x_ref[...] + 0.5 * _dot(u_ref[...], w2_ref[...])
    if final_norm:
        y = _rms(y, gf_ref[...])
    o_ref[...] = y


def _ffn_weights(w1, w3, w2):
    return w1.astype(BF16), w3.astype(BF16), w2.astype(BF16)


def _ffn(x2d, g, weights, gf=None):
    n, d = x2d.shape
    w1, w3, w2 = weights
    f = w1.shape[1]
    assert f % FFN_CHUNK == 0
    rows = min(FFN_ROWS, n)
    final_norm = gf is not None
    row_spec = pl.BlockSpec((rows, d), lambda i: (i, 0))
    in_specs = [row_spec, _resident((1, d)), _resident(w1.shape), _resident(w3.shape), _resident(w2.shape)]
    args = [x2d, g.reshape(1, d), w1, w3, w2]
    if final_norm:
        in_specs.append(_resident((1, d)))
        args.append(gf.reshape(1, d))
    return pl.pallas_call(
        functools.partial(_ffn_kernel, final_norm=final_norm),
        grid=(n // rows,),
        in_specs=in_specs,
        out_specs=row_spec,
        out_shape=jax.ShapeDtypeStruct((n, d), F32),
        scratch_shapes=[pltpu.VMEM((rows, d), BF16), pltpu.VMEM((rows, f), BF16)],
        compiler_params=pltpu.CompilerParams(dimension_semantics=("arbitrary",),
                                             vmem_limit_bytes=VMEM_LIMIT_BYTES),
        name="ffn_final" if final_norm else "ffn",
    )(*args)


def _mixer_kernel(*refs, tile, group, chunk, band_rows, pos0, n_tiles, fused_ffn, has_state, ahead):
    refs = iter(refs)
    take = lambda n: [next(refs) for _ in range(n)]
    (xa_ref,) = take(1)
    if fused_ffn:
        g1_ref, w1_ref, w3_ref, w2_ref = take(4)
    else:
        (xb_ref,) = take(1)
    gmix_ref, win_ref, wgt_ref, convw_ref, convb_ref, gbrow_ref, gbcol_ref, diag_ref, mask_ref, normm_ref, wout_ref = take(11)
    if has_state:
        k0_ref, v0_ref, c0_ref, n0_ref, m0_ref, conv0_ref = take(6)
    y_ref, ko_ref, vo_ref, co_ref, no_ref, mo_ref, convo_ref = take(7)
    z_new, z_cur, g_new, g_cur, kband, vband, ubuf, c_s, n_s, m_s, mix_s, h_s, tab_s = take(13)
    if fused_ffn:
        u_s, h1_s, x1_new, x1_mid, x1_cur = take(5)
    assert group == 1 or n_tiles == 1
    streams = range(group)
    span = lambda g: slice(g * tile, (g + 1) * tile)

    p = pl.program_id(0)
    lag = 2 if fused_ffn else 1
    t = lax.rem(jnp.maximum(p - lag, 0), n_tiles)

    @pl.when(p == 0)
    def _first_step():
        z_cur[...] = jnp.zeros_like(z_cur)
        g_cur[...] = jnp.zeros_like(g_cur)
        for g in streams:
            kband[g, ATT_BAND:band_rows, :] = jnp.zeros((band_rows - ATT_BAND, A_WIDTH), BF16)
            vband[g, ATT_BAND:band_rows, :] = jnp.zeros((band_rows - ATT_BAND, A_WIDTH), BF16)
            ubuf[g, SUBLANES:SUBLANES + tile, :] = jnp.zeros((tile, 2 * M_WIDTH), F32)
        if fused_ffn:
            x1_cur[...] = jnp.zeros_like(x1_cur)
            x1_mid[...] = jnp.zeros_like(x1_mid)
        width = diag_ref.shape[1]
        for head in range(A_HEADS):
            rows_of_diag = jnp.broadcast_to(diag_ref[head:head + 1, :], (tile, width))
            skew = pltpu.roll(rows_of_diag, width - (tile - 1), 1, stride=1, stride_axis=0)
            tab_s[head] = skew[:, :band_rows] + mask_ref[...]

    @pl.when(t == 0)
    def _load_state():
        if has_state:
            for g in streams:
                for lo in range(0, A_WIDTH, LANES):
                    h0 = lo // A_DH
                    for band, cache in ((kband, k0_ref), (vband, v0_ref)):
                        halves = [cache[g, pl.ds(h0 + j, ATT_BAND, stride=A_HEADS), :] for j in range(LANES // A_DH)]
                        band[g, 0:ATT_BAND, lo:lo + LANES] = jnp.concatenate(halves, axis=1).astype(BF16)
                ubuf[g, 0:SUBLANES, :] = conv0_ref[g]
            c_s[...] = c0_ref[...]
            n_s[...] = n0_ref[...]
            m_s[...] = m0_ref[...]
        else:
            for g in streams:
                kband[g, 0:ATT_BAND, :] = jnp.zeros((ATT_BAND, A_WIDTH), BF16)
                vband[g, 0:ATT_BAND, :] = jnp.zeros((ATT_BAND, A_WIDTH), BF16)
                ubuf[g, 0:SUBLANES, :] = jnp.zeros((SUBLANES, 2 * M_WIDTH), F32)
            c_s[...] = jnp.zeros_like(c_s)
            n_s[...] = jnp.zeros_like(n_s)
            m_s[...] = jnp.zeros_like(m_s)

    def stage_a1():
        h1_s[...] = _rms(xa_ref[...], g1_ref[...]).astype(BF16)
        yield
        for c in range(0, u_s.shape[1], FFN_CHUNK):
            hf = h1_s[...]
            a = _dot(hf, w1_ref[:, c:c + FFN_CHUNK])
            b = _dot(hf, w3_ref[:, c:c + FFN_CHUNK])
            u_s[:, c:c + FFN_CHUNK] = (a * _sigmoid(a) * b).astype(BF16)
            yield
        for c in range(0, x1_new.shape[1], MXU_DIM):
            x1_new[:, c:c + MXU_DIM] = (xa_ref[:, c:c + MXU_DIM]
                                        + 0.5 * _dot(u_s[...], w2_ref[:, c:c + MXU_DIM]))
            yield

    def stage_a2():
        x_in = x1_mid[...] if fused_ffn else xa_ref[...]
        h_s[...] = _rms(x_in, gmix_ref[...]).astype(BF16)
        yield
        for c in range(0, D_IN_PAD, MXU_DIM):
            c1 = min(c + MXU_DIM, D_IN_PAD)
            z_new[:, c:c1] = _dot(h_s[...], win_ref[:, c:c1])
            projected_cols[0] = c1
            yield
        g_new[...] = _dot_nt(wgt_ref[...], h_s[...])

    projected_cols = [0]

    def alternate(*gens):
        gens = list(gens)
        while gens:
            for gen in list(gens):
                try:
                    next(gen)
                    yield
                except StopIteration:
                    gens.remove(gen)

    units = alternate(stage_a1(), stage_a2()) if fused_ffn else stage_a2()

    def project_next(count=1):
        for _ in range(count):
            next(units, None)

    lane = lax.broadcasted_iota(jnp.int32, (1, LANES), 1)
    even = lane < A_DH
    if pos0 < ATT_BAND:
        col = lax.broadcasted_iota(jnp.int32, (1, band_rows), 1)
        in_stream = col >= (ATT_BAND - pos0) - t * tile

    def scores(g, head):
        lo = (head // 2) * LANES
        mine = even if head % 2 == 0 else jnp.logical_not(even)
        qh = jnp.where(mine, z_cur[span(g), lo:lo + LANES] * (LOG2E * A_DH ** -0.5), 0.0).astype(BF16)
        s = _dot_nt(qh, kband[g, :, lo:lo + LANES]) + tab_s[head]
        if pos0 < ATT_BAND:
            s = jnp.where(in_stream, s, NEG)
        return s

    def attend(g, head, s):
        lo = (head // 2) * LANES
        mine = even if head % 2 == 0 else jnp.logical_not(even)
        vp = vband[g, :, lo:lo + LANES]
        e = jnp.exp2(s - jnp.max(s, axis=-1, keepdims=True))
        o = _dot(e.astype(BF16), jnp.where(mine, vp, jnp.zeros_like(vp)))
        o = o * (1.0 / jnp.sum(e, axis=-1, keepdims=True))
        if head % 2 == 0:
            mix_s[span(g), lo:lo + LANES] = o
        else:
            mix_s[span(g), lo:lo + LANES] += o

    pending = []
    for g in streams:
        for head in range(A_HEADS):
            pending.append((g, head, scores(g, head)))
            project_next(lag)
            if len(pending) > ahead:
                attend(*pending.pop(0))
    for item in pending:
        attend(*item)
    assert projected_cols[0] >= OFF_MQK
    ko_ref[...] = z_new[:, OFF_AK:OFF_AV]
    vo_ref[...] = z_new[:, OFF_AV:OFF_MQK]
    for g in streams:
        if n_tiles > 1:
            kband[g, 0:ATT_BAND, :] = kband[g, tile:tile + ATT_BAND, :]
            vband[g, 0:ATT_BAND, :] = vband[g, tile:tile + ATT_BAND, :]
        kband[g, ATT_BAND:ATT_BAND + tile, :] = z_new[span(g), OFF_AK:OFF_AV].astype(BF16)
        vband[g, ATT_BAND:ATT_BAND + tile, :] = z_new[span(g), OFF_AV:OFF_MQK].astype(BF16)
    z_cur[:, 0:OFF_AK] = z_new[:, 0:OFF_AK]

    project_next(2)
    mq, mk, last_rows = [], [], []
    for g in streams:
        qk = convb_ref[...]
        for j in range(CONV_W):
            start = SUBLANES - (CONV_W - 1) + j
            qk = qk + ubuf[g, start:start + tile, :] * convw_ref[j:j + 1, :]
        qk = qk * _sigmoid(qk)
        mq.append(qk[:, :M_WIDTH])
        mk.append(qk[:, M_WIDTH:] * (M_DH ** -0.5))
        last_rows.append(ubuf[g, tile:tile + SUBLANES, :])
        convo_ref[g] = last_rows[g]

    project_next(2)
    rows = group * tile
    g_col = z_cur[:, ZC_MG:ZC_MG + LANES] + gbrow_ref[...]
    g_row = g_cur[...] + gbcol_ref[...]
    lf_col = _log_sigmoid(g_col)
    lf_row = _log_sigmoid(g_row)
    ri = lax.broadcasted_iota(jnp.int32, (rows, rows), 0)
    ci = lax.broadcasted_iota(jnp.int32, (rows, rows), 1)
    assert chunk & (chunk - 1) == 0
    chunk_shift = chunk.bit_length() - 1
    same_chunk = (ri >> chunk_shift) == (ci >> chunk_shift)
    tri = jnp.where(jnp.logical_and(same_chunk, ci <= ri), 1.0, 0.0).astype(BF16)
    tri_t = jnp.where(jnp.logical_and(same_chunk, ri <= ci), 1.0, 0.0).astype(BF16)
    b_col = sum(_dot(tri, part) for part in _split3(lf_col))
    b_row = sum(_dot(part, tri_t) for part in _split3(lf_row))

    causal = (lax.broadcasted_iota(jnp.int32, (chunk, chunk), 1)
              <= lax.broadcasted_iota(jnp.int32, (chunk, chunk), 0))
    assert chunk == tile
    first = {}

    def block_scores(g, hd):
        lo, hi = hd * M_DH, (hd + 1) * M_DH
        cmat = c_s[g, hd]
        m_prev = m_s[g, hd:hd + 1, 0:1]
        bc = b_col[span(g), M_HEADS + hd:M_HEADS + hd + 1]
        br = b_row[M_HEADS + hd:M_HEADS + hd + 1, span(g)]
        igr = g_row[hd:hd + 1, span(g)]
        dmat = jnp.where(causal, bc + (igr - br), NEG)
        inter = bc + m_prev
        mt = jnp.maximum(inter, jnp.max(dmat, axis=-1, keepdims=True))
        q = mq[g][:, lo:hi]
        k = mk[g][:, lo:hi]
        vb = z_cur[span(g), ZC_MV + lo:ZC_MV + hi].astype(BF16)
        qb = q.astype(BF16)
        first[g, hd] = dict(bc=bc, mt=mt, inter=inter, m_prev=m_prev, cmat=cmat, q=q, k=k, vb=vb, dmat=dmat,
                            qk=_dot_nt(qb, k.astype(BF16)), qc=_dot(qb, cmat.astype(BF16)))

    def block_output(g, hd):
        lo, hi = hd * M_DH, (hd + 1) * M_DH
        f = first[g, hd]
        nrow = n_s[g, hd:hd + 1, :]
        s = f["qk"] * jnp.exp(f["dmat"] - f["mt"])
        iw = jnp.exp(f["inter"] - f["mt"])
        num = iw * f["qc"] + _dot(s.astype(BF16), f["vb"])
        den = iw * jnp.sum(f["q"] * nrow, axis=-1, keepdims=True) + jnp.sum(s, axis=-1, keepdims=True)
        mh = num / jnp.maximum(jnp.abs(den), jnp.exp(-f["mt"]))
        mh = mh * _sigmoid(z_cur[span(g), ZC_MO + lo:ZC_MO + hi])
        mix_s[span(g), A_WIDTH + lo:A_WIDTH + hi] = _rms(mh, normm_ref[:, lo:hi])

    def block_state(g, hd):
        f = first[g, hd]
        bc, mt, m_prev = f["bc"], f["mt"], f["m_prev"]
        igc = g_col[span(g), hd:hd + 1]
        b_last = bc[tile - 1:tile, :]
        m_new = mt[tile - 1:tile, :]
        kw = f["k"] * jnp.exp(b_last - bc + igc - m_new)
        decay = jnp.exp(b_last + m_prev - m_new)
        c_s[g, hd] = decay * f["cmat"] + _dot(kw.T.astype(BF16), f["vb"])
        n_s[g, hd:hd + 1, :] = decay * n_s[g, hd:hd + 1, :] + jnp.sum(kw, axis=0, keepdims=True)
        m_s[g, hd:hd + 1, :] = jnp.broadcast_to(m_new, (1, LANES))

    blocks = [(g, hd) for hd in range(M_HEADS) for g in streams]
    if group == 1:
        for blk in blocks:
            project_next(3)
            block_scores(*blk)
            block_output(*blk)
            block_state(*blk)
    else:
        for phase in (block_scores, block_output, block_state):
            for blk in blocks:
                project_next()
                phase(*blk)

    for _ in units:
        pass
    co_ref[...] = c_s[...]
    no_ref[...] = n_s[...]
    mo_ref[...] = m_s[...]
    resid = x1_cur[...] if fused_ffn else xb_ref[...]
    y_ref[...] = resid + _dot(mix_s[...].astype(BF16), wout_ref[...])

    if fused_ffn:
        x1_cur[...] = x1_mid[...]
        x1_mid[...] = x1_new[...]
    for g in streams:
        ubuf[g, 0:SUBLANES, :] = last_rows[g]
        ubuf[g, SUBLANES:SUBLANES + tile, :] = z_new[span(g), OFF_MQK:OFF_MV]
    z_cur[:, ZC_MV:ZC_WIDTH] = z_new[:, OFF_MV:D_IN_PAD]
    g_cur[...] = g_new[...]


def _bias_pieces(rel_bias, tile, band_rows, chunked):
    heads = rel_bias.shape[0]
    span = band_rows + tile - 1
    width = -(-span // LANES) * LANES
    n_far = ATT_BAND + tile - REL_CLIP
    n_near = width - n_far - (2 * REL_CLIP - 1)
    diag = jnp.concatenate([jnp.broadcast_to(rel_bias[:, 2 * REL_CLIP:], (heads, n_far)),
                            rel_bias[:, 2 * REL_CLIP - 1:0:-1],
                            jnp.broadcast_to(rel_bias[:, :1], (heads, n_near))], axis=1)
    i = np.arange(tile)[:, None]
    j = np.arange(band_rows)[None, :]
    visible = j < ATT_BAND + tile
    if chunked:
        qc = i // CHUNK
        kc = (j - ATT_BAND) // CHUNK
        visible = visible & (kc <= qc) & (kc >= qc - ATT_BAND // CHUNK)
    mask = jnp.asarray(np.where(visible, 0.0, NEG * LOG2E), F32)
    return diag * LOG2E, mask


def _mixer(x, params, state, *, tile, chunk, pos0, chunked, ffn=None):
    nb, frames, d = x.shape
    gmix, w_in, conv_w, conv_b, gate_bias, rel_bias, norm_m, w_out = params
    n_tiles = frames // tile
    band_rows = ATT_BAND + -(-tile // LANES) * LANES
    keep_tiles = min(ATT_BAND, frames) // tile
    width2 = 2 * M_WIDTH
    fused_ffn = ffn is not None
    has_state = state is not None
    group = max(1, min(nb, STEP_ROWS // tile)) if n_tiles == 1 else 1
    assert nb % group == 0
    rows = group * tile

    win = jnp.concatenate([w_in, jnp.zeros((d, D_IN_PAD - w_in.shape[1]), F32)], axis=1).astype(BF16)
    wgt = jnp.zeros((GATE_ROWS, d), F32).at[:N_GATES].set(w_in[:, OFF_MG:].T).astype(BF16)
    gb_row = jnp.zeros((1, LANES), F32).at[0, :N_GATES].set(gate_bias)
    gb_col = jnp.zeros((GATE_ROWS, 1), F32).at[:N_GATES, 0].set(gate_bias)
    diag, mask = _bias_pieces(rel_bias, tile, band_rows, chunked)

    lag = 2 if fused_ffn else 1
    n_tiles_all = (nb // group) * n_tiles
    n_steps = n_tiles_all + lag
    entering = lambda p: jnp.minimum(p, n_tiles_all - 1)
    projected = lambda p: jnp.clip(p - (lag - 1), 0, n_tiles_all - 1)
    mixed = lambda p: jnp.maximum(p - lag, 0)
    per_group = lambda *dims: pl.BlockSpec((group,) + dims,
                                           lambda p: (lax.div(mixed(p), n_tiles),) + (0,) * len(dims))
    enter_spec = pl.BlockSpec((rows, d), lambda p: (entering(p), 0))
    mix_spec = pl.BlockSpec((rows, d), lambda p: (mixed(p), 0))

    def kv_index(p):
        q = projected(p)
        return (lax.div(q, n_tiles) * keep_tiles + jnp.maximum(lax.rem(q, n_tiles) - (n_tiles - keep_tiles), 0), 0)

    kv_spec = pl.BlockSpec((rows, A_WIDTH), kv_index)

    x2d = x.reshape(nb * frames, d)
    args, in_specs = [x2d], [enter_spec]
    scratch_ffn = []
    if fused_ffn:
        g1, w1, w3, w2 = ffn
        args += [g1.reshape(1, d), w1, w3, w2]
        in_specs += [_resident((1, d)), _resident(w1.shape), _resident(w3.shape), _resident(w2.shape)]
        scratch_ffn = [pltpu.VMEM((rows, w1.shape[1]), BF16), pltpu.VMEM((rows, d), BF16)] + [pltpu.VMEM((rows, d), F32)] * 3
    else:
        args.append(x2d)
        in_specs.append(mix_spec)
    args += [gmix.reshape(1, d), win, wgt, conv_w, conv_b.reshape(1, width2), gb_row, gb_col, diag, mask,
             norm_m.reshape(1, M_WIDTH), w_out.astype(BF16)]
    in_specs += [_resident((1, d)), _resident(win.shape), _resident(wgt.shape),
                 _resident((CONV_W, width2)), _resident((1, width2)), _resident((1, LANES)),
                 _resident((GATE_ROWS, 1)), _resident(diag.shape), _resident(mask.shape), _resident((1, M_WIDTH)),
                 _resident((d, d))]
    if has_state:
        k0, v0, c0, n0, m0, conv0 = state
        conv0p = jnp.concatenate([jnp.zeros((nb, SUBLANES - (CONV_W - 1), width2), F32), conv0], axis=1)
        m0p = jnp.broadcast_to(m0[:, :, None], (nb, M_HEADS, LANES))
        args += [k0, v0, c0, n0, m0p, conv0p]
        in_specs += [per_group(ATT_BAND * A_HEADS, A_DH), per_group(ATT_BAND * A_HEADS, A_DH),
                     per_group(M_HEADS, M_DH, M_DH), per_group(M_HEADS, M_DH), per_group(M_HEADS, LANES),
                     per_group(SUBLANES, width2)]
    out_specs = [
        mix_spec, kv_spec, kv_spec, per_group(M_HEADS, M_DH, M_DH), per_group(M_HEADS, M_DH),
        per_group(M_HEADS, LANES), per_group(SUBLANES, width2),
    ]
    keep = keep_tiles * tile
    out_shape = [
        jax.ShapeDtypeStruct((nb * frames, d), F32),
        jax.ShapeDtypeStruct((nb * keep, A_WIDTH), F32),
        jax.ShapeDtypeStruct((nb * keep, A_WIDTH), F32),
        jax.ShapeDtypeStruct((nb, M_HEADS, M_DH, M_DH), F32),
        jax.ShapeDtypeStruct((nb, M_HEADS, M_DH), F32),
        jax.ShapeDtypeStruct((nb, M_HEADS, LANES), F32),
        jax.ShapeDtypeStruct((nb, SUBLANES, width2), F32),
    ]
    scratch = [
        pltpu.VMEM((rows, D_IN_PAD), F32), pltpu.VMEM((rows, ZC_WIDTH), F32),
        pltpu.VMEM((GATE_ROWS, rows), F32), pltpu.VMEM((GATE_ROWS, rows), F32),
        pltpu.VMEM((group, band_rows, A_WIDTH), BF16), pltpu.VMEM((group, band_rows, A_WIDTH), BF16),
        pltpu.VMEM((group, tile + SUBLANES, width2), F32),
        pltpu.VMEM((group, M_HEADS, M_DH, M_DH), F32), pltpu.VMEM((group, M_HEADS, M_DH), F32),
        pltpu.VMEM((group, M_HEADS, LANES), F32),
        pltpu.VMEM((rows, d), F32), pltpu.VMEM((rows, d), BF16),
        pltpu.VMEM((A_HEADS, tile, band_rows), F32),
    ] + scratch_ffn
    y, ko, vo, c1, n1, m1, conv1 = pl.pallas_call(
        functools.partial(_mixer_kernel, tile=tile, group=group, chunk=chunk, band_rows=band_rows, pos0=pos0,
                          n_tiles=n_tiles, fused_ffn=fused_ffn, has_state=has_state,
                          ahead=2 if group == 1 else ATTENTION_AHEAD),
        grid=(n_steps,),
        in_specs=in_specs,
        out_specs=out_specs,
        out_shape=out_shape,
        scratch_shapes=scratch,
        compiler_params=pltpu.CompilerParams(dimension_semantics=("arbitrary",),
                                             vmem_limit_bytes=VMEM_LIMIT_BYTES),
        name="mixer_chunked" if chunked else "mixer_step",
    )(*args)
    new_state = (ko.reshape(nb, keep, A_HEADS, A_DH), vo.reshape(nb, keep, A_HEADS, A_DH), c1, n1,
                 m1[:, :, 0], conv1[:, SUBLANES - (CONV_W - 1):, :])
    return y.reshape(nb, frames, d), new_state


def kernel(x_prompt, x_sample, cache_attn_k, cache_attn_v, state_mlstm_C, state_mlstm_n, state_mlstm_m, state_mlstm_conv, norm_ffn1, w1_ffn1, w3_ffn1, w2_ffn1, norm_mix, w_in, conv_w, conv_b, gate_bias, rel_bias, norm_mlstm_out, w_out, norm_ffn2, w1_ffn2, w3_ffn2, w2_ffn2, norm_final):
    depth = norm_ffn1.shape[0]
    nbp, seq, d = x_prompt.shape
    nbs, dec, _ = x_sample.shape
    xp = x_prompt
    xs = x_sample.reshape(nbs * dec, d)
    prompt_tile = min(PROMPT_TILE, seq)
    new_p, new_s = [], []
    for l in range(depth):
        last = l == depth - 1
        ffn1 = _ffn_weights(w1_ffn1[l], w3_ffn1[l], w2_ffn1[l])
        ffn2 = _ffn_weights(w1_ffn2[l], w3_ffn2[l], w2_ffn2[l])
        mix = (norm_mix[l], w_in[l], conv_w[l], conv_b[l], gate_bias[l], rel_bias[l], norm_mlstm_out[l], w_out[l])
        gf = norm_final if last else None

        xp, st = _mixer(xp, mix, None, tile=prompt_tile, chunk=prompt_tile, pos0=0, chunked=True,
                        ffn=(norm_ffn1[l],) + ffn1)
        new_p.append(st)
        xp = _ffn(xp.reshape(nbp * seq, d), norm_ffn2[l], ffn2, gf).reshape(nbp, seq, d)

        xs = _ffn(xs, norm_ffn1[l], ffn1)
        cache = (cache_attn_k[l].reshape(nbs, -1, A_DH), cache_attn_v[l].reshape(nbs, -1, A_DH),
                 state_mlstm_C[l], state_mlstm_n[l], state_mlstm_m[l], state_mlstm_conv[l])
        xs3, st = _mixer(xs.reshape(nbs, dec, d), mix, cache, tile=dec, chunk=dec, pos0=PAST_LEN, chunked=False)
        new_s.append(st)
        xs = _ffn(xs3.reshape(nbs * dec, d), norm_ffn2[l], ffn2, gf)

    stack = lambda states, i: jnp.stack([s[i] for s in states])
    return ((xp, xs.reshape(nbs, dec, d))
            + tuple(stack(new_p, i) for i in range(6)) + tuple(stack(new_s, i) for i in range(6)))
```

```python
import functools

import numpy as np
import jax
import jax.numpy as jnp
from jax import lax
from jax.experimental import pallas as pl
from jax.experimental.pallas import tpu as pltpu

F32 = jnp.float32
BF16 = jnp.bfloat16

CHUNK = 64
ATT_BAND = 8 * CHUNK
A_HEADS = 8
A_DH = 64
A_WIDTH = A_HEADS * A_DH
M_HEADS = 4
M_DH = 128
M_WIDTH = M_HEADS * M_DH
REL_CLIP = 128
CONV_W = 4
PAST_LEN = 4096
EPS = 1e-6
NEG = -1e30
LOG2E = 1.4426950408889634

LANES = 128
SUBLANES = 8
MXU_DIM = 256
VMEM_LIMIT_BYTES = 60 * 1024 * 1024

OFF_AK = A_WIDTH
OFF_AV = 2 * A_WIDTH
OFF_MQK = 3 * A_WIDTH
OFF_MV = OFF_MQK + 2 * M_WIDTH
OFF_MO = OFF_MV + M_WIDTH
OFF_MG = OFF_MO + M_WIDTH
N_GATES = 2 * M_HEADS
D_IN_PAD = OFF_MG + LANES
GATE_ROWS = 16
ZC_MV = A_WIDTH
ZC_MO = ZC_MV + M_WIDTH
ZC_MG = ZC_MO + M_WIDTH
ZC_WIDTH = ZC_MG + LANES

FFN_ROWS = 512
FFN_CHUNK = MXU_DIM
PROMPT_TILE = 256
STEP_ROWS = 128
ATTENTION_AHEAD = 4


def _rms(x, g):
    return x * lax.rsqrt(jnp.mean(x * x, axis=-1, keepdims=True) + EPS) * g


def _sigmoid(x):
    return 1.0 / (1.0 + jnp.exp(-x))


def _log_sigmoid(x):
    return jnp.minimum(x, 0.0) - jnp.log1p(jnp.exp(-jnp.abs(x)))


def _dot(a, b):
    return jnp.dot(a, b, preferred_element_type=F32)


def _dot_nt(a, b):
    return lax.dot_general(a, b, (((1,), (1,)), ((), ())), preferred_element_type=F32)


def _split3(x):
    p1 = x.astype(BF16)
    r1 = x - p1.astype(F32)
    p2 = r1.astype(BF16)
    p3 = (r1 - p2.astype(F32)).astype(BF16)
    return p1, p2, p3


def _resident(shape):
    nd = len(shape)
    return pl.BlockSpec(shape, lambda *_: (0,) * nd, pipeline_mode=pl.Buffered(1))


def _ffn_kernel(x_ref, g_ref, w1_ref, w3_ref, w2_ref, *rest, final_norm):
    if final_norm:
        gf_ref, o_ref, h_ref, u_ref = rest
    else:
        o_ref, h_ref, u_ref = rest
    h_ref[...] = _rms(x_ref[...], g_ref[...]).astype(BF16)
    for c in range(0, u_ref.shape[1], FFN_CHUNK):
        h = h_ref[...]
        a = _dot(h, w1_ref[:, c:c + FFN_CHUNK])
        b = _dot(h, w3_ref[:, c:c + FFN_CHUNK])
        u_ref[:, c:c + FFN_CHUNK] = (a * _sigmoid(a) * b).astype(BF16)
    y = x_ref[...] + 0.5 * _dot(u_ref[...], w2_ref[...])
    if final_norm:
        y = _rms(y, gf_ref[...])
    o_ref[...] = y


def _ffn_weights(w1, w3, w2):
    return w1.astype(BF16), w3.astype(BF16), w2.astype(BF16)


def _ffn(x2d, g, weights, gf=None):
    n, d = x2d.shape
    w1, w3, w2 = weights
    f = w1.shape[1]
    assert f % FFN_CHUNK == 0
    rows = min(FFN_ROWS, n)
    final_norm = gf is not None
    row_spec = pl.BlockSpec((rows, d), lambda i: (i, 0))
    in_specs = [row_spec, _resident((1, d)), _resident(w1.shape), _resident(w3.shape), _resident(w2.shape)]
    args = [x2d, g.reshape(1, d), w1, w3, w2]
    if final_norm:
        in_specs.append(_resident((1, d)))
        args.append(gf.reshape(1, d))
    return pl.pallas_call(
        functools.partial(_ffn_kernel, final_norm=final_norm),
        grid=(n // rows,),
        in_specs=in_specs,
        out_specs=row_spec,
        out_shape=jax.ShapeDtypeStruct((n, d), F32),
        scratch_shapes=[pltpu.VMEM((rows, d), BF16), pltpu.VMEM((rows, f), BF16)],
        compiler_params=pltpu.CompilerParams(dimension_semantics=("arbitrary",),
                                             vmem_limit_bytes=VMEM_LIMIT_BYTES),
        name="ffn_final" if final_norm else "ffn",
    )(*args)


def _mixer_kernel(*refs, tile, group, chunk, band_rows, pos0, n_tiles, fused_ffn, has_state, cache_t, ahead):
    refs = iter(refs)
    take = lambda n: [next(refs) for _ in range(n)]
    (xa_ref,) = take(1)
    if fused_ffn:
        g1_ref, w1_ref, w3_ref, w2_ref = take(4)
    else:
        (xb_ref,) = take(1)
    gmix_ref, win_ref, wgt_ref, convw_ref, convb_ref, gbrow_ref, gbcol_ref, diag_ref, mask_ref, normm_ref, wout_ref = take(11)
    if has_state:
        k0_ref, v0_ref, c0_ref, n0_ref, m0_ref, conv0_ref = take(6)
    y_ref, ko_ref, vo_ref, co_ref, no_ref, mo_ref, convo_ref = take(7)
    z_new, z_cur, g_new, g_cur, kband, vband, ubuf, c_s, n_s, m_s, mix_s, h_s, tab_s = take(13)
    if fused_ffn:
        u_s, h1_s, x1_new, x1_mid, x1_cur = take(5)
    if cache_t:
        kt_s, vt_s = take(2)
        assert has_state and n_tiles == 1 and pos0 >= ATT_BAND
    assert group == 1 or n_tiles == 1
    streams = range(group)
    span = lambda g: slice(g * tile, (g + 1) * tile)

    p = pl.program_id(0)
    lag = 2 if fused_ffn else 1
    t = lax.rem(jnp.maximum(p - lag, 0), n_tiles)

    @pl.when(p == 0)
    def _first_step():
        z_cur[...] = jnp.zeros_like(z_cur)
        g_cur[...] = jnp.zeros_like(g_cur)
        for g in streams:
            kband[g, ATT_BAND:band_rows, :] = jnp.zeros((band_rows - ATT_BAND, A_WIDTH), BF16)
            vband[g, ATT_BAND:band_rows, :] = jnp.zeros((band_rows - ATT_BAND, A_WIDTH), BF16)
            ubuf[g, SUBLANES:SUBLANES + tile, :] = jnp.zeros((tile, 2 * M_WIDTH), F32)
        if fused_ffn:
            x1_cur[...] = jnp.zeros_like(x1_cur)
            x1_mid[...] = jnp.zeros_like(x1_mid)
        width = diag_ref.shape[1]
        for head in range(A_HEADS):
            rows_of_diag = jnp.broadcast_to(diag_ref[head:head + 1, :], (tile, width))
            skew = pltpu.roll(rows_of_diag, width - (tile - 1), 1, stride=1, stride_axis=0)
            tab_s[head] = skew[:, :band_rows] + mask_ref[...]

    @pl.when(t == 0)
    def _load_state():
        if has_state:
            for g in streams:
                if cache_t:
                    kt_s[g] = k0_ref[g].astype(BF16)
                    vt_s[g] = v0_ref[g].astype(BF16)
                else:
                    kband[g, 0:ATT_BAND, :] = k0_ref[g].astype(BF16)
                    vband[g, 0:ATT_BAND, :] = v0_ref[g].astype(BF16)
                ubuf[g, 0:SUBLANES, :] = conv0_ref[g]
            c_s[...] = c0_ref[...]
            n_s[...] = n0_ref[...]
            m_s[...] = m0_ref[...]
        else:
            for g in streams:
                kband[g, 0:ATT_BAND, :] = jnp.zeros((ATT_BAND, A_WIDTH), BF16)
                vband[g, 0:ATT_BAND, :] = jnp.zeros((ATT_BAND, A_WIDTH), BF16)
                ubuf[g, 0:SUBLANES, :] = jnp.zeros((SUBLANES, 2 * M_WIDTH), F32)
            c_s[...] = jnp.zeros_like(c_s)
            n_s[...] = jnp.zeros_like(n_s)
            m_s[...] = jnp.zeros_like(m_s)

    def stage_a1():
        h1_s[...] = _rms(xa_ref[...], g1_ref[...]).astype(BF16)
        yield
        for c in range(0, u_s.shape[1], FFN_CHUNK):
            hf = h1_s[...]
            a = _dot(hf, w1_ref[:, c:c + FFN_CHUNK])
            b = _dot(hf, w3_ref[:, c:c + FFN_CHUNK])
            u_s[:, c:c + FFN_CHUNK] = (a * _sigmoid(a) * b).astype(BF16)
            yield
        for c in range(0, x1_new.shape[1], MXU_DIM):
            x1_new[:, c:c + MXU_DIM] = (xa_ref[:, c:c + MXU_DIM]
                                        + 0.5 * _dot(u_s[...], w2_ref[:, c:c + MXU_DIM]))
            yield

    def stage_a2():
        x_in = x1_mid[...] if fused_ffn else xa_ref[...]
        h_s[...] = _rms(x_in, gmix_ref[...]).astype(BF16)
        yield
        for c in range(0, D_IN_PAD, MXU_DIM):
            c1 = min(c + MXU_DIM, D_IN_PAD)
            z_new[:, c:c1] = _dot(h_s[...], win_ref[:, c:c1])
            projected_cols[0] = c1
            yield
        g_new[...] = _dot_nt(wgt_ref[...], h_s[...])

    projected_cols = [0]

    def alternate(*gens):
        gens = list(gens)
        while gens:
            for gen in list(gens):
                try:
                    next(gen)
                    yield
                except StopIteration:
                    gens.remove(gen)

    units = alternate(stage_a1(), stage_a2()) if fused_ffn else stage_a2()

    def project_next(count=1):
        for _ in range(count):
            next(units, None)

    lane = lax.broadcasted_iota(jnp.int32, (1, LANES), 1)
    even = lane < A_DH
    if pos0 < ATT_BAND:
        col = lax.broadcasted_iota(jnp.int32, (1, band_rows), 1)
        in_stream = col >= (ATT_BAND - pos0) - t * tile

    def scores(g, head):
        lo = (head // 2) * LANES
        mine = even if head % 2 == 0 else jnp.logical_not(even)
        qh = jnp.where(mine, z_cur[span(g), lo:lo + LANES] * (LOG2E * A_DH ** -0.5), 0.0).astype(BF16)
        if cache_t:
            s_old = _dot(qh, kt_s[g, lo:lo + LANES, :]) + tab_s[head, :, 0:ATT_BAND]
            s_new = _dot_nt(qh, kband[g, ATT_BAND:band_rows, lo:lo + LANES]) + tab_s[head, :, ATT_BAND:band_rows]
            return s_old, s_new
        s = _dot_nt(qh, kband[g, :, lo:lo + LANES]) + tab_s[head]
        if pos0 < ATT_BAND:
            s = jnp.where(in_stream, s, NEG)
        return s

    def attend(g, head, s):
        lo = (head // 2) * LANES
        mine = even if head % 2 == 0 else jnp.logical_not(even)
        if cache_t:
            s_old, s_new = s
            top = jnp.maximum(jnp.max(s_old, axis=-1, keepdims=True), jnp.max(s_new, axis=-1, keepdims=True))
            e_old = jnp.exp2(s_old - top)
            e_new = jnp.exp2(s_new - top)
            o = (_dot_nt(e_old.astype(BF16), vt_s[g, lo:lo + LANES, :])
                 + _dot(e_new.astype(BF16), vband[g, ATT_BAND:band_rows, lo:lo + LANES]))
            total = jnp.sum(e_old, axis=-1, keepdims=True) + jnp.sum(e_new, axis=-1, keepdims=True)
            o = jnp.where(mine, o, 0.0) * (1.0 / total)
        else:
            vp = vband[g, :, lo:lo + LANES]
            e = jnp.exp2(s - jnp.max(s, axis=-1, keepdims=True))
            o = _dot(e.astype(BF16), jnp.where(mine, vp, jnp.zeros_like(vp)))
            o = o * (1.0 / jnp.sum(e, axis=-1, keepdims=True))
        if head % 2 == 0:
            mix_s[span(g), lo:lo + LANES] = o
        else:
            mix_s[span(g), lo:lo + LANES] += o

    pending = []
    for g in streams:
        for head in range(A_HEADS):
            pending.append((g, head, scores(g, head)))
            project_next(lag)
            if len(pending) > ahead:
                attend(*pending.pop(0))
    for item in pending:
        attend(*item)
    assert projected_cols[0] >= OFF_MQK
    ko_ref[...] = z_new[:, OFF_AK:OFF_AV]
    vo_ref[...] = z_new[:, OFF_AV:OFF_MQK]
    for g in streams:
        if n_tiles > 1:
            kband[g, 0:ATT_BAND, :] = kband[g, tile:tile + ATT_BAND, :]
            vband[g, 0:ATT_BAND, :] = vband[g, tile:tile + ATT_BAND, :]
        kband[g, ATT_BAND:ATT_BAND + tile, :] = z_new[span(g), OFF_AK:OFF_AV].astype(BF16)
        vband[g, ATT_BAND:ATT_BAND + tile, :] = z_new[span(g), OFF_AV:OFF_MQK].astype(BF16)
    z_cur[:, 0:OFF_AK] = z_new[:, 0:OFF_AK]

    project_next(2)
    mq, mk, last_rows = [], [], []
    for g in streams:
        qk = convb_ref[...]
        for j in range(CONV_W):
            start = SUBLANES - (CONV_W - 1) + j
            qk = qk + ubuf[g, start:start + tile, :] * convw_ref[j:j + 1, :]
        qk = qk * _sigmoid(qk)
        mq.append(qk[:, :M_WIDTH])
        mk.append(qk[:, M_WIDTH:] * (M_DH ** -0.5))
        last_rows.append(ubuf[g, tile:tile + SUBLANES, :])
        convo_ref[g] = last_rows[g]

    project_next(2)
    rows = group * tile
    g_col = z_cur[:, ZC_MG:ZC_MG + LANES] + gbrow_ref[...]
    g_row = g_cur[...] + gbcol_ref[...]
    lf_col = _log_sigmoid(g_col)
    lf_row = _log_sigmoid(g_row)
    ri = lax.broadcasted_iota(jnp.int32, (rows, rows), 0)
    ci = lax.broadcasted_iota(jnp.int32, (rows, rows), 1)
    assert chunk & (chunk - 1) == 0
    chunk_shift = chunk.bit_length() - 1
    same_chunk = (ri >> chunk_shift) == (ci >> chunk_shift)
    tri = jnp.where(jnp.logical_and(same_chunk, ci <= ri), 1.0, 0.0).astype(BF16)
    tri_t = jnp.where(jnp.logical_and(same_chunk, ri <= ci), 1.0, 0.0).astype(BF16)
    b_col = sum(_dot(tri, part) for part in _split3(lf_col))
    b_row = sum(_dot(part, tri_t) for part in _split3(lf_row))

    causal = (lax.broadcasted_iota(jnp.int32, (chunk, chunk), 1)
              <= lax.broadcasted_iota(jnp.int32, (chunk, chunk), 0))
    assert chunk == tile
    first = {}

    def block_scores(g, hd):
        lo, hi = hd * M_DH, (hd + 1) * M_DH
        cmat = c_s[g, hd]
        m_prev = m_s[g, hd:hd + 1, 0:1]
        bc = b_col[span(g), M_HEADS + hd:M_HEADS + hd + 1]
        br = b_row[M_HEADS + hd:M_HEADS + hd + 1, span(g)]
        igr = g_row[hd:hd + 1, span(g)]
        dmat = jnp.where(causal, bc + (igr - br), NEG)
        inter = bc + m_prev
        mt = jnp.maximum(inter, jnp.max(dmat, axis=-1, keepdims=True))
        q = mq[g][:, lo:hi]
        k = mk[g][:, lo:hi]
        vb = z_cur[span(g), ZC_MV + lo:ZC_MV + hi].astype(BF16)
        qb = q.astype(BF16)
        first[g, hd] = dict(bc=bc, mt=mt, inter=inter, m_prev=m_prev, cmat=cmat, q=q, k=k, vb=vb, dmat=dmat,
                            qk=_dot_nt(qb, k.astype(BF16)), qc=_dot(qb, cmat.astype(BF16)))

    def block_output(g, hd):
        lo, hi = hd * M_DH, (hd + 1) * M_DH
        f = first[g, hd]
        nrow = n_s[g, hd:hd + 1, :]
        s = f["qk"] * jnp.exp(f["dmat"] - f["mt"])
        iw = jnp.exp(f["inter"] - f["mt"])
        num = iw * f["qc"] + _dot(s.astype(BF16), f["vb"])
        den = iw * jnp.sum(f["q"] * nrow, axis=-1, keepdims=True) + jnp.sum(s, axis=-1, keepdims=True)
        mh = num / jnp.maximum(jnp.abs(den), jnp.exp(-f["mt"]))
        mh = mh * _sigmoid(z_cur[span(g), ZC_MO + lo:ZC_MO + hi])
        mix_s[span(g), A_WIDTH + lo:A_WIDTH + hi] = _rms(mh, normm_ref[:, lo:hi])

    def block_state(g, hd):
        f = first[g, hd]
        bc, mt, m_prev = f["bc"], f["mt"], f["m_prev"]
        igc = g_col[span(g), hd:hd + 1]
        b_last = bc[tile - 1:tile, :]
        m_new = mt[tile - 1:tile, :]
        kw = f["k"] * jnp.exp(b_last - bc + igc - m_new)
        decay = jnp.exp(b_last + m_prev - m_new)
        c_s[g, hd] = decay * f["cmat"] + _dot(kw.T.astype(BF16), f["vb"])
        n_s[g, hd:hd + 1, :] = decay * n_s[g, hd:hd + 1, :] + jnp.sum(kw, axis=0, keepdims=True)
        m_s[g, hd:hd + 1, :] = jnp.broadcast_to(m_new, (1, LANES))

    blocks = [(g, hd) for hd in range(M_HEADS) for g in streams]
    if group == 1:
        for blk in blocks:
            project_next(3)
            block_scores(*blk)
            block_output(*blk)
            block_state(*blk)
    else:
        for phase in (block_scores, block_output, block_state):
            for blk in blocks:
                project_next()
                phase(*blk)

    for _ in units:
        pass
    co_ref[...] = c_s[...]
    no_ref[...] = n_s[...]
    mo_ref[...] = m_s[...]
    resid = x1_cur[...] if fused_ffn else xb_ref[...]
    y_ref[...] = resid + _dot(mix_s[...].astype(BF16), wout_ref[...])

    if fused_ffn:
        x1_cur[...] = x1_mid[...]
        x1_mid[...] = x1_new[...]
    for g in streams:
        ubuf[g, 0:SUBLANES, :] = last_rows[g]
        ubuf[g, SUBLANES:SUBLANES + tile, :] = z_new[span(g), OFF_MQK:OFF_MV]
    z_cur[:, ZC_MV:ZC_WIDTH] = z_new[:, OFF_MV:D_IN_PAD]
    g_cur[...] = g_new[...]


def _bias_pieces(rel_bias, tile, band_rows, chunked):
    heads = rel_bias.shape[0]
    span = band_rows + tile - 1
    width = -(-span // LANES) * LANES
    n_far = ATT_BAND + tile - REL_CLIP
    n_near = width - n_far - (2 * REL_CLIP - 1)
    diag = jnp.concatenate([jnp.broadcast_to(rel_bias[:, 2 * REL_CLIP:], (heads, n_far)),
                            rel_bias[:, 2 * REL_CLIP - 1:0:-1],
                            jnp.broadcast_to(rel_bias[:, :1], (heads, n_near))], axis=1)
    i = np.arange(tile)[:, None]
    j = np.arange(band_rows)[None, :]
    visible = j < ATT_BAND + tile
    if chunked:
        qc = i // CHUNK
        kc = (j - ATT_BAND) // CHUNK
        visible = visible & (kc <= qc) & (kc >= qc - ATT_BAND // CHUNK)
    mask = jnp.asarray(np.where(visible, 0.0, NEG * LOG2E), F32)
    return diag * LOG2E, mask


def _mixer(x, params, state, *, tile, chunk, pos0, chunked, ffn=None, cache_t=False):
    nb, frames, d = x.shape
    gmix, w_in, conv_w, conv_b, gate_bias, rel_bias, norm_m, w_out = params
    n_tiles = frames // tile
    band_rows = ATT_BAND + -(-tile // LANES) * LANES
    keep_tiles = min(ATT_BAND, frames) // tile
    width2 = 2 * M_WIDTH
    fused_ffn = ffn is not None
    has_state = state is not None
    group = max(1, min(nb, STEP_ROWS // tile)) if n_tiles == 1 else 1
    assert nb % group == 0
    rows = group * tile

    win = jnp.concatenate([w_in, jnp.zeros((d, D_IN_PAD - w_in.shape[1]), F32)], axis=1).astype(BF16)
    wgt = jnp.zeros((GATE_ROWS, d), F32).at[:N_GATES].set(w_in[:, OFF_MG:].T).astype(BF16)
    gb_row = jnp.zeros((1, LANES), F32).at[0, :N_GATES].set(gate_bias)
    gb_col = jnp.zeros((GATE_ROWS, 1), F32).at[:N_GATES, 0].set(gate_bias)
    diag, mask = _bias_pieces(rel_bias, tile, band_rows, chunked)

    lag = 2 if fused_ffn else 1
    n_tiles_all = (nb // group) * n_tiles
    n_steps = n_tiles_all + lag
    entering = lambda p: jnp.minimum(p, n_tiles_all - 1)
    projected = lambda p: jnp.clip(p - (lag - 1), 0, n_tiles_all - 1)
    mixed = lambda p: jnp.maximum(p - lag, 0)
    per_group = lambda *dims: pl.BlockSpec((group,) + dims,
                                           lambda p: (lax.div(mixed(p), n_tiles),) + (0,) * len(dims))
    enter_spec = pl.BlockSpec((rows, d), lambda p: (entering(p), 0))
    mix_spec = pl.BlockSpec((rows, d), lambda p: (mixed(p), 0))

    def kv_index(p):
        q = projected(p)
        return (lax.div(q, n_tiles) * keep_tiles + jnp.maximum(lax.rem(q, n_tiles) - (n_tiles - keep_tiles), 0), 0)

    kv_spec = pl.BlockSpec((rows, A_WIDTH), kv_index)

    x2d = x.reshape(nb * frames, d)
    args, in_specs = [x2d], [enter_spec]
    scratch_ffn = []
    if fused_ffn:
        g1, w1, w3, w2 = ffn
        args += [g1.reshape(1, d), w1, w3, w2]
        in_specs += [_resident((1, d)), _resident(w1.shape), _resident(w3.shape), _resident(w2.shape)]
        scratch_ffn = [pltpu.VMEM((rows, w1.shape[1]), BF16), pltpu.VMEM((rows, d), BF16)] + [pltpu.VMEM((rows, d), F32)] * 3
    else:
        args.append(x2d)
        in_specs.append(mix_spec)
    args += [gmix.reshape(1, d), win, wgt, conv_w, conv_b.reshape(1, width2), gb_row, gb_col, diag, mask,
             norm_m.reshape(1, M_WIDTH), w_out.astype(BF16)]
    in_specs += [_resident((1, d)), _resident(win.shape), _resident(wgt.shape),
                 _resident((CONV_W, width2)), _resident((1, width2)), _resident((1, LANES)),
                 _resident((GATE_ROWS, 1)), _resident(diag.shape), _resident(mask.shape), _resident((1, M_WIDTH)),
                 _resident((d, d))]
    if has_state:
        k0, v0, c0, n0, m0, conv0 = state
        conv0p = jnp.concatenate([jnp.zeros((nb, SUBLANES - (CONV_W - 1), width2), F32), conv0], axis=1)
        m0p = jnp.broadcast_to(m0[:, :, None], (nb, M_HEADS, LANES))
        args += [k0, v0, c0, n0, m0p, conv0p]
        in_specs += [per_group(ATT_BAND, A_WIDTH), per_group(ATT_BAND, A_WIDTH),
                     per_group(M_HEADS, M_DH, M_DH), per_group(M_HEADS, M_DH), per_group(M_HEADS, LANES),
                     per_group(SUBLANES, width2)]
    out_specs = [
        mix_spec, kv_spec, kv_spec, per_group(M_HEADS, M_DH, M_DH), per_group(M_HEADS, M_DH),
        per_group(M_HEADS, LANES), per_group(SUBLANES, width2),
    ]
    keep = keep_tiles * tile
    out_shape = [
        jax.ShapeDtypeStruct((nb * frames, d), F32),
        jax.ShapeDtypeStruct((nb * keep, A_WIDTH), F32),
        jax.ShapeDtypeStruct((nb * keep, A_WIDTH), F32),
        jax.ShapeDtypeStruct((nb, M_HEADS, M_DH, M_DH), F32),
        jax.ShapeDtypeStruct((nb, M_HEADS, M_DH), F32),
        jax.ShapeDtypeStruct((nb, M_HEADS, LANES), F32),
        jax.ShapeDtypeStruct((nb, SUBLANES, width2), F32),
    ]
    scratch = [
        pltpu.VMEM((rows, D_IN_PAD), F32), pltpu.VMEM((rows, ZC_WIDTH), F32),
        pltpu.VMEM((GATE_ROWS, rows), F32), pltpu.VMEM((GATE_ROWS, rows), F32),
        pltpu.VMEM((group, band_rows, A_WIDTH), BF16), pltpu.VMEM((group, band_rows, A_WIDTH), BF16),
        pltpu.VMEM((group, tile + SUBLANES, width2), F32),
        pltpu.VMEM((group, M_HEADS, M_DH, M_DH), F32), pltpu.VMEM((group, M_HEADS, M_DH), F32),
        pltpu.VMEM((group, M_HEADS, LANES), F32),
        pltpu.VMEM((rows, d), F32), pltpu.VMEM((rows, d), BF16),
        pltpu.VMEM((A_HEADS, tile, band_rows), F32),
    ] + scratch_ffn
    if cache_t:
        scratch += [pltpu.VMEM((group, A_WIDTH, ATT_BAND), BF16)] * 2
    y, ko, vo, c1, n1, m1, conv1 = pl.pallas_call(
        functools.partial(_mixer_kernel, tile=tile, group=group, chunk=chunk, band_rows=band_rows, pos0=pos0,
                          n_tiles=n_tiles, fused_ffn=fused_ffn, has_state=has_state, cache_t=cache_t,
                          ahead=2 if group == 1 else ATTENTION_AHEAD),
        grid=(n_steps,),
        in_specs=in_specs,
        out_specs=out_specs,
        out_shape=out_shape,
        scratch_shapes=scratch,
        compiler_params=pltpu.CompilerParams(dimension_semantics=("arbitrary",),
                                             vmem_limit_bytes=VMEM_LIMIT_BYTES),
        name="mixer_chunked" if chunked else "mixer_step",
    )(*args)
    new_state = (ko.reshape(nb, keep, A_HEADS, A_DH), vo.reshape(nb, keep, A_HEADS, A_DH), c1, n1,
                 m1[:, :, 0], conv1[:, SUBLANES - (CONV_W - 1):, :])
    return y.reshape(nb, frames, d), new_state


def kernel(x_prompt, x_sample, cache_attn_k, cache_attn_v, state_mlstm_C, state_mlstm_n, state_mlstm_m, state_mlstm_conv, norm_ffn1, w1_ffn1, w3_ffn1, w2_ffn1, norm_mix, w_in, conv_w, conv_b, gate_bias, rel_bias, norm_mlstm_out, w_out, norm_ffn2, w1_ffn2, w3_ffn2, w2_ffn2, norm_final):
    depth = norm_ffn1.shape[0]
    nbp, seq, d = x_prompt.shape
    nbs, dec, _ = x_sample.shape
    xp = x_prompt
    xs = x_sample.reshape(nbs * dec, d)
    prompt_tile = min(PROMPT_TILE, seq)
    new_p, new_s = [], []
    for l in range(depth):
        last = l == depth - 1
        ffn1 = _ffn_weights(w1_ffn1[l], w3_ffn1[l], w2_ffn1[l])
        ffn2 = _ffn_weights(w1_ffn2[l], w3_ffn2[l], w2_ffn2[l])
        mix = (norm_mix[l], w_in[l], conv_w[l], conv_b[l], gate_bias[l], rel_bias[l], norm_mlstm_out[l], w_out[l])
        gf = norm_final if last else None

        xp, st = _mixer(xp, mix, None, tile=prompt_tile, chunk=prompt_tile, pos0=0, chunked=True,
                        ffn=(norm_ffn1[l],) + ffn1)
        new_p.append(st)
        xp = _ffn(xp.reshape(nbp * seq, d), norm_ffn2[l], ffn2, gf).reshape(nbp, seq, d)

        xs = _ffn(xs, norm_ffn1[l], ffn1)
        feature_major = lambda c: c.transpose(0, 2, 3, 1).reshape(nbs, A_WIDTH, -1)
        cache = (feature_major(cache_attn_k[l]), feature_major(cache_attn_v[l]),
                 state_mlstm_C[l], state_mlstm_n[l], state_mlstm_m[l], state_mlstm_conv[l])
        xs3, st = _mixer(xs.reshape(nbs, dec, d), mix, cache, tile=dec, chunk=dec, pos0=PAST_LEN, chunked=False,
                         cache_t=True)
        new_s.append(st)
        xs = _ffn(xs3.reshape(nbs * dec, d), norm_ffn2[l], ffn2, gf)

    stack = lambda states, i: jnp.stack([s[i] for s in states])
    return ((xp, xs.reshape(nbs, dec, d))
            + tuple(stack(new_p, i) for i in range(6)) + tuple(stack(new_s, i) for i in range(6)))
```

```python
import functools

import numpy as np
import jax
import jax.numpy as jnp
from jax import lax
from jax.experimental import pallas as pl
from jax.experimental.pallas import tpu as pltpu

F32 = jnp.float32
BF16 = jnp.bfloat16

CHUNK = 64
ATT_BAND = 8 * CHUNK
A_HEADS = 8
A_DH = 64
A_WIDTH = A_HEADS * A_DH
M_HEADS = 4
M_DH = 128
M_WIDTH = M_HEADS * M_DH
REL_CLIP = 128
CONV_W = 4
PAST_LEN = 4096
EPS = 1e-6
NEG = -1e30
LOG2E = 1.4426950408889634

LANES = 128
SUBLANES = 8
MXU_DIM = 256
VMEM_LIMIT_BYTES = 60 * 1024 * 1024

OFF_AK = A_WIDTH
OFF_AV = 2 * A_WIDTH
OFF_MQK = 3 * A_WIDTH
OFF_MV = OFF_MQK + 2 * M_WIDTH
OFF_MO = OFF_MV + M_WIDTH
OFF_MG = OFF_MO + M_WIDTH
N_GATES = 2 * M_HEADS
D_IN_PAD = OFF_MG + LANES
GATE_ROWS = 16
ZC_MV = A_WIDTH
ZC_MO = ZC_MV + M_WIDTH
ZC_MG = ZC_MO + M_WIDTH
ZC_WIDTH = ZC_MG + LANES
NM_ROWS = 32

FFN_ROWS = 512
FFN_CHUNK = MXU_DIM
PROMPT_TILE = 256
STEP_ROWS = 128
ATTENTION_AHEAD = 4
PROMPT_AHEAD = 2
UNITS_BEFORE_CONV = 2
UNITS_BEFORE_GATES = 2
UNITS_PER_MLSTM_BLOCK = 3


def _rms(x, g):
    return x * lax.rsqrt(jnp.mean(x * x, axis=-1, keepdims=True) + EPS) * g


def _sigmoid(x):
    return 1.0 / (1.0 + jnp.exp(-x))


def _log_sigmoid(x):
    return jnp.minimum(x, 0.0) - jnp.log1p(jnp.exp(-jnp.abs(x)))


def _dot(a, b):
    return jnp.dot(a, b, preferred_element_type=F32)


def _dot_nt(a, b):
    return lax.dot_general(a, b, (((1,), (1,)), ((), ())), preferred_element_type=F32)


def _split3(x):
    p1 = x.astype(BF16)
    r1 = x - p1.astype(F32)
    p2 = r1.astype(BF16)
    p3 = (r1 - p2.astype(F32)).astype(BF16)
    return p1, p2, p3


def _resident(shape):
    nd = len(shape)
    return pl.BlockSpec(shape, lambda *_: (0,) * nd, pipeline_mode=pl.Buffered(1))


def _ffn_kernel(x_ref, g_ref, w1_ref, w3_ref, w2_ref, *rest, final_norm):
    if final_norm:
        gf_ref, o_ref, h_ref, u_ref = rest
    else:
        o_ref, h_ref, u_ref = rest
    h_ref[...] = _rms(x_ref[...], g_ref[...]).astype(BF16)
    for c in range(0, u_ref.shape[1], FFN_CHUNK):
        h = h_ref[...]
        a = _dot(h, w1_ref[:, c:c + FFN_CHUNK])
        b = _dot(h, w3_ref[:, c:c + FFN_CHUNK])
        u_ref[:, c:c + FFN_CHUNK] = (a * _sigmoid(a) * b).astype(BF16)
    y = x_ref[...] + 0.5 * _dot(u_ref[...], w2_ref[...])
    if final_norm:
        y = _rms(y, gf_ref[...])
    o_ref[...] = y


def _ffn_weights(w1, w3, w2):
    return w1.astype(BF16), w3.astype(BF16), w2.astype(BF16)


def _ffn(x2d, g, weights, gf=None):
    n, d = x2d.shape
    w1, w3, w2 = weights
    f = w1.shape[1]
    assert f % FFN_CHUNK == 0
    rows = min(FFN_ROWS, n)
    final_norm = gf is not None
    row_spec = pl.BlockSpec((rows, d), lambda i: (i, 0))
    in_specs = [row_spec, _resident((1, d)), _resident(w1.shape), _resident(w3.shape), _resident(w2.shape)]
    args = [x2d, g.reshape(1, d), w1, w3, w2]
    if final_norm:
        in_specs.append(_resident((1, d)))
        args.append(gf.reshape(1, d))
    return pl.pallas_call(
        functools.partial(_ffn_kernel, final_norm=final_norm),
        grid=(n // rows,),
        in_specs=in_specs,
        out_specs=row_spec,
        out_shape=jax.ShapeDtypeStruct((n, d), F32),
        scratch_shapes=[pltpu.VMEM((rows, d), BF16), pltpu.VMEM((rows, f), BF16)],
        compiler_params=pltpu.CompilerParams(dimension_semantics=("arbitrary",),
                                             vmem_limit_bytes=VMEM_LIMIT_BYTES),
        name="ffn_final" if final_norm else "ffn",
    )(*args)


def _mixer_kernel(*refs, tile, group, chunk, band_rows, pos0, n_tiles, fused_ffn, has_state, cache_t, ahead):
    refs = iter(refs)
    take = lambda n: [next(refs) for _ in range(n)]
    (xa_ref,) = take(1)
    if fused_ffn:
        g1_ref, w1_ref, w3_ref, w2_ref = take(4)
    else:
        (xb_ref,) = take(1)
    gmix_ref, win_ref, wgt_ref, convw_ref, convb_ref, gbrow_ref, gbcol_ref, diag_ref, mask_ref, normm_ref, wout_ref = take(11)
    if has_state:
        k0_ref, v0_ref, c0_ref, n0_ref, m0_ref, conv0_ref = take(6)
    y_ref, ko_ref, vo_ref, co_ref, no_ref, mo_ref, convo_ref = take(7)
    z_new, z_cur, g_new, g_cur, kband, vband, ubuf, c_s, nm_s, mix_s, h_s, tab_s = take(12)
    n_rows, m_rows = slice(0, M_HEADS), slice(SUBLANES, SUBLANES + M_HEADS)
    if fused_ffn:
        u_s, h1_s, x1_new, x1_mid, x1_cur = take(5)
    if cache_t:
        kt_s, vt_s = take(2)
        assert has_state and n_tiles == 1 and pos0 >= ATT_BAND
    assert group == 1 or n_tiles == 1
    streams = range(group)
    span = lambda g: slice(g * tile, (g + 1) * tile)

    p = pl.program_id(0)
    lag = 2 if fused_ffn else 1
    t = lax.rem(jnp.maximum(p - lag, 0), n_tiles)

    @pl.when(p == 0)
    def _first_step():
        z_cur[...] = jnp.zeros_like(z_cur)
        g_cur[...] = jnp.zeros_like(g_cur)
        for g in streams:
            kband[g, ATT_BAND:band_rows, :] = jnp.zeros((band_rows - ATT_BAND, A_WIDTH), BF16)
            vband[g, ATT_BAND:band_rows, :] = jnp.zeros((band_rows - ATT_BAND, A_WIDTH), BF16)
            ubuf[g, SUBLANES:SUBLANES + tile, :] = jnp.zeros((tile, 2 * M_WIDTH), F32)
        if fused_ffn:
            x1_cur[...] = jnp.zeros_like(x1_cur)
            x1_mid[...] = jnp.zeros_like(x1_mid)
        width = diag_ref.shape[1]
        for head in range(A_HEADS):
            rows_of_diag = jnp.broadcast_to(diag_ref[head:head + 1, :], (tile, width))
            skew = pltpu.roll(rows_of_diag, width - (tile - 1), 1, stride=1, stride_axis=0)
            tab_s[head] = skew[:, :band_rows] + mask_ref[...]

    @pl.when(t == 0)
    def _load_state():
        if has_state:
            for g in streams:
                if cache_t:
                    kt_s[g] = k0_ref[g].astype(BF16)
                    vt_s[g] = v0_ref[g].astype(BF16)
                else:
                    kband[g, 0:ATT_BAND, :] = k0_ref[g].astype(BF16)
                    vband[g, 0:ATT_BAND, :] = v0_ref[g].astype(BF16)
                ubuf[g, 0:SUBLANES, :] = conv0_ref[g]
            c_s[...] = c0_ref[...]
            nm_s[:, n_rows, :] = n0_ref[...]
            nm_s[:, m_rows, :] = m0_ref[...]
        else:
            for g in streams:
                kband[g, 0:ATT_BAND, :] = jnp.zeros((ATT_BAND, A_WIDTH), BF16)
                vband[g, 0:ATT_BAND, :] = jnp.zeros((ATT_BAND, A_WIDTH), BF16)
                ubuf[g, 0:SUBLANES, :] = jnp.zeros((SUBLANES, 2 * M_WIDTH), F32)
            c_s[...] = jnp.zeros_like(c_s)
            nm_s[...] = jnp.zeros_like(nm_s)

    def stage_a1():
        h1_s[...] = _rms(xa_ref[...], g1_ref[...]).astype(BF16)
        yield
        for c in range(0, u_s.shape[1], FFN_CHUNK):
            hf = h1_s[...]
            a = _dot(hf, w1_ref[:, c:c + FFN_CHUNK])
            b = _dot(hf, w3_ref[:, c:c + FFN_CHUNK])
            u_s[:, c:c + FFN_CHUNK] = (a * _sigmoid(a) * b).astype(BF16)
            yield
        for c in range(0, x1_new.shape[1], MXU_DIM):
            x1_new[:, c:c + MXU_DIM] = (xa_ref[:, c:c + MXU_DIM]
                                        + 0.5 * _dot(u_s[...], w2_ref[:, c:c + MXU_DIM]))
            yield

    def stage_a2():
        x_in = x1_mid[...] if fused_ffn else xa_ref[...]
        h_s[...] = _rms(x_in, gmix_ref[...]).astype(BF16)
        yield
        for c in range(0, D_IN_PAD, MXU_DIM):
            c1 = min(c + MXU_DIM, D_IN_PAD)
            z_new[:, c:c1] = _dot(h_s[...], win_ref[:, c:c1])
            projected_cols[0] = c1
            yield
        g_new[...] = _dot_nt(wgt_ref[...], h_s[...])

    projected_cols = [0]

    def alternate(*gens):
        gens = list(gens)
        while gens:
            for gen in list(gens):
                try:
                    next(gen)
                    yield
                except StopIteration:
                    gens.remove(gen)

    units = alternate(stage_a1(), stage_a2()) if fused_ffn else stage_a2()

    def project_next(count=1):
        for _ in range(count):
            next(units, None)

    lane = lax.broadcasted_iota(jnp.int32, (1, LANES), 1)
    even = lane < A_DH
    if pos0 < ATT_BAND:
        col = lax.broadcasted_iota(jnp.int32, (1, band_rows), 1)
        in_stream = col >= (ATT_BAND - pos0) - t * tile

    def scores(g, head):
        lo = (head // 2) * LANES
        mine = even if head % 2 == 0 else jnp.logical_not(even)
        qh = jnp.where(mine, z_cur[span(g), lo:lo + LANES] * (LOG2E * A_DH ** -0.5), 0.0).astype(BF16)
        if cache_t:
            s_old = _dot(qh, kt_s[g, lo:lo + LANES, :]) + tab_s[head, :, 0:ATT_BAND]
            s_new = _dot_nt(qh, kband[g, ATT_BAND:band_rows, lo:lo + LANES]) + tab_s[head, :, ATT_BAND:band_rows]
            return s_old, s_new
        s = _dot_nt(qh, kband[g, :, lo:lo + LANES]) + tab_s[head]
        if pos0 < ATT_BAND:
            s = jnp.where(in_stream, s, NEG)
        return s

    def attend(g, head, s):
        lo = (head // 2) * LANES
        mine = even if head % 2 == 0 else jnp.logical_not(even)
        if cache_t:
            s_old, s_new = s
            top = jnp.maximum(jnp.max(s_old, axis=-1, keepdims=True), jnp.max(s_new, axis=-1, keepdims=True))
            e_old = jnp.exp2(s_old - top)
            e_new = jnp.exp2(s_new - top)
            o = (_dot_nt(e_old.astype(BF16), vt_s[g, lo:lo + LANES, :])
                 + _dot(e_new.astype(BF16), vband[g, ATT_BAND:band_rows, lo:lo + LANES]))
            total = jnp.sum(e_old, axis=-1, keepdims=True) + jnp.sum(e_new, axis=-1, keepdims=True)
            o = jnp.where(mine, o, 0.0) * (1.0 / total)
        else:
            vp = vband[g, :, lo:lo + LANES]
            e = jnp.exp2(s - jnp.max(s, axis=-1, keepdims=True))
            o = _dot(e.astype(BF16), jnp.where(mine, vp, jnp.zeros_like(vp)))
            o = o * (1.0 / jnp.sum(e, axis=-1, keepdims=True))
        if head % 2 == 0:
            mix_s[span(g), lo:lo + LANES] = o
        else:
            mix_s[span(g), lo:lo + LANES] += o

    pending = []
    for g in streams:
        for head in range(A_HEADS):
            pending.append((g, head, scores(g, head)))
            project_next(lag)
            if len(pending) > ahead:
                attend(*pending.pop(0))
    for item in pending:
        attend(*item)
    assert projected_cols[0] >= OFF_MQK
    ko_ref[...] = z_new[:, OFF_AK:OFF_AV]
    vo_ref[...] = z_new[:, OFF_AV:OFF_MQK]
    for g in streams:
        if n_tiles > 1:
            kband[g, 0:ATT_BAND, :] = kband[g, tile:tile + ATT_BAND, :]
            vband[g, 0:ATT_BAND, :] = vband[g, tile:tile + ATT_BAND, :]
        kband[g, ATT_BAND:ATT_BAND + tile, :] = z_new[span(g), OFF_AK:OFF_AV].astype(BF16)
        vband[g, ATT_BAND:ATT_BAND + tile, :] = z_new[span(g), OFF_AV:OFF_MQK].astype(BF16)
    z_cur[:, 0:OFF_AK] = z_new[:, 0:OFF_AK]

    project_next(UNITS_BEFORE_CONV)
    mq, mk, last_rows = [], [], []
    for g in streams:
        qk = convb_ref[...]
        for j in range(CONV_W):
            start = SUBLANES - (CONV_W - 1) + j
            qk = qk + ubuf[g, start:start + tile, :] * convw_ref[j:j + 1, :]
        qk = qk * _sigmoid(qk)
        mq.append(qk[:, :M_WIDTH])
        mk.append(qk[:, M_WIDTH:] * (M_DH ** -0.5))
        last_rows.append(ubuf[g, tile:tile + SUBLANES, :])
        convo_ref[g] = last_rows[g]

    project_next(UNITS_BEFORE_GATES)
    rows = group * tile
    g_col = z_cur[:, ZC_MG:ZC_MG + LANES] + gbrow_ref[...]
    g_row = g_cur[...] + gbcol_ref[...]
    lf_col = _log_sigmoid(g_col)
    lf_row = _log_sigmoid(g_row)
    ri = lax.broadcasted_iota(jnp.int32, (rows, rows), 0)
    ci = lax.broadcasted_iota(jnp.int32, (rows, rows), 1)
    assert chunk & (chunk - 1) == 0
    chunk_shift = chunk.bit_length() - 1
    same_chunk = (ri >> chunk_shift) == (ci >> chunk_shift)
    tri = jnp.where(jnp.logical_and(same_chunk, ci <= ri), 1.0, 0.0).astype(BF16)
    tri_t = jnp.where(jnp.logical_and(same_chunk, ri <= ci), 1.0, 0.0).astype(BF16)
    b_col = sum(_dot(tri, part) for part in _split3(lf_col))
    b_row = sum(_dot(part, tri_t) for part in _split3(lf_row))

    causal = (lax.broadcasted_iota(jnp.int32, (chunk, chunk), 1)
              <= lax.broadcasted_iota(jnp.int32, (chunk, chunk), 0))
    assert chunk == tile
    first = {}

    def block_scores(g, hd):
        lo, hi = hd * M_DH, (hd + 1) * M_DH
        cmat = c_s[g, hd]
        m_prev = nm_s[g, SUBLANES + hd:SUBLANES + hd + 1, 0:1]
        bc = b_col[span(g), M_HEADS + hd:M_HEADS + hd + 1]
        br = b_row[M_HEADS + hd:M_HEADS + hd + 1, span(g)]
        igr = g_row[hd:hd + 1, span(g)]
        dmat = jnp.where(causal, bc + (igr - br), NEG)
        inter = bc + m_prev
        mt = jnp.maximum(inter, jnp.max(dmat, axis=-1, keepdims=True))
        q = mq[g][:, lo:hi]
        k = mk[g][:, lo:hi]
        vb = z_cur[span(g), ZC_MV + lo:ZC_MV + hi].astype(BF16)
        qb = q.astype(BF16)
        first[g, hd] = dict(bc=bc, mt=mt, inter=inter, m_prev=m_prev, cmat=cmat, q=q, k=k, vb=vb, dmat=dmat,
                            qk=_dot_nt(qb, k.astype(BF16)), qc=_dot(qb, cmat.astype(BF16)))

    def block_output(g, hd):
        lo, hi = hd * M_DH, (hd + 1) * M_DH
        f = first[g, hd]
        nrow = nm_s[g, hd:hd + 1, :]
        s = f["qk"] * jnp.exp(f["dmat"] - f["mt"])
        iw = jnp.exp(f["inter"] - f["mt"])
        num = iw * f["qc"] + _dot(s.astype(BF16), f["vb"])
        den = iw * jnp.sum(f["q"] * nrow, axis=-1, keepdims=True) + jnp.sum(s, axis=-1, keepdims=True)
        mh = num / jnp.maximum(jnp.abs(den), jnp.exp(-f["mt"]))
        mh = mh * _sigmoid(z_cur[span(g), ZC_MO + lo:ZC_MO + hi])
        mix_s[span(g), A_WIDTH + lo:A_WIDTH + hi] = _rms(mh, normm_ref[:, lo:hi])

    def block_state(g, hd):
        f = first[g, hd]
        bc, mt, m_prev = f["bc"], f["mt"], f["m_prev"]
        igc = g_col[span(g), hd:hd + 1]
        b_last = bc[tile - 1:tile, :]
        m_new = mt[tile - 1:tile, :]
        kw = f["k"] * jnp.exp(b_last - bc + igc - m_new)
        decay = jnp.exp(b_last + m_prev - m_new)
        c_s[g, hd] = decay * f["cmat"] + _dot(kw.T.astype(BF16), f["vb"])
        nm_s[g, hd:hd + 1, :] = decay * nm_s[g, hd:hd + 1, :] + jnp.sum(kw, axis=0, keepdims=True)
        nm_s[g, SUBLANES + hd:SUBLANES + hd + 1, :] = jnp.broadcast_to(m_new, (1, LANES))

    blocks = [(g, hd) for hd in range(M_HEADS) for g in streams]
    if group == 1:
        for blk in blocks:
            project_next(UNITS_PER_MLSTM_BLOCK)
            block_scores(*blk)
            block_output(*blk)
            block_state(*blk)
    else:
        for phase in (block_scores, block_output, block_state):
            for blk in blocks:
                project_next()
                phase(*blk)

    for _ in units:
        pass
    co_ref[...] = c_s[...]
    no_ref[...] = nm_s[:, n_rows, :]
    mo_ref[...] = nm_s[:, m_rows, :]
    resid = x1_cur[...] if fused_ffn else xb_ref[...]
    y_ref[...] = resid + _dot(mix_s[...].astype(BF16), wout_ref[...])

    if fused_ffn:
        x1_cur[...] = x1_mid[...]
        x1_mid[...] = x1_new[...]
    for g in streams:
        ubuf[g, 0:SUBLANES, :] = last_rows[g]
        ubuf[g, SUBLANES:SUBLANES + tile, :] = z_new[span(g), OFF_MQK:OFF_MV]
    z_cur[:, ZC_MV:ZC_WIDTH] = z_new[:, OFF_MV:D_IN_PAD]
    g_cur[...] = g_new[...]


def _bias_pieces(rel_bias, tile, band_rows, chunked):
    heads = rel_bias.shape[0]
    span = band_rows + tile - 1
    width = -(-span // LANES) * LANES
    n_far = ATT_BAND + tile - REL_CLIP
    n_near = width - n_far - (2 * REL_CLIP - 1)
    diag = jnp.concatenate([jnp.broadcast_to(rel_bias[:, 2 * REL_CLIP:], (heads, n_far)),
                            rel_bias[:, 2 * REL_CLIP - 1:0:-1],
                            jnp.broadcast_to(rel_bias[:, :1], (heads, n_near))], axis=1)
    i = np.arange(tile)[:, None]
    j = np.arange(band_rows)[None, :]
    visible = j < ATT_BAND + tile
    if chunked:
        qc = i // CHUNK
        kc = (j - ATT_BAND) // CHUNK
        visible = visible & (kc <= qc) & (kc >= qc - ATT_BAND // CHUNK)
    mask = jnp.asarray(np.where(visible, 0.0, NEG * LOG2E), F32)
    return diag * LOG2E, mask


def _mixer(x, params, state, *, tile, chunk, pos0, chunked, ffn=None, cache_t=False):
    nb, frames, d = x.shape
    gmix, w_in, conv_w, conv_b, gate_bias, rel_bias, norm_m, w_out = params
    n_tiles = frames // tile
    band_rows = ATT_BAND + -(-tile // LANES) * LANES
    keep_tiles = min(ATT_BAND, frames) // tile
    width2 = 2 * M_WIDTH
    fused_ffn = ffn is not None
    has_state = state is not None
    group = max(1, min(nb, STEP_ROWS // tile)) if n_tiles == 1 else 1
    assert nb % group == 0
    rows = group * tile

    win = jnp.concatenate([w_in, jnp.zeros((d, D_IN_PAD - w_in.shape[1]), F32)], axis=1).astype(BF16)
    wgt = jnp.zeros((GATE_ROWS, d), F32).at[:N_GATES].set(w_in[:, OFF_MG:].T).astype(BF16)
    gb_row = jnp.zeros((1, LANES), F32).at[0, :N_GATES].set(gate_bias)
    gb_col = jnp.zeros((GATE_ROWS, 1), F32).at[:N_GATES, 0].set(gate_bias)
    diag, mask = _bias_pieces(rel_bias, tile, band_rows, chunked)

    lag = 2 if fused_ffn else 1
    n_tiles_all = (nb // group) * n_tiles
    n_steps = n_tiles_all + lag
    entering = lambda p: jnp.minimum(p, n_tiles_all - 1)
    projected = lambda p: jnp.clip(p - (lag - 1), 0, n_tiles_all - 1)
    mixed = lambda p: jnp.maximum(p - lag, 0)
    per_group = lambda *dims: pl.BlockSpec((group,) + dims,
                                           lambda p: (lax.div(mixed(p), n_tiles),) + (0,) * len(dims))
    enter_spec = pl.BlockSpec((rows, d), lambda p: (entering(p), 0))
    mix_spec = pl.BlockSpec((rows, d), lambda p: (mixed(p), 0))

    def kv_index(p):
        q = projected(p)
        return (lax.div(q, n_tiles) * keep_tiles + jnp.maximum(lax.rem(q, n_tiles) - (n_tiles - keep_tiles), 0), 0)

    kv_spec = pl.BlockSpec((rows, A_WIDTH), kv_index)

    x2d = x.reshape(nb * frames, d)
    args, in_specs = [x2d], [enter_spec]
    scratch_ffn = []
    if fused_ffn:
        g1, w1, w3, w2 = ffn
        args += [g1.reshape(1, d), w1, w3, w2]
        in_specs += [_resident((1, d)), _resident(w1.shape), _resident(w3.shape), _resident(w2.shape)]
        scratch_ffn = [pltpu.VMEM((rows, w1.shape[1]), BF16), pltpu.VMEM((rows, d), BF16)] + [pltpu.VMEM((rows, d), F32)] * 3
    else:
        args.append(x2d)
        in_specs.append(mix_spec)
    args += [gmix.reshape(1, d), win, wgt, conv_w, conv_b.reshape(1, width2), gb_row, gb_col, diag, mask,
             norm_m.reshape(1, M_WIDTH), w_out.astype(BF16)]
    in_specs += [_resident((1, d)), _resident(win.shape), _resident(wgt.shape),
                 _resident((CONV_W, width2)), _resident((1, width2)), _resident((1, LANES)),
                 _resident((GATE_ROWS, 1)), _resident(diag.shape), _resident(mask.shape), _resident((1, M_WIDTH)),
                 _resident((d, d))]
    if has_state:
        k0, v0, c0, n0, m0, conv0 = state
        conv0p = jnp.concatenate([jnp.zeros((nb, SUBLANES - (CONV_W - 1), width2), F32), conv0], axis=1)
        m0p = jnp.broadcast_to(m0[:, :, None], (nb, M_HEADS, LANES))
        args += [k0, v0, c0, n0, m0p, conv0p]
        in_specs += [per_group(ATT_BAND, A_WIDTH), per_group(ATT_BAND, A_WIDTH),
                     per_group(M_HEADS, M_DH, M_DH), per_group(M_HEADS, M_DH), per_group(M_HEADS, LANES),
                     per_group(SUBLANES, width2)]
    out_specs = [
        mix_spec, kv_spec, kv_spec, per_group(M_HEADS, M_DH, M_DH), per_group(M_HEADS, M_DH),
        per_group(M_HEADS, LANES), per_group(SUBLANES, width2),
    ]
    keep = keep_tiles * tile
    out_shape = [
        jax.ShapeDtypeStruct((nb * frames, d), F32),
        jax.ShapeDtypeStruct((nb * keep, A_WIDTH), F32),
        jax.ShapeDtypeStruct((nb * keep, A_WIDTH), F32),
        jax.ShapeDtypeStruct((nb, M_HEADS, M_DH, M_DH), F32),
        jax.ShapeDtypeStruct((nb, M_HEADS, M_DH), F32),
        jax.ShapeDtypeStruct((nb, M_HEADS, LANES), F32),
        jax.ShapeDtypeStruct((nb, SUBLANES, width2), F32),
    ]
    scratch = [
        pltpu.VMEM((rows, D_IN_PAD), F32), pltpu.VMEM((rows, ZC_WIDTH), F32),
        pltpu.VMEM((GATE_ROWS, rows), F32), pltpu.VMEM((GATE_ROWS, rows), F32),
        pltpu.VMEM((group, band_rows, A_WIDTH), BF16), pltpu.VMEM((group, band_rows, A_WIDTH), BF16),
        pltpu.VMEM((group, tile + SUBLANES, width2), F32),
        pltpu.VMEM((group, M_HEADS, M_DH, M_DH), F32),
        pltpu.VMEM((group, NM_ROWS, LANES), F32),
        pltpu.VMEM((rows, d), F32), pltpu.VMEM((rows, d), BF16),
        pltpu.VMEM((A_HEADS, tile, band_rows), F32),
    ] + scratch_ffn
    if cache_t:
        scratch += [pltpu.VMEM((group, A_WIDTH, ATT_BAND), BF16)] * 2
    y, ko, vo, c1, n1, m1, conv1 = pl.pallas_call(
        functools.partial(_mixer_kernel, tile=tile, group=group, chunk=chunk, band_rows=band_rows, pos0=pos0,
                          n_tiles=n_tiles, fused_ffn=fused_ffn, has_state=has_state, cache_t=cache_t,
                          ahead=PROMPT_AHEAD if group == 1 else ATTENTION_AHEAD),
        grid=(n_steps,),
        in_specs=in_specs,
        out_specs=out_specs,
        out_shape=out_shape,
        scratch_shapes=scratch,
        compiler_params=pltpu.CompilerParams(dimension_semantics=("arbitrary",),
                                             vmem_limit_bytes=VMEM_LIMIT_BYTES),
        name="mixer_chunked" if chunked else "mixer_step",
    )(*args)
    new_state = (ko.reshape(nb, keep, A_HEADS, A_DH), vo.reshape(nb, keep, A_HEADS, A_DH), c1, n1,
                 m1[:, :, 0], conv1[:, SUBLANES - (CONV_W - 1):, :])
    return y.reshape(nb, frames, d), new_state


def kernel(x_prompt, x_sample, cache_attn_k, cache_attn_v, state_mlstm_C, state_mlstm_n, state_mlstm_m, state_mlstm_conv, norm_ffn1, w1_ffn1, w3_ffn1, w2_ffn1, norm_mix, w_in, conv_w, conv_b, gate_bias, rel_bias, norm_mlstm_out, w_out, norm_ffn2, w1_ffn2, w3_ffn2, w2_ffn2, norm_final):
    depth = norm_ffn1.shape[0]
    nbp, seq, d = x_prompt.shape
    nbs, dec, _ = x_sample.shape
    xp = x_prompt
    xs = x_sample.reshape(nbs * dec, d)
    prompt_tile = min(PROMPT_TILE, seq)
    new_p, new_s = [], []
    for l in range(depth):
        last = l == depth - 1
        ffn1 = _ffn_weights(w1_ffn1[l], w3_ffn1[l], w2_ffn1[l])
        ffn2 = _ffn_weights(w1_ffn2[l], w3_ffn2[l], w2_ffn2[l])
        mix = (norm_mix[l], w_in[l], conv_w[l], conv_b[l], gate_bias[l], rel_bias[l], norm_mlstm_out[l], w_out[l])
        gf = norm_final if last else None

        xp, st = _mixer(xp, mix, None, tile=prompt_tile, chunk=prompt_tile, pos0=0, chunked=True,
                        ffn=(norm_ffn1[l],) + ffn1)
        new_p.append(st)
        xp = _ffn(xp.reshape(nbp * seq, d), norm_ffn2[l], ffn2, gf).reshape(nbp, seq, d)

        xs = _ffn(xs, norm_ffn1[l], ffn1)
        feature_major = lambda c: c.transpose(0, 2, 3, 1).reshape(nbs, A_WIDTH, -1)
        cache = (feature_major(cache_attn_k[l]), feature_major(cache_attn_v[l]),
                 state_mlstm_C[l], state_mlstm_n[l], state_mlstm_m[l], state_mlstm_conv[l])
        xs3, st = _mixer(xs.reshape(nbs, dec, d), mix, cache, tile=dec, chunk=dec, pos0=PAST_LEN, chunked=False,
                         cache_t=True)
        new_s.append(st)
        xs = _ffn(xs3.reshape(nbs * dec, d), norm_ffn2[l], ffn2, gf)

    stack = lambda states, i: jnp.stack([s[i] for s in states])
    return ((xp, xs.reshape(nbs, dec, d))
            + tuple(stack(new_p, i) for i in range(6)) + tuple(stack(new_s, i) for i in range(6)))
```

```python
import functools

import numpy as np
import jax
import jax.numpy as jnp
from jax import lax
from jax.experimental import pallas as pl
from jax.experimental.pallas import tpu as pltpu

F32 = jnp.float32
BF16 = jnp.bfloat16

CHUNK = 64
ATT_BAND = 8 * CHUNK
A_HEADS = 8
A_DH = 64
A_WIDTH = A_HEADS * A_DH
M_HEADS = 4
M_DH = 128
M_WIDTH = M_HEADS * M_DH
REL_CLIP = 128
CONV_W = 4
PAST_LEN = 4096
EPS = 1e-6
NEG = -1e30
LOG2E = 1.4426950408889634

LANES = 128
SUBLANES = 8
MXU_DIM = 256
VMEM_LIMIT_BYTES = 60 * 1024 * 1024

OFF_AK = A_WIDTH
OFF_AV = 2 * A_WIDTH
OFF_MQK = 3 * A_WIDTH
OFF_MV = OFF_MQK + 2 * M_WIDTH
OFF_MO = OFF_MV + M_WIDTH
OFF_MG = OFF_MO + M_WIDTH
N_GATES = 2 * M_HEADS
D_IN_PAD = OFF_MG + LANES
GATE_ROWS = 16
ZC_MV = A_WIDTH
ZC_MO = ZC_MV + M_WIDTH
ZC_MG = ZC_MO + M_WIDTH
ZC_WIDTH = ZC_MG + LANES
NM_ROWS = 32

FFN_ROWS = 512
FFN_CHUNK = MXU_DIM
PROMPT_TILE = 256
STEP_ROWS = 128
ATTENTION_AHEAD = 4
PROMPT_AHEAD = 2
UNITS_BEFORE_CONV = 2
UNITS_BEFORE_GATES = 2
UNITS_PER_MLSTM_BLOCK = 3


def _rms(x, g):
    return x * lax.rsqrt(jnp.mean(x * x, axis=-1, keepdims=True) + EPS) * g


def _sigmoid(x):
    return 1.0 / (1.0 + jnp.exp(-x))


def _silu(x):
    h = 0.5 * x
    return h + h * jnp.tanh(h)


def _log_sigmoid(x):
    return jnp.minimum(x, 0.0) - jnp.log1p(jnp.exp(-jnp.abs(x)))


def _dot(a, b):
    return jnp.dot(a, b, preferred_element_type=F32)


def _dot_nt(a, b):
    return lax.dot_general(a, b, (((1,), (1,)), ((), ())), preferred_element_type=F32)


def _split3(x):
    p1 = x.astype(BF16)
    r1 = x - p1.astype(F32)
    p2 = r1.astype(BF16)
    p3 = (r1 - p2.astype(F32)).astype(BF16)
    return p1, p2, p3


def _resident(shape):
    nd = len(shape)
    return pl.BlockSpec(shape, lambda *_: (0,) * nd, pipeline_mode=pl.Buffered(1))


def _ffn_kernel(x_ref, g_ref, w1_ref, w3_ref, w2_ref, *rest, final_norm):
    if final_norm:
        gf_ref, o_ref, h_ref, u_ref = rest
    else:
        o_ref, h_ref, u_ref = rest
    h_ref[...] = _rms(x_ref[...], g_ref[...]).astype(BF16)
    for c in range(0, u_ref.shape[1], FFN_CHUNK):
        h = h_ref[...]
        a = _dot(h, w1_ref[:, c:c + FFN_CHUNK])
        b = _dot(h, w3_ref[:, c:c + FFN_CHUNK])
        u_ref[:, c:c + FFN_CHUNK] = (_silu(a) * b).astype(BF16)
    y = x_ref[...] + 0.5 * _dot(u_ref[...], w2_ref[...])
    if final_norm:
        y = _rms(y, gf_ref[...])
    o_ref[...] = y


def _ffn_weights(w1, w3, w2):
    return w1.astype(BF16), w3.astype(BF16), w2.astype(BF16)


def _ffn(x2d, g, weights, gf=None):
    n, d = x2d.shape
    w1, w3, w2 = weights
    f = w1.shape[1]
    assert f % FFN_CHUNK == 0
    rows = min(FFN_ROWS, n)
    final_norm = gf is not None
    row_spec = pl.BlockSpec((rows, d), lambda i: (i, 0))
    in_specs = [row_spec, _resident((1, d)), _resident(w1.shape), _resident(w3.shape), _resident(w2.shape)]
    args = [x2d, g.reshape(1, d), w1, w3, w2]
    if final_norm:
        in_specs.append(_resident((1, d)))
        args.append(gf.reshape(1, d))
    return pl.pallas_call(
        functools.partial(_ffn_kernel, final_norm=final_norm),
        grid=(n // rows,),
        in_specs=in_specs,
        out_specs=row_spec,
        out_shape=jax.ShapeDtypeStruct((n, d), F32),
        scratch_shapes=[pltpu.VMEM((rows, d), BF16), pltpu.VMEM((rows, f), BF16)],
        compiler_params=pltpu.CompilerParams(dimension_semantics=("arbitrary",),
                                             vmem_limit_bytes=VMEM_LIMIT_BYTES),
        name="ffn_final" if final_norm else "ffn",
    )(*args)


def _mixer_kernel(*refs, tile, group, chunk, band_rows, pos0, n_tiles, fused_ffn, has_state, cache_t, ahead):
    refs = iter(refs)
    take = lambda n: [next(refs) for _ in range(n)]
    (xa_ref,) = take(1)
    if fused_ffn:
        g1_ref, w1_ref, w3_ref, w2_ref = take(4)
    else:
        (xb_ref,) = take(1)
    gmix_ref, win_ref, wgt_ref, convw_ref, convb_ref, gbrow_ref, gbcol_ref, diag_ref, mask_ref, normm_ref, wout_ref = take(11)
    if has_state:
        k0_ref, v0_ref, c0_ref, n0_ref, m0_ref, conv0_ref = take(6)
    y_ref, ko_ref, vo_ref, co_ref, no_ref, mo_ref, convo_ref = take(7)
    z_new, z_cur, g_new, g_cur, kband, vband, ubuf, c_s, nm_s, mix_s, h_s, tab_s = take(12)
    n_rows, m_rows = slice(0, M_HEADS), slice(SUBLANES, SUBLANES + M_HEADS)
    if fused_ffn:
        u_s, h1_s, x1_new, x1_mid, x1_cur = take(5)
    if cache_t:
        kt_s, vt_s = take(2)
        assert has_state and n_tiles == 1 and pos0 >= ATT_BAND
    assert group == 1 or n_tiles == 1
    streams = range(group)
    span = lambda g: slice(g * tile, (g + 1) * tile)

    p = pl.program_id(0)
    lag = 2 if fused_ffn else 1
    t = lax.rem(jnp.maximum(p - lag, 0), n_tiles)

    @pl.when(p == 0)
    def _first_step():
        z_cur[...] = jnp.zeros_like(z_cur)
        g_cur[...] = jnp.zeros_like(g_cur)
        for g in streams:
            kband[g, ATT_BAND:band_rows, :] = jnp.zeros((band_rows - ATT_BAND, A_WIDTH), BF16)
            vband[g, ATT_BAND:band_rows, :] = jnp.zeros((band_rows - ATT_BAND, A_WIDTH), BF16)
            ubuf[g, SUBLANES:SUBLANES + tile, :] = jnp.zeros((tile, 2 * M_WIDTH), F32)
        if fused_ffn:
            x1_cur[...] = jnp.zeros_like(x1_cur)
            x1_mid[...] = jnp.zeros_like(x1_mid)
        width = diag_ref.shape[1]
        for head in range(A_HEADS):
            rows_of_diag = jnp.broadcast_to(diag_ref[head:head + 1, :], (tile, width))
            skew = pltpu.roll(rows_of_diag, width - (tile - 1), 1, stride=1, stride_axis=0)
            tab_s[head] = skew[:, :band_rows] + mask_ref[...]

    @pl.when(t == 0)
    def _load_state():
        if has_state:
            for g in streams:
                if cache_t:
                    kt_s[g] = k0_ref[g].astype(BF16)
                    vt_s[g] = v0_ref[g].astype(BF16)
                else:
                    kband[g, 0:ATT_BAND, :] = k0_ref[g].astype(BF16)
                    vband[g, 0:ATT_BAND, :] = v0_ref[g].astype(BF16)
                ubuf[g, 0:SUBLANES, :] = conv0_ref[g]
            c_s[...] = c0_ref[...]
            nm_s[:, n_rows, :] = n0_ref[...]
            nm_s[:, m_rows, :] = m0_ref[...]
        else:
            for g in streams:
                kband[g, 0:ATT_BAND, :] = jnp.zeros((ATT_BAND, A_WIDTH), BF16)
                vband[g, 0:ATT_BAND, :] = jnp.zeros((ATT_BAND, A_WIDTH), BF16)
                ubuf[g, 0:SUBLANES, :] = jnp.zeros((SUBLANES, 2 * M_WIDTH), F32)
            c_s[...] = jnp.zeros_like(c_s)
            nm_s[...] = jnp.zeros_like(nm_s)

    def stage_a1():
        h1_s[...] = _rms(xa_ref[...], g1_ref[...]).astype(BF16)
        yield
        for c in range(0, u_s.shape[1], FFN_CHUNK):
            hf = h1_s[...]
            a = _dot(hf, w1_ref[:, c:c + FFN_CHUNK])
            b = _dot(hf, w3_ref[:, c:c + FFN_CHUNK])
            u_s[:, c:c + FFN_CHUNK] = (_silu(a) * b).astype(BF16)
            yield
        for c in range(0, x1_new.shape[1], MXU_DIM):
            x1_new[:, c:c + MXU_DIM] = (xa_ref[:, c:c + MXU_DIM]
                                        + 0.5 * _dot(u_s[...], w2_ref[:, c:c + MXU_DIM]))
            yield

    def stage_a2():
        x_in = x1_mid[...] if fused_ffn else xa_ref[...]
        h_s[...] = _rms(x_in, gmix_ref[...]).astype(BF16)
        yield
        for c in range(0, D_IN_PAD, MXU_DIM):
            c1 = min(c + MXU_DIM, D_IN_PAD)
            z_new[:, c:c1] = _dot(h_s[...], win_ref[:, c:c1])
            projected_cols[0] = c1
            yield
        g_new[...] = _dot_nt(wgt_ref[...], h_s[...])

    projected_cols = [0]

    def alternate(*gens):
        gens = list(gens)
        while gens:
            for gen in list(gens):
                try:
                    next(gen)
                    yield
                except StopIteration:
                    gens.remove(gen)

    units = alternate(stage_a1(), stage_a2()) if fused_ffn else stage_a2()

    def project_next(count=1):
        for _ in range(count):
            next(units, None)

    lane = lax.broadcasted_iota(jnp.int32, (1, LANES), 1)
    even = lane < A_DH
    if pos0 < ATT_BAND:
        col = lax.broadcasted_iota(jnp.int32, (1, band_rows), 1)
        in_stream = col >= (ATT_BAND - pos0) - t * tile

    def scores(g, head):
        lo = (head // 2) * LANES
        mine = even if head % 2 == 0 else jnp.logical_not(even)
        qh = jnp.where(mine, z_cur[span(g), lo:lo + LANES] * (LOG2E * A_DH ** -0.5), 0.0).astype(BF16)
        if cache_t:
            s_old = _dot(qh, kt_s[g, lo:lo + LANES, :]) + tab_s[head, :, 0:ATT_BAND]
            s_new = _dot_nt(qh, kband[g, ATT_BAND:band_rows, lo:lo + LANES]) + tab_s[head, :, ATT_BAND:band_rows]
            return s_old, s_new
        s = _dot_nt(qh, kband[g, :, lo:lo + LANES]) + tab_s[head]
        if pos0 < ATT_BAND:
            s = jnp.where(in_stream, s, NEG)
        return s

    def attend(g, head, s):
        lo = (head // 2) * LANES
        mine = even if head % 2 == 0 else jnp.logical_not(even)
        if cache_t:
            s_old, s_new = s
            top = jnp.maximum(jnp.max(s_old, axis=-1, keepdims=True), jnp.max(s_new, axis=-1, keepdims=True))
            e_old = jnp.exp2(s_old - top)
            e_new = jnp.exp2(s_new - top)
            o = (_dot_nt(e_old.astype(BF16), vt_s[g, lo:lo + LANES, :])
                 + _dot(e_new.astype(BF16), vband[g, ATT_BAND:band_rows, lo:lo + LANES]))
            total = jnp.sum(e_old, axis=-1, keepdims=True) + jnp.sum(e_new, axis=-1, keepdims=True)
            o = jnp.where(mine, o, 0.0) * (1.0 / total)
        else:
            vp = vband[g, :, lo:lo + LANES]
            e = jnp.exp2(s - jnp.max(s, axis=-1, keepdims=True))
            o = _dot(e.astype(BF16), jnp.where(mine, vp, jnp.zeros_like(vp)))
            o = o * (1.0 / jnp.sum(e, axis=-1, keepdims=True))
        if head % 2 == 0:
            mix_s[span(g), lo:lo + LANES] = o
        else:
            mix_s[span(g), lo:lo + LANES] += o

    pending = []
    for g in streams:
        for head in range(A_HEADS):
            pending.append((g, head, scores(g, head)))
            project_next(lag)
            if len(pending) > ahead:
                attend(*pending.pop(0))
    for item in pending:
        attend(*item)
    assert projected_cols[0] >= OFF_MQK
    ko_ref[...] = z_new[:, OFF_AK:OFF_AV]
    vo_ref[...] = z_new[:, OFF_AV:OFF_MQK]
    for g in streams:
        if n_tiles > 1:
            kband[g, 0:ATT_BAND, :] = kband[g, tile:tile + ATT_BAND, :]
            vband[g, 0:ATT_BAND, :] = vband[g, tile:tile + ATT_BAND, :]
        kband[g, ATT_BAND:ATT_BAND + tile, :] = z_new[span(g), OFF_AK:OFF_AV].astype(BF16)
        vband[g, ATT_BAND:ATT_BAND + tile, :] = z_new[span(g), OFF_AV:OFF_MQK].astype(BF16)
    z_cur[:, 0:OFF_AK] = z_new[:, 0:OFF_AK]

    project_next(UNITS_BEFORE_CONV)
    mq, mk, last_rows = [], [], []
    for g in streams:
        frames = ubuf[g]
        qk = convb_ref[...] + frames[SUBLANES:, :] * convw_ref[CONV_W - 1:CONV_W, :]
        for j in range(CONV_W - 1):
            shifted = pltpu.roll(frames, CONV_W - 1 - j, 0)
            qk = qk + shifted[SUBLANES:, :] * convw_ref[j:j + 1, :]
        qk = _silu(qk)
        mq.append(qk[:, :M_WIDTH])
        mk.append(qk[:, M_WIDTH:] * (M_DH ** -0.5))
        last_rows.append(ubuf[g, tile:tile + SUBLANES, :])
        convo_ref[g] = last_rows[g]

    project_next(UNITS_BEFORE_GATES)
    rows = group * tile
    g_col = z_cur[:, ZC_MG:ZC_MG + LANES] + gbrow_ref[...]
    g_row = g_cur[...] + gbcol_ref[...]
    lf_col = _log_sigmoid(g_col)
    lf_row = _log_sigmoid(g_row)
    ri = lax.broadcasted_iota(jnp.int32, (rows, rows), 0)
    ci = lax.broadcasted_iota(jnp.int32, (rows, rows), 1)
    assert chunk & (chunk - 1) == 0
    chunk_shift = chunk.bit_length() - 1
    same_chunk = (ri >> chunk_shift) == (ci >> chunk_shift)
    tri = jnp.where(jnp.logical_and(same_chunk, ci <= ri), 1.0, 0.0).astype(BF16)
    tri_t = jnp.where(jnp.logical_and(same_chunk, ri <= ci), 1.0, 0.0).astype(BF16)
    b_col = sum(_dot(tri, part) for part in _split3(lf_col))
    b_row = sum(_dot(part, tri_t) for part in _split3(lf_row))

    causal = (lax.broadcasted_iota(jnp.int32, (chunk, chunk), 1)
              <= lax.broadcasted_iota(jnp.int32, (chunk, chunk), 0))
    assert chunk == tile
    first = {}

    def block_scores(g, hd):
        lo, hi = hd * M_DH, (hd + 1) * M_DH
        cmat = c_s[g, hd]
        m_prev = nm_s[g, SUBLANES + hd:SUBLANES + hd + 1, 0:1]
        bc = b_col[span(g), M_HEADS + hd:M_HEADS + hd + 1]
        br = b_row[M_HEADS + hd:M_HEADS + hd + 1, span(g)]
        igr = g_row[hd:hd + 1, span(g)]
        dmat = jnp.where(causal, bc + (igr - br), NEG)
        inter = bc + m_prev
        mt = jnp.maximum(inter, jnp.max(dmat, axis=-1, keepdims=True))
        q = mq[g][:, lo:hi]
        k = mk[g][:, lo:hi]
        vb = z_cur[span(g), ZC_MV + lo:ZC_MV + hi].astype(BF16)
        qb = q.astype(BF16)
        first[g, hd] = dict(bc=bc, mt=mt, inter=inter, m_prev=m_prev, cmat=cmat, q=q, k=k, vb=vb, dmat=dmat,
                            qk=_dot_nt(qb, k.astype(BF16)), qc=_dot(qb, cmat.astype(BF16)))

    def block_output(g, hd):
        lo, hi = hd * M_DH, (hd + 1) * M_DH
        f = first[g, hd]
        nrow = nm_s[g, hd:hd + 1, :]
        s = f["qk"] * jnp.exp(f["dmat"] - f["mt"])
        iw = jnp.exp(f["inter"] - f["mt"])
        num = iw * f["qc"] + _dot(s.astype(BF16), f["vb"])
        den = iw * jnp.sum(f["q"] * nrow, axis=-1, keepdims=True) + jnp.sum(s, axis=-1, keepdims=True)
        mh = num / jnp.maximum(jnp.abs(den), jnp.exp(-f["mt"]))
        mh = mh * _sigmoid(z_cur[span(g), ZC_MO + lo:ZC_MO + hi])
        mix_s[span(g), A_WIDTH + lo:A_WIDTH + hi] = _rms(mh, normm_ref[:, lo:hi])

    def block_state(g, hd):
        f = first[g, hd]
        bc, mt, m_prev = f["bc"], f["mt"], f["m_prev"]
        igc = g_col[span(g), hd:hd + 1]
        b_last = bc[tile - 1:tile, :]
        m_new = mt[tile - 1:tile, :]
        kw = f["k"] * jnp.exp(b_last - bc + igc - m_new)
        decay = jnp.exp(b_last + m_prev - m_new)
        c_s[g, hd] = decay * f["cmat"] + _dot(kw.T.astype(BF16), f["vb"])
        nm_s[g, hd:hd + 1, :] = decay * nm_s[g, hd:hd + 1, :] + jnp.sum(kw, axis=0, keepdims=True)
        nm_s[g, SUBLANES + hd:SUBLANES + hd + 1, :] = jnp.broadcast_to(m_new, (1, LANES))

    blocks = [(g, hd) for hd in range(M_HEADS) for g in streams]
    if group == 1:
        for blk in blocks:
            project_next(UNITS_PER_MLSTM_BLOCK)
            block_scores(*blk)
            block_output(*blk)
            block_state(*blk)
    else:
        for phase in (block_scores, block_output, block_state):
            for blk in blocks:
                project_next()
                phase(*blk)

    for _ in units:
        pass
    co_ref[...] = c_s[...]
    no_ref[...] = nm_s[:, n_rows, :]
    mo_ref[...] = nm_s[:, m_rows, :]
    resid = x1_cur[...] if fused_ffn else xb_ref[...]
    y_ref[...] = resid + _dot(mix_s[...].astype(BF16), wout_ref[...])

    if fused_ffn:
        x1_cur[...] = x1_mid[...]
        x1_mid[...] = x1_new[...]
    for g in streams:
        ubuf[g, 0:SUBLANES, :] = last_rows[g]
        ubuf[g, SUBLANES:SUBLANES + tile, :] = z_new[span(g), OFF_MQK:OFF_MV]
    z_cur[:, ZC_MV:ZC_WIDTH] = z_new[:, OFF_MV:D_IN_PAD]
    g_cur[...] = g_new[...]


def _bias_pieces(rel_bias, tile, band_rows, chunked):
    heads = rel_bias.shape[0]
    span = band_rows + tile - 1
    width = -(-span // LANES) * LANES
    n_far = ATT_BAND + tile - REL_CLIP
    n_near = width - n_far - (2 * REL_CLIP - 1)
    diag = jnp.concatenate([jnp.broadcast_to(rel_bias[:, 2 * REL_CLIP:], (heads, n_far)),
                            rel_bias[:, 2 * REL_CLIP - 1:0:-1],
                            jnp.broadcast_to(rel_bias[:, :1], (heads, n_near))], axis=1)
    i = np.arange(tile)[:, None]
    j = np.arange(band_rows)[None, :]
    visible = j < ATT_BAND + tile
    if chunked:
        qc = i // CHUNK
        kc = (j - ATT_BAND) // CHUNK
        visible = visible & (kc <= qc) & (kc >= qc - ATT_BAND // CHUNK)
    mask = jnp.asarray(np.where(visible, 0.0, NEG * LOG2E), F32)
    return diag * LOG2E, mask


def _mixer(x, params, state, *, tile, chunk, pos0, chunked, ffn=None, cache_t=False):
    nb, frames, d = x.shape
    gmix, w_in, conv_w, conv_b, gate_bias, rel_bias, norm_m, w_out = params
    n_tiles = frames // tile
    band_rows = ATT_BAND + -(-tile // LANES) * LANES
    keep_tiles = min(ATT_BAND, frames) // tile
    width2 = 2 * M_WIDTH
    fused_ffn = ffn is not None
    has_state = state is not None
    group = max(1, min(nb, STEP_ROWS // tile)) if n_tiles == 1 else 1
    assert nb % group == 0
    rows = group * tile

    win = jnp.concatenate([w_in, jnp.zeros((d, D_IN_PAD - w_in.shape[1]), F32)], axis=1).astype(BF16)
    wgt = jnp.zeros((GATE_ROWS, d), F32).at[:N_GATES].set(w_in[:, OFF_MG:].T).astype(BF16)
    gb_row = jnp.zeros((1, LANES), F32).at[0, :N_GATES].set(gate_bias)
    gb_col = jnp.zeros((GATE_ROWS, 1), F32).at[:N_GATES, 0].set(gate_bias)
    diag, mask = _bias_pieces(rel_bias, tile, band_rows, chunked)

    lag = 2 if fused_ffn else 1
    n_tiles_all = (nb // group) * n_tiles
    n_steps = n_tiles_all + lag
    entering = lambda p: jnp.minimum(p, n_tiles_all - 1)
    projected = lambda p: jnp.clip(p - (lag - 1), 0, n_tiles_all - 1)
    mixed = lambda p: jnp.maximum(p - lag, 0)
    per_group = lambda *dims: pl.BlockSpec((group,) + dims,
                                           lambda p: (lax.div(mixed(p), n_tiles),) + (0,) * len(dims))
    enter_spec = pl.BlockSpec((rows, d), lambda p: (entering(p), 0))
    mix_spec = pl.BlockSpec((rows, d), lambda p: (mixed(p), 0))

    def kv_index(p):
        q = projected(p)
        return (lax.div(q, n_tiles) * keep_tiles + jnp.maximum(lax.rem(q, n_tiles) - (n_tiles - keep_tiles), 0), 0)

    kv_spec = pl.BlockSpec((rows, A_WIDTH), kv_index)

    x2d = x.reshape(nb * frames, d)
    args, in_specs = [x2d], [enter_spec]
    scratch_ffn = []
    if fused_ffn:
        g1, w1, w3, w2 = ffn
        args += [g1.reshape(1, d), w1, w3, w2]
        in_specs += [_resident((1, d)), _resident(w1.shape), _resident(w3.shape), _resident(w2.shape)]
        scratch_ffn = [pltpu.VMEM((rows, w1.shape[1]), BF16), pltpu.VMEM((rows, d), BF16)] + [pltpu.VMEM((rows, d), F32)] * 3
    else:
        args.append(x2d)
        in_specs.append(mix_spec)
    args += [gmix.reshape(1, d), win, wgt, conv_w, conv_b.reshape(1, width2), gb_row, gb_col, diag, mask,
             norm_m.reshape(1, M_WIDTH), w_out.astype(BF16)]
    in_specs += [_resident((1, d)), _resident(win.shape), _resident(wgt.shape),
                 _resident((CONV_W, width2)), _resident((1, width2)), _resident((1, LANES)),
                 _resident((GATE_ROWS, 1)), _resident(diag.shape), _resident(mask.shape), _resident((1, M_WIDTH)),
                 _resident((d, d))]
    if has_state:
        k0, v0, c0, n0, m0, conv0 = state
        conv0p = jnp.concatenate([jnp.zeros((nb, SUBLANES - (CONV_W - 1), width2), F32), conv0], axis=1)
        m0p = jnp.broadcast_to(m0[:, :, None], (nb, M_HEADS, LANES))
        args += [k0, v0, c0, n0, m0p, conv0p]
        in_specs += [per_group(ATT_BAND, A_WIDTH), per_group(ATT_BAND, A_WIDTH),
                     per_group(M_HEADS, M_DH, M_DH), per_group(M_HEADS, M_DH), per_group(M_HEADS, LANES),
                     per_group(SUBLANES, width2)]
    out_specs = [
        mix_spec, kv_spec, kv_spec, per_group(M_HEADS, M_DH, M_DH), per_group(M_HEADS, M_DH),
        per_group(M_HEADS, LANES), per_group(SUBLANES, width2),
    ]
    keep = keep_tiles * tile
    out_shape = [
        jax.ShapeDtypeStruct((nb * frames, d), F32),
        jax.ShapeDtypeStruct((nb * keep, A_WIDTH), F32),
        jax.ShapeDtypeStruct((nb * keep, A_WIDTH), F32),
        jax.ShapeDtypeStruct((nb, M_HEADS, M_DH, M_DH), F32),
        jax.ShapeDtypeStruct((nb, M_HEADS, M_DH), F32),
        jax.ShapeDtypeStruct((nb, M_HEADS, LANES), F32),
        jax.ShapeDtypeStruct((nb, SUBLANES, width2), F32),
    ]
    scratch = [
        pltpu.VMEM((rows, D_IN_PAD), F32), pltpu.VMEM((rows, ZC_WIDTH), F32),
        pltpu.VMEM((GATE_ROWS, rows), F32), pltpu.VMEM((GATE_ROWS, rows), F32),
        pltpu.VMEM((group, band_rows, A_WIDTH), BF16), pltpu.VMEM((group, band_rows, A_WIDTH), BF16),
        pltpu.VMEM((group, tile + SUBLANES, width2), F32),
        pltpu.VMEM((group, M_HEADS, M_DH, M_DH), F32),
        pltpu.VMEM((group, NM_ROWS, LANES), F32),
        pltpu.VMEM((rows, d), F32), pltpu.VMEM((rows, d), BF16),
        pltpu.VMEM((A_HEADS, tile, band_rows), F32),
    ] + scratch_ffn
    if cache_t:
        scratch += [pltpu.VMEM((group, A_WIDTH, ATT_BAND), BF16)] * 2
    y, ko, vo, c1, n1, m1, conv1 = pl.pallas_call(
        functools.partial(_mixer_kernel, tile=tile, group=group, chunk=chunk, band_rows=band_rows, pos0=pos0,
                          n_tiles=n_tiles, fused_ffn=fused_ffn, has_state=has_state, cache_t=cache_t,
                          ahead=PROMPT_AHEAD if group == 1 else ATTENTION_AHEAD),
        grid=(n_steps,),
        in_specs=in_specs,
        out_specs=out_specs,
        out_shape=out_shape,
        scratch_shapes=scratch,
        compiler_params=pltpu.CompilerParams(dimension_semantics=("arbitrary",),
                                             vmem_limit_bytes=VMEM_LIMIT_BYTES),
        name="mixer_chunked" if chunked else "mixer_step",
    )(*args)
    new_state = (ko.reshape(nb, keep, A_HEADS, A_DH), vo.reshape(nb, keep, A_HEADS, A_DH), c1, n1,
                 m1[:, :, 0], conv1[:, SUBLANES - (CONV_W - 1):, :])
    return y.reshape(nb, frames, d), new_state


def kernel(x_prompt, x_sample, cache_attn_k, cache_attn_v, state_mlstm_C, state_mlstm_n, state_mlstm_m, state_mlstm_conv, norm_ffn1, w1_ffn1, w3_ffn1, w2_ffn1, norm_mix, w_in, conv_w, conv_b, gate_bias, rel_bias, norm_mlstm_out, w_out, norm_ffn2, w1_ffn2, w3_ffn2, w2_ffn2, norm_final):
    depth = norm_ffn1.shape[0]
    nbp, seq, d = x_prompt.shape
    nbs, dec, _ = x_sample.shape
    xp = x_prompt
    xs = x_sample.reshape(nbs * dec, d)
    prompt_tile = min(PROMPT_TILE, seq)
    new_p, new_s = [], []
    for l in range(depth):
        last = l == depth - 1
        ffn1 = _ffn_weights(w1_ffn1[l], w3_ffn1[l], w2_ffn1[l])
        ffn2 = _ffn_weights(w1_ffn2[l], w3_ffn2[l], w2_ffn2[l])
        mix = (norm_mix[l], w_in[l], conv_w[l], conv_b[l], gate_bias[l], rel_bias[l], norm_mlstm_out[l], w_out[l])
        gf = norm_final if last else None

        xp, st = _mixer(xp, mix, None, tile=prompt_tile, chunk=prompt_tile, pos0=0, chunked=True,
                        ffn=(norm_ffn1[l],) + ffn1)
        new_p.append(st)
        xp = _ffn(xp.reshape(nbp * seq, d), norm_ffn2[l], ffn2, gf).reshape(nbp, seq, d)

        xs = _ffn(xs, norm_ffn1[l], ffn1)
        feature_major = lambda c: c.transpose(0, 2, 3, 1).reshape(nbs, A_WIDTH, -1)
        cache = (feature_major(cache_attn_k[l]), feature_major(cache_attn_v[l]),
                 state_mlstm_C[l], state_mlstm_n[l], state_mlstm_m[l], state_mlstm_conv[l])
        xs3, st = _mixer(xs.reshape(nbs, dec, d), mix, cache, tile=dec, chunk=dec, pos0=PAST_LEN, chunked=False,
                         cache_t=True)
        new_s.append(st)
        xs = _ffn(xs3.reshape(nbs * dec, d), norm_ffn2[l], ffn2, gf)

    stack = lambda states, i: jnp.stack([s[i] for s in states])
    return ((xp, xs.reshape(nbs, dec, d))
            + tuple(stack(new_p, i) for i in range(6)) + tuple(stack(new_s, i) for i in range(6)))
```

```python
import functools

import numpy as np
import jax
import jax.numpy as jnp
from jax import lax
from jax.experimental import pallas as pl
from jax.experimental.pallas import tpu as pltpu

F32 = jnp.float32
BF16 = jnp.bfloat16

CHUNK = 64
ATT_BAND = 8 * CHUNK
A_HEADS = 8
A_DH = 64
A_WIDTH = A_HEADS * A_DH
M_HEADS = 4
M_DH = 128
M_WIDTH = M_HEADS * M_DH
REL_CLIP = 128
CONV_W = 4
PAST_LEN = 4096
EPS = 1e-6
NEG = -1e30
LOG2E = 1.4426950408889634

LANES = 128
SUBLANES = 8
MXU_DIM = 256
VMEM_LIMIT_BYTES = 60 * 1024 * 1024

OFF_AK = A_WIDTH
OFF_AV = 2 * A_WIDTH
OFF_MQK = 3 * A_WIDTH
OFF_MV = OFF_MQK + 2 * M_WIDTH
OFF_MO = OFF_MV + M_WIDTH
OFF_MG = OFF_MO + M_WIDTH
N_GATES = 2 * M_HEADS
D_IN_PAD = OFF_MG + LANES
GATE_ROWS = 16
ZC_MV = A_WIDTH
ZC_MO = ZC_MV + M_WIDTH
ZC_MG = ZC_MO + M_WIDTH
ZC_WIDTH = ZC_MG + LANES
NM_ROWS = 32

FFN_ROWS = 512
FFN_CHUNK = MXU_DIM
PROMPT_TILE = 256
STEP_ROWS = 128
ATTENTION_AHEAD = 4
PROMPT_AHEAD = 2
UNITS_BEFORE_CONV = 2
UNITS_BEFORE_GATES = 2
UNITS_PER_MLSTM_BLOCK = 3


def _rms(x, g):
    return x * lax.rsqrt(jnp.mean(x * x, axis=-1, keepdims=True) + EPS) * g


def _sigmoid(x):
    return 1.0 / (1.0 + jnp.exp(-x))


def _silu(x):
    h = 0.5 * x
    return h + h * jnp.tanh(h)


def _log_sigmoid(x):
    return jnp.minimum(x, 0.0) - jnp.log1p(jnp.exp(-jnp.abs(x)))


def _dot(a, b):
    return jnp.dot(a, b, preferred_element_type=F32)


def _dot_nt(a, b):
    return lax.dot_general(a, b, (((1,), (1,)), ((), ())), preferred_element_type=F32)


def _split3(x):
    p1 = x.astype(BF16)
    r1 = x - p1.astype(F32)
    p2 = r1.astype(BF16)
    p3 = (r1 - p2.astype(F32)).astype(BF16)
    return p1, p2, p3


def _resident(shape):
    nd = len(shape)
    return pl.BlockSpec(shape, lambda *_: (0,) * nd, pipeline_mode=pl.Buffered(1))


def _ffn_kernel(x_ref, g_ref, w1_ref, w3_ref, w2_ref, *rest, final_norm):
    if final_norm:
        gf_ref, o_ref, h_ref, u_ref = rest
    else:
        o_ref, h_ref, u_ref = rest
    h_ref[...] = _rms(x_ref[...], g_ref[...]).astype(BF16)
    for c in range(0, u_ref.shape[1], FFN_CHUNK):
        h = h_ref[...]
        a = _dot(h, w1_ref[:, c:c + FFN_CHUNK])
        b = _dot(h, w3_ref[:, c:c + FFN_CHUNK])
        u_ref[:, c:c + FFN_CHUNK] = (_silu(a) * b).astype(BF16)
    y = x_ref[...] + 0.5 * _dot(u_ref[...], w2_ref[...])
    if final_norm:
        y = _rms(y, gf_ref[...])
    o_ref[...] = y


def _ffn_weights(w1, w3, w2):
    return w1.astype(BF16), w3.astype(BF16), w2.astype(BF16)


def _ffn(x2d, g, weights, gf=None):
    n, d = x2d.shape
    w1, w3, w2 = weights
    f = w1.shape[1]
    assert f % FFN_CHUNK == 0
    rows = min(FFN_ROWS, n)
    final_norm = gf is not None
    row_spec = pl.BlockSpec((rows, d), lambda i: (i, 0))
    in_specs = [row_spec, _resident((1, d)), _resident(w1.shape), _resident(w3.shape), _resident(w2.shape)]
    args = [x2d, g.reshape(1, d), w1, w3, w2]
    if final_norm:
        in_specs.append(_resident((1, d)))
        args.append(gf.reshape(1, d))
    return pl.pallas_call(
        functools.partial(_ffn_kernel, final_norm=final_norm),
        grid=(n // rows,),
        in_specs=in_specs,
        out_specs=row_spec,
        out_shape=jax.ShapeDtypeStruct((n, d), F32),
        scratch_shapes=[pltpu.VMEM((rows, d), BF16), pltpu.VMEM((rows, f), BF16)],
        compiler_params=pltpu.CompilerParams(dimension_semantics=("arbitrary",),
                                             vmem_limit_bytes=VMEM_LIMIT_BYTES),
        name="ffn_final" if final_norm else "ffn",
    )(*args)


def _mixer_kernel(*refs, tile, group, chunk, band_rows, pos0, n_tiles, fused_ffn, has_state, cache_t, ahead):
    refs = iter(refs)
    take = lambda n: [next(refs) for _ in range(n)]
    (xa_ref,) = take(1)
    if fused_ffn:
        g1_ref, w1_ref, w3_ref, w2_ref = take(4)
    else:
        (xb_ref,) = take(1)
    gmix_ref, win_ref, wgt_ref, convw_ref, convb_ref, gbrow_ref, gbcol_ref, diag_ref, mask_ref, normm_ref, wout_ref = take(11)
    if has_state:
        k0_ref, v0_ref, c0_ref, n0_ref, m0_ref, conv0_ref = take(6)
    y_ref, ko_ref, vo_ref, co_ref, no_ref, mo_ref, convo_ref = take(7)
    z_new, z_cur, g_new, g_cur, kband, vband, ubuf, c_s, nm_s, mix_s, h_s, tab_s = take(12)
    n_rows, m_rows = slice(0, M_HEADS), slice(SUBLANES, SUBLANES + M_HEADS)
    if fused_ffn:
        u_s, h1_s, x1_new, x1_mid, x1_cur = take(5)
    if cache_t:
        kt_s, vt_s = take(2)
        assert has_state and n_tiles == 1 and pos0 >= ATT_BAND
    assert group == 1 or n_tiles == 1
    streams = range(group)
    span = lambda g: slice(g * tile, (g + 1) * tile)

    p = pl.program_id(0)
    lag = 2 if fused_ffn else 1
    t = lax.rem(jnp.maximum(p - lag, 0), n_tiles)

    @pl.when(p == 0)
    def _first_step():
        z_cur[...] = jnp.zeros_like(z_cur)
        g_cur[...] = jnp.zeros_like(g_cur)
        for g in streams:
            kband[g, ATT_BAND:band_rows, :] = jnp.zeros((band_rows - ATT_BAND, A_WIDTH), BF16)
            vband[g, ATT_BAND:band_rows, :] = jnp.zeros((band_rows - ATT_BAND, A_WIDTH), BF16)
            ubuf[g, SUBLANES:SUBLANES + tile, :] = jnp.zeros((tile, 2 * M_WIDTH), F32)
        if fused_ffn:
            x1_cur[...] = jnp.zeros_like(x1_cur)
            x1_mid[...] = jnp.zeros_like(x1_mid)

    early_tiles = -(-(ATT_BAND - pos0) // tile) if pos0 < ATT_BAND else 0

    @pl.when(p == 0 if early_tiles == 0 else t <= early_tiles)
    def _bias_table():
        width = diag_ref.shape[1]
        if early_tiles:
            col = lax.broadcasted_iota(jnp.int32, (1, band_rows), 1)
            in_stream = col >= (ATT_BAND - pos0) - t * tile
        for head in range(A_HEADS):
            rows_of_diag = jnp.broadcast_to(diag_ref[head:head + 1, :], (tile, width))
            skew = pltpu.roll(rows_of_diag, width - (tile - 1), 1, stride=1, stride_axis=0)
            tab = skew[:, :band_rows] + mask_ref[...]
            tab_s[head] = jnp.where(in_stream, tab, NEG) if early_tiles else tab

    @pl.when(t == 0)
    def _load_state():
        if has_state:
            for g in streams:
                if cache_t:
                    kt_s[g] = k0_ref[g].astype(BF16)
                    vt_s[g] = v0_ref[g].astype(BF16)
                else:
                    kband[g, 0:ATT_BAND, :] = k0_ref[g].astype(BF16)
                    vband[g, 0:ATT_BAND, :] = v0_ref[g].astype(BF16)
                ubuf[g, 0:SUBLANES, :] = conv0_ref[g]
            c_s[...] = c0_ref[...]
            nm_s[:, n_rows, :] = n0_ref[...]
            nm_s[:, m_rows, :] = m0_ref[...]
        else:
            for g in streams:
                kband[g, 0:ATT_BAND, :] = jnp.zeros((ATT_BAND, A_WIDTH), BF16)
                vband[g, 0:ATT_BAND, :] = jnp.zeros((ATT_BAND, A_WIDTH), BF16)
                ubuf[g, 0:SUBLANES, :] = jnp.zeros((SUBLANES, 2 * M_WIDTH), F32)
            c_s[...] = jnp.zeros_like(c_s)
            nm_s[...] = jnp.zeros_like(nm_s)

    def stage_a1():
        h1_s[...] = _rms(xa_ref[...], g1_ref[...]).astype(BF16)
        yield
        for c in range(0, u_s.shape[1], FFN_CHUNK):
            hf = h1_s[...]
            a = _dot(hf, w1_ref[:, c:c + FFN_CHUNK])
            b = _dot(hf, w3_ref[:, c:c + FFN_CHUNK])
            u_s[:, c:c + FFN_CHUNK] = (_silu(a) * b).astype(BF16)
            yield
        for c in range(0, x1_new.shape[1], MXU_DIM):
            x1_new[:, c:c + MXU_DIM] = (xa_ref[:, c:c + MXU_DIM]
                                        + 0.5 * _dot(u_s[...], w2_ref[:, c:c + MXU_DIM]))
            yield

    def stage_a2():
        x_in = x1_mid[...] if fused_ffn else xa_ref[...]
        h_s[...] = _rms(x_in, gmix_ref[...]).astype(BF16)
        yield
        for c in range(0, D_IN_PAD, MXU_DIM):
            c1 = min(c + MXU_DIM, D_IN_PAD)
            z_new[:, c:c1] = _dot(h_s[...], win_ref[:, c:c1])
            projected_cols[0] = c1
            yield
        g_new[...] = _dot_nt(wgt_ref[...], h_s[...])

    projected_cols = [0]

    def alternate(*gens):
        gens = list(gens)
        while gens:
            for gen in list(gens):
                try:
                    next(gen)
                    yield
                except StopIteration:
                    gens.remove(gen)

    units = alternate(stage_a1(), stage_a2()) if fused_ffn else stage_a2()

    def project_next(count=1):
        for _ in range(count):
            next(units, None)

    lane = lax.broadcasted_iota(jnp.int32, (1, LANES), 1)
    even = lane < A_DH

    def scores(g, head):
        lo = (head // 2) * LANES
        mine = even if head % 2 == 0 else jnp.logical_not(even)
        qh = jnp.where(mine, z_cur[span(g), lo:lo + LANES] * (LOG2E * A_DH ** -0.5), 0.0).astype(BF16)
        if cache_t:
            s_old = _dot(qh, kt_s[g, lo:lo + LANES, :]) + tab_s[head, :, 0:ATT_BAND]
            s_new = _dot_nt(qh, kband[g, ATT_BAND:band_rows, lo:lo + LANES]) + tab_s[head, :, ATT_BAND:band_rows]
            return s_old, s_new
        return _dot_nt(qh, kband[g, :, lo:lo + LANES]) + tab_s[head]

    def attend(g, head, s):
        lo = (head // 2) * LANES
        mine = even if head % 2 == 0 else jnp.logical_not(even)
        if cache_t:
            s_old, s_new = s
            top = jnp.maximum(jnp.max(s_old, axis=-1, keepdims=True), jnp.max(s_new, axis=-1, keepdims=True))
            e_old = jnp.exp2(s_old - top)
            e_new = jnp.exp2(s_new - top)
            o = (_dot_nt(e_old.astype(BF16), vt_s[g, lo:lo + LANES, :])
                 + _dot(e_new.astype(BF16), vband[g, ATT_BAND:band_rows, lo:lo + LANES]))
            total = jnp.sum(e_old, axis=-1, keepdims=True) + jnp.sum(e_new, axis=-1, keepdims=True)
            o = jnp.where(mine, o, 0.0) * (1.0 / total)
        else:
            vp = vband[g, :, lo:lo + LANES]
            e = jnp.exp2(s - jnp.max(s, axis=-1, keepdims=True))
            o = _dot(e.astype(BF16), jnp.where(mine, vp, jnp.zeros_like(vp)))
            o = o * (1.0 / jnp.sum(e, axis=-1, keepdims=True))
        if head % 2 == 0:
            mix_s[span(g), lo:lo + LANES] = o
        else:
            mix_s[span(g), lo:lo + LANES] += o

    pending = []
    for g in streams:
        for head in range(A_HEADS):
            pending.append((g, head, scores(g, head)))
            project_next(lag)
            if len(pending) > ahead:
                attend(*pending.pop(0))
    for item in pending:
        attend(*item)
    assert projected_cols[0] >= OFF_MQK
    ko_ref[...] = z_new[:, OFF_AK:OFF_AV]
    vo_ref[...] = z_new[:, OFF_AV:OFF_MQK]
    for g in streams:
        if n_tiles > 1:
            kband[g, 0:ATT_BAND, :] = kband[g, tile:tile + ATT_BAND, :]
            vband[g, 0:ATT_BAND, :] = vband[g, tile:tile + ATT_BAND, :]
        kband[g, ATT_BAND:ATT_BAND + tile, :] = z_new[span(g), OFF_AK:OFF_AV].astype(BF16)
        vband[g, ATT_BAND:ATT_BAND + tile, :] = z_new[span(g), OFF_AV:OFF_MQK].astype(BF16)
    z_cur[:, 0:OFF_AK] = z_new[:, 0:OFF_AK]

    project_next(UNITS_BEFORE_CONV)
    mq, mk, last_rows = [], [], []
    for g in streams:
        frames = ubuf[g]
        qk = convb_ref[...] + frames[SUBLANES:, :] * convw_ref[CONV_W - 1:CONV_W, :]
        for j in range(CONV_W - 1):
            shifted = pltpu.roll(frames, CONV_W - 1 - j, 0)
            qk = qk + shifted[SUBLANES:, :] * convw_ref[j:j + 1, :]
        qk = _silu(qk)
        mq.append(qk[:, :M_WIDTH])
        mk.append(qk[:, M_WIDTH:] * (M_DH ** -0.5))
        last_rows.append(ubuf[g, tile:tile + SUBLANES, :])
        convo_ref[g] = last_rows[g]

    project_next(UNITS_BEFORE_GATES)
    rows = group * tile
    g_col = z_cur[:, ZC_MG:ZC_MG + LANES] + gbrow_ref[...]
    g_row = g_cur[...] + gbcol_ref[...]
    lf_col = _log_sigmoid(g_col)
    lf_row = _log_sigmoid(g_row)
    ri = lax.broadcasted_iota(jnp.int32, (rows, rows), 0)
    ci = lax.broadcasted_iota(jnp.int32, (rows, rows), 1)
    assert chunk & (chunk - 1) == 0
    chunk_shift = chunk.bit_length() - 1
    same_chunk = (ri >> chunk_shift) == (ci >> chunk_shift)
    tri = jnp.where(jnp.logical_and(same_chunk, ci <= ri), 1.0, 0.0).astype(BF16)
    tri_t = jnp.where(jnp.logical_and(same_chunk, ri <= ci), 1.0, 0.0).astype(BF16)
    b_col = sum(_dot(tri, part) for part in _split3(lf_col))
    b_row = sum(_dot(part, tri_t) for part in _split3(lf_row))

    causal = (lax.broadcasted_iota(jnp.int32, (chunk, chunk), 1)
              <= lax.broadcasted_iota(jnp.int32, (chunk, chunk), 0))
    assert chunk == tile
    first = {}

    def block_scores(g, hd):
        lo, hi = hd * M_DH, (hd + 1) * M_DH
        cmat = c_s[g, hd]
        m_prev = nm_s[g, SUBLANES + hd:SUBLANES + hd + 1, 0:1]
        bc = b_col[span(g), M_HEADS + hd:M_HEADS + hd + 1]
        br = b_row[M_HEADS + hd:M_HEADS + hd + 1, span(g)]
        igr = g_row[hd:hd + 1, span(g)]
        dmat = jnp.where(causal, bc + (igr - br), NEG)
        inter = bc + m_prev
        mt = jnp.maximum(inter, jnp.max(dmat, axis=-1, keepdims=True))
        q = mq[g][:, lo:hi]
        k = mk[g][:, lo:hi]
        vb = z_cur[span(g), ZC_MV + lo:ZC_MV + hi].astype(BF16)
        qb = q.astype(BF16)
        first[g, hd] = dict(bc=bc, mt=mt, inter=inter, m_prev=m_prev, cmat=cmat, q=q, k=k, vb=vb, dmat=dmat,
                            qk=_dot_nt(qb, k.astype(BF16)), qc=_dot(qb, cmat.astype(BF16)))

    def block_output(g, hd):
        lo, hi = hd * M_DH, (hd + 1) * M_DH
        f = first[g, hd]
        nrow = nm_s[g, hd:hd + 1, :]
        s = f["qk"] * jnp.exp(f["dmat"] - f["mt"])
        iw = jnp.exp(f["inter"] - f["mt"])
        num = iw * f["qc"] + _dot(s.astype(BF16), f["vb"])
        den = iw * jnp.sum(f["q"] * nrow, axis=-1, keepdims=True) + jnp.sum(s, axis=-1, keepdims=True)
        mh = num / jnp.maximum(jnp.abs(den), jnp.exp(-f["mt"]))
        mh = mh * _sigmoid(z_cur[span(g), ZC_MO + lo:ZC_MO + hi])
        mix_s[span(g), A_WIDTH + lo:A_WIDTH + hi] = _rms(mh, normm_ref[:, lo:hi])

    def block_state(g, hd):
        f = first[g, hd]
        bc, mt, m_prev = f["bc"], f["mt"], f["m_prev"]
        igc = g_col[span(g), hd:hd + 1]
        b_last = bc[tile - 1:tile, :]
        m_new = mt[tile - 1:tile, :]
        kw = f["k"] * jnp.exp(b_last - bc + igc - m_new)
        decay = jnp.exp(b_last + m_prev - m_new)
        c_s[g, hd] = decay * f["cmat"] + _dot(kw.T.astype(BF16), f["vb"])
        nm_s[g, hd:hd + 1, :] = decay * nm_s[g, hd:hd + 1, :] + jnp.sum(kw, axis=0, keepdims=True)
        nm_s[g, SUBLANES + hd:SUBLANES + hd + 1, :] = jnp.broadcast_to(m_new, (1, LANES))

    blocks = [(g, hd) for hd in range(M_HEADS) for g in streams]
    if group == 1:
        for blk in blocks:
            project_next(UNITS_PER_MLSTM_BLOCK)
            block_scores(*blk)
            block_output(*blk)
            block_state(*blk)
    else:
        for phase in (block_scores, block_output, block_state):
            for blk in blocks:
                project_next()
                phase(*blk)

    for _ in units:
        pass
    co_ref[...] = c_s[...]
    no_ref[...] = nm_s[:, n_rows, :]
    mo_ref[...] = nm_s[:, m_rows, :]
    resid = x1_cur[...] if fused_ffn else xb_ref[...]
    y_ref[...] = resid + _dot(mix_s[...].astype(BF16), wout_ref[...])

    if fused_ffn:
        x1_cur[...] = x1_mid[...]
        x1_mid[...] = x1_new[...]
    for g in streams:
        ubuf[g, 0:SUBLANES, :] = last_rows[g]
        ubuf[g, SUBLANES:SUBLANES + tile, :] = z_new[span(g), OFF_MQK:OFF_MV]
    z_cur[:, ZC_MV:ZC_WIDTH] = z_new[:, OFF_MV:D_IN_PAD]
    g_cur[...] = g_new[...]


def _bias_pieces(rel_bias, tile, band_rows, chunked):
    heads = rel_bias.shape[0]
    span = band_rows + tile - 1
    width = -(-span // LANES) * LANES
    n_far = ATT_BAND + tile - REL_CLIP
    n_near = width - n_far - (2 * REL_CLIP - 1)
    diag = jnp.concatenate([jnp.broadcast_to(rel_bias[:, 2 * REL_CLIP:], (heads, n_far)),
                            rel_bias[:, 2 * REL_CLIP - 1:0:-1],
                            jnp.broadcast_to(rel_bias[:, :1], (heads, n_near))], axis=1)
    i = np.arange(tile)[:, None]
    j = np.arange(band_rows)[None, :]
    visible = j < ATT_BAND + tile
    if chunked:
        qc = i // CHUNK
        kc = (j - ATT_BAND) // CHUNK
        visible = visible & (kc <= qc) & (kc >= qc - ATT_BAND // CHUNK)
    mask = jnp.asarray(np.where(visible, 0.0, NEG * LOG2E), F32)
    return diag * LOG2E, mask


def _mixer(x, params, state, *, tile, chunk, pos0, chunked, ffn=None, cache_t=False):
    nb, frames, d = x.shape
    gmix, w_in, conv_w, conv_b, gate_bias, rel_bias, norm_m, w_out = params
    n_tiles = frames // tile
    band_rows = ATT_BAND + -(-tile // LANES) * LANES
    keep_tiles = min(ATT_BAND, frames) // tile
    width2 = 2 * M_WIDTH
    fused_ffn = ffn is not None
    has_state = state is not None
    group = max(1, min(nb, STEP_ROWS // tile)) if n_tiles == 1 else 1
    assert nb % group == 0
    rows = group * tile

    win = jnp.concatenate([w_in, jnp.zeros((d, D_IN_PAD - w_in.shape[1]), F32)], axis=1).astype(BF16)
    wgt = jnp.zeros((GATE_ROWS, d), F32).at[:N_GATES].set(w_in[:, OFF_MG:].T).astype(BF16)
    gb_row = jnp.zeros((1, LANES), F32).at[0, :N_GATES].set(gate_bias)
    gb_col = jnp.zeros((GATE_ROWS, 1), F32).at[:N_GATES, 0].set(gate_bias)
    diag, mask = _bias_pieces(rel_bias, tile, band_rows, chunked)

    lag = 2 if fused_ffn else 1
    n_tiles_all = (nb // group) * n_tiles
    n_steps = n_tiles_all + lag
    entering = lambda p: jnp.minimum(p, n_tiles_all - 1)
    projected = lambda p: jnp.clip(p - (lag - 1), 0, n_tiles_all - 1)
    mixed = lambda p: jnp.maximum(p - lag, 0)
    per_group = lambda *dims: pl.BlockSpec((group,) + dims,
                                           lambda p: (lax.div(mixed(p), n_tiles),) + (0,) * len(dims))
    enter_spec = pl.BlockSpec((rows, d), lambda p: (entering(p), 0))
    mix_spec = pl.BlockSpec((rows, d), lambda p: (mixed(p), 0))

    def kv_index(p):
        q = projected(p)
        return (lax.div(q, n_tiles) * keep_tiles + jnp.maximum(lax.rem(q, n_tiles) - (n_tiles - keep_tiles), 0), 0)

    kv_spec = pl.BlockSpec((rows, A_WIDTH), kv_index)

    x2d = x.reshape(nb * frames, d)
    args, in_specs = [x2d], [enter_spec]
    scratch_ffn = []
    if fused_ffn:
        g1, w1, w3, w2 = ffn
        args += [g1.reshape(1, d), w1, w3, w2]
        in_specs += [_resident((1, d)), _resident(w1.shape), _resident(w3.shape), _resident(w2.shape)]
        scratch_ffn = [pltpu.VMEM((rows, w1.shape[1]), BF16), pltpu.VMEM((rows, d), BF16)] + [pltpu.VMEM((rows, d), F32)] * 3
    else:
        args.append(x2d)
        in_specs.append(mix_spec)
    args += [gmix.reshape(1, d), win, wgt, conv_w, conv_b.reshape(1, width2), gb_row, gb_col, diag, mask,
             norm_m.reshape(1, M_WIDTH), w_out.astype(BF16)]
    in_specs += [_resident((1, d)), _resident(win.shape), _resident(wgt.shape),
                 _resident((CONV_W, width2)), _resident((1, width2)), _resident((1, LANES)),
                 _resident((GATE_ROWS, 1)), _resident(diag.shape), _resident(mask.shape), _resident((1, M_WIDTH)),
                 _resident((d, d))]
    if has_state:
        k0, v0, c0, n0, m0, conv0 = state
        conv0p = jnp.concatenate([jnp.zeros((nb, SUBLANES - (CONV_W - 1), width2), F32), conv0], axis=1)
        m0p = jnp.broadcast_to(m0[:, :, None], (nb, M_HEADS, LANES))
        args += [k0, v0, c0, n0, m0p, conv0p]
        in_specs += [per_group(ATT_BAND, A_WIDTH), per_group(ATT_BAND, A_WIDTH),
                     per_group(M_HEADS, M_DH, M_DH), per_group(M_HEADS, M_DH), per_group(M_HEADS, LANES),
                     per_group(SUBLANES, width2)]
    out_specs = [
        mix_spec, kv_spec, kv_spec, per_group(M_HEADS, M_DH, M_DH), per_group(M_HEADS, M_DH),
        per_group(M_HEADS, LANES), per_group(SUBLANES, width2),
    ]
    keep = keep_tiles * tile
    out_shape = [
        jax.ShapeDtypeStruct((nb * frames, d), F32),
        jax.ShapeDtypeStruct((nb * keep, A_WIDTH), F32),
        jax.ShapeDtypeStruct((nb * keep, A_WIDTH), F32),
        jax.ShapeDtypeStruct((nb, M_HEADS, M_DH, M_DH), F32),
        jax.ShapeDtypeStruct((nb, M_HEADS, M_DH), F32),
        jax.ShapeDtypeStruct((nb, M_HEADS, LANES), F32),
        jax.ShapeDtypeStruct((nb, SUBLANES, width2), F32),
    ]
    scratch = [
        pltpu.VMEM((rows, D_IN_PAD), F32), pltpu.VMEM((rows, ZC_WIDTH), F32),
        pltpu.VMEM((GATE_ROWS, rows), F32), pltpu.VMEM((GATE_ROWS, rows), F32),
        pltpu.VMEM((group, band_rows, A_WIDTH), BF16), pltpu.VMEM((group, band_rows, A_WIDTH), BF16),
        pltpu.VMEM((group, tile + SUBLANES, width2), F32),
        pltpu.VMEM((group, M_HEADS, M_DH, M_DH), F32),
        pltpu.VMEM((group, NM_ROWS, LANES), F32),
        pltpu.VMEM((rows, d), F32), pltpu.VMEM((rows, d), BF16),
        pltpu.VMEM((A_HEADS, tile, band_rows), F32),
    ] + scratch_ffn
    if cache_t:
        scratch += [pltpu.VMEM((group, A_WIDTH, ATT_BAND), BF16)] * 2
    y, ko, vo, c1, n1, m1, conv1 = pl.pallas_call(
        functools.partial(_mixer_kernel, tile=tile, group=group, chunk=chunk, band_rows=band_rows, pos0=pos0,
                          n_tiles=n_tiles, fused_ffn=fused_ffn, has_state=has_state, cache_t=cache_t,
                          ahead=PROMPT_AHEAD if group == 1 else ATTENTION_AHEAD),
        grid=(n_steps,),
        in_specs=in_specs,
        out_specs=out_specs,
        out_shape=out_shape,
        scratch_shapes=scratch,
        compiler_params=pltpu.CompilerParams(dimension_semantics=("arbitrary",),
                                             vmem_limit_bytes=VMEM_LIMIT_BYTES),
        name="mixer_chunked" if chunked else "mixer_step",
    )(*args)
    new_state = (ko.reshape(nb, keep, A_HEADS, A_DH), vo.reshape(nb, keep, A_HEADS, A_DH), c1, n1,
                 m1[:, :, 0], conv1[:, SUBLANES - (CONV_W - 1):, :])
    return y.reshape(nb, frames, d), new_state


def kernel(x_prompt, x_sample, cache_attn_k, cache_attn_v, state_mlstm_C, state_mlstm_n, state_mlstm_m, state_mlstm_conv, norm_ffn1, w1_ffn1, w3_ffn1, w2_ffn1, norm_mix, w_in, conv_w, conv_b, gate_bias, rel_bias, norm_mlstm_out, w_out, norm_ffn2, w1_ffn2, w3_ffn2, w2_ffn2, norm_final):
    depth = norm_ffn1.shape[0]
    nbp, seq, d = x_prompt.shape
    nbs, dec, _ = x_sample.shape
    xp = x_prompt
    xs = x_sample.reshape(nbs * dec, d)
    prompt_tile = min(PROMPT_TILE, seq)
    new_p, new_s = [], []
    for l in range(depth):
        last = l == depth - 1
        ffn1 = _ffn_weights(w1_ffn1[l], w3_ffn1[l], w2_ffn1[l])
        ffn2 = _ffn_weights(w1_ffn2[l], w3_ffn2[l], w2_ffn2[l])
        mix = (norm_mix[l], w_in[l], conv_w[l], conv_b[l], gate_bias[l], rel_bias[l], norm_mlstm_out[l], w_out[l])
        gf = norm_final if last else None

        xp, st = _mixer(xp, mix, None, tile=prompt_tile, chunk=prompt_tile, pos0=0, chunked=True,
                        ffn=(norm_ffn1[l],) + ffn1)
        new_p.append(st)
        xp = _ffn(xp.reshape(nbp * seq, d), norm_ffn2[l], ffn2, gf).reshape(nbp, seq, d)

        xs = _ffn(xs, norm_ffn1[l], ffn1)
        feature_major = lambda c: c.transpose(0, 2, 3, 1).reshape(nbs, A_WIDTH, -1)
        cache = (feature_major(cache_attn_k[l]), feature_major(cache_attn_v[l]),
                 state_mlstm_C[l], state_mlstm_n[l], state_mlstm_m[l], state_mlstm_conv[l])
        xs3, st = _mixer(xs.reshape(nbs, dec, d), mix, cache, tile=dec, chunk=dec, pos0=PAST_LEN, chunked=False,
                         cache_t=True)
        new_s.append(st)
        xs = _ffn(xs3.reshape(nbs * dec, d), norm_ffn2[l], ffn2, gf)

    stack = lambda states, i: jnp.stack([s[i] for s in states])
    return ((xp, xs.reshape(nbs, dec, d))
            + tuple(stack(new_p, i) for i in range(6)) + tuple(stack(new_s, i) for i in range(6)))
```

```python
import functools

import numpy as np
import jax
import jax.numpy as jnp
from jax import lax
from jax.experimental import pallas as pl
from jax.experimental.pallas import tpu as pltpu

F32 = jnp.float32
BF16 = jnp.bfloat16

CHUNK = 64
ATT_BAND = 8 * CHUNK
A_HEADS = 8
A_DH = 64
A_WIDTH = A_HEADS * A_DH
M_HEADS = 4
M_DH = 128
M_WIDTH = M_HEADS * M_DH
REL_CLIP = 128
CONV_W = 4
PAST_LEN = 4096
EPS = 1e-6
NEG = -1e30
LOG2E = 1.4426950408889634

LANES = 128
SUBLANES = 8
MXU_DIM = 256
VMEM_LIMIT_BYTES = 60 * 1024 * 1024

OFF_AK = A_WIDTH
OFF_AV = 2 * A_WIDTH
OFF_MQK = 3 * A_WIDTH
OFF_MV = OFF_MQK + 2 * M_WIDTH
OFF_MO = OFF_MV + M_WIDTH
OFF_MG = OFF_MO + M_WIDTH
N_GATES = 2 * M_HEADS
D_IN_PAD = OFF_MG + LANES
GATE_ROWS = 16
ZC_MV = A_WIDTH
ZC_MO = ZC_MV + M_WIDTH
ZC_MG = ZC_MO + M_WIDTH
ZC_WIDTH = ZC_MG + LANES
NM_ROWS = 32

FFN_ROWS = 512
FFN_CHUNK = MXU_DIM
PROMPT_TILE = 256
STEP_ROWS = 128
ATTENTION_AHEAD = 4
PROMPT_AHEAD = 2
UNITS_BEFORE_CONV = 2
UNITS_BEFORE_GATES = 2
UNITS_PER_MLSTM_BLOCK = 3


def _rms(x, g):
    return x * lax.rsqrt(jnp.mean(x * x, axis=-1, keepdims=True) + EPS) * g


def _sigmoid(x):
    return 1.0 / (1.0 + jnp.exp(-x))


def _silu(x):
    h = 0.5 * x
    return h + h * jnp.tanh(h)


def _log_sigmoid(x):
    return jnp.minimum(x, 0.0) - jnp.log1p(jnp.exp(-jnp.abs(x)))


def _dot(a, b):
    return jnp.dot(a, b, preferred_element_type=F32)


def _dot_nt(a, b):
    return lax.dot_general(a, b, (((1,), (1,)), ((), ())), preferred_element_type=F32)


def _split3(x):
    p1 = x.astype(BF16)
    r1 = x - p1.astype(F32)
    p2 = r1.astype(BF16)
    p3 = (r1 - p2.astype(F32)).astype(BF16)
    return p1, p2, p3


def _resident(shape):
    nd = len(shape)
    return pl.BlockSpec(shape, lambda *_: (0,) * nd, pipeline_mode=pl.Buffered(1))


def _ffn_kernel(x_ref, g_ref, w1_ref, w3_ref, w2_ref, *rest, final_norm):
    if final_norm:
        gf_ref, o_ref, h_ref, u_ref = rest
    else:
        o_ref, h_ref, u_ref = rest
    h_ref[...] = _rms(x_ref[...], g_ref[...]).astype(BF16)
    for c in range(0, u_ref.shape[1], FFN_CHUNK):
        h = h_ref[...]
        a = _dot(h, w1_ref[:, c:c + FFN_CHUNK])
        b = _dot(h, w3_ref[:, c:c + FFN_CHUNK])
        u_ref[:, c:c + FFN_CHUNK] = (_silu(a) * b).astype(BF16)
    y = x_ref[...] + 0.5 * _dot(u_ref[...], w2_ref[...])
    if final_norm:
        y = _rms(y, gf_ref[...])
    o_ref[...] = y


def _ffn_weights(w1, w3, w2):
    return w1.astype(BF16), w3.astype(BF16), w2.astype(BF16)


def _ffn(x2d, g, weights, gf=None):
    n, d = x2d.shape
    w1, w3, w2 = weights
    f = w1.shape[1]
    assert f % FFN_CHUNK == 0
    rows = min(FFN_ROWS, n)
    final_norm = gf is not None
    row_spec = pl.BlockSpec((rows, d), lambda i: (i, 0))
    in_specs = [row_spec, _resident((1, d)), _resident(w1.shape), _resident(w3.shape), _resident(w2.shape)]
    args = [x2d, g.reshape(1, d), w1, w3, w2]
    if final_norm:
        in_specs.append(_resident((1, d)))
        args.append(gf.reshape(1, d))
    return pl.pallas_call(
        functools.partial(_ffn_kernel, final_norm=final_norm),
        grid=(n // rows,),
        in_specs=in_specs,
        out_specs=row_spec,
        out_shape=jax.ShapeDtypeStruct((n, d), F32),
        scratch_shapes=[pltpu.VMEM((rows, d), BF16), pltpu.VMEM((rows, f), BF16)],
        compiler_params=pltpu.CompilerParams(dimension_semantics=("arbitrary",),
                                             vmem_limit_bytes=VMEM_LIMIT_BYTES),
        name="ffn_final" if final_norm else "ffn",
    )(*args)


def _mixer_kernel(*refs, tile, group, chunk, band_rows, pos0, n_tiles, fused_ffn, has_state, cache_t, ahead):
    refs = iter(refs)
    take = lambda n: [next(refs) for _ in range(n)]
    (xa_ref,) = take(1)
    if fused_ffn:
        g1_ref, w1_ref, w3_ref, w2_ref = take(4)
    else:
        (xb_ref,) = take(1)
    gmix_ref, win_ref, wgt_ref, convw_ref, convb_ref, gbrow_ref, gbcol_ref, diag_ref, mask_ref, normm_ref, wout_ref = take(11)
    if has_state:
        k0_ref, v0_ref, c0_ref, n0_ref, m0_ref, conv0_ref = take(6)
    y_ref, ko_ref, vo_ref, co_ref, no_ref, mo_ref, convo_ref = take(7)
    z_new, z_cur, g_new, g_cur, kband, vband, ubuf, c_s, nm_s, mix_s, h_s, tab_s = take(12)
    n_rows, m_rows = slice(0, M_HEADS), slice(SUBLANES, SUBLANES + M_HEADS)
    if fused_ffn:
        u_s, h1_s, x1_new, x1_mid, x1_cur = take(5)
    if cache_t:
        kt_s, vt_s = take(2)
        assert has_state and n_tiles == 1 and pos0 >= ATT_BAND
    assert group == 1 or n_tiles == 1
    streams = range(group)
    span = lambda g: slice(g * tile, (g + 1) * tile)

    p = pl.program_id(0)
    lag = 2 if fused_ffn else 1
    t = lax.rem(jnp.maximum(p - lag, 0), n_tiles)

    @pl.when(p == 0)
    def _first_step():
        z_cur[...] = jnp.zeros_like(z_cur)
        g_cur[...] = jnp.zeros_like(g_cur)
        for g in streams:
            kband[g, ATT_BAND:band_rows, :] = jnp.zeros((band_rows - ATT_BAND, A_WIDTH), BF16)
            vband[g, ATT_BAND:band_rows, :] = jnp.zeros((band_rows - ATT_BAND, A_WIDTH), BF16)
            ubuf[g, SUBLANES:SUBLANES + tile, :] = jnp.zeros((tile, 2 * M_WIDTH), F32)
        if fused_ffn:
            x1_cur[...] = jnp.zeros_like(x1_cur)
            x1_mid[...] = jnp.zeros_like(x1_mid)
        width = diag_ref.shape[1]
        for head in range(A_HEADS):
            rows_of_diag = jnp.broadcast_to(diag_ref[head:head + 1, :], (tile, width))
            skew = pltpu.roll(rows_of_diag, width - (tile - 1), 1, stride=1, stride_axis=0)
            tab_s[head] = skew[:, :band_rows] + mask_ref[...]

    @pl.when(t == 0)
    def _load_state():
        if has_state:
            for g in streams:
                if cache_t:
                    kt_s[g] = k0_ref[g].astype(BF16)
                    vt_s[g] = v0_ref[g].astype(BF16)
                else:
                    kband[g, 0:ATT_BAND, :] = k0_ref[g].astype(BF16)
                    vband[g, 0:ATT_BAND, :] = v0_ref[g].astype(BF16)
                ubuf[g, 0:SUBLANES, :] = conv0_ref[g]
            c_s[...] = c0_ref[...]
            nm_s[:, n_rows, :] = n0_ref[...]
            nm_s[:, m_rows, :] = m0_ref[...]
        else:
            for g in streams:
                kband[g, 0:ATT_BAND, :] = jnp.zeros((ATT_BAND, A_WIDTH), BF16)
                vband[g, 0:ATT_BAND, :] = jnp.zeros((ATT_BAND, A_WIDTH), BF16)
                ubuf[g, 0:SUBLANES, :] = jnp.zeros((SUBLANES, 2 * M_WIDTH), F32)
            c_s[...] = jnp.zeros_like(c_s)
            nm_s[...] = jnp.zeros_like(nm_s)

    def stage_a1():
        h1_s[...] = _rms(xa_ref[...], g1_ref[...]).astype(BF16)
        yield
        for c in range(0, u_s.shape[1], FFN_CHUNK):
            hf = h1_s[...]
            a = _dot(hf, w1_ref[:, c:c + FFN_CHUNK])
            b = _dot(hf, w3_ref[:, c:c + FFN_CHUNK])
            u_s[:, c:c + FFN_CHUNK] = (_silu(a) * b).astype(BF16)
            yield
        for c in range(0, x1_new.shape[1], MXU_DIM):
            x1_new[:, c:c + MXU_DIM] = (xa_ref[:, c:c + MXU_DIM]
                                        + 0.5 * _dot(u_s[...], w2_ref[:, c:c + MXU_DIM]))
            yield

    def stage_a2():
        x_in = x1_mid[...] if fused_ffn else xa_ref[...]
        h_s[...] = _rms(x_in, gmix_ref[...]).astype(BF16)
        yield
        for c in range(0, D_IN_PAD, MXU_DIM):
            c1 = min(c + MXU_DIM, D_IN_PAD)
            z_new[:, c:c1] = _dot(h_s[...], win_ref[:, c:c1])
            projected_cols[0] = c1
            yield
        g_new[...] = _dot_nt(wgt_ref[...], h_s[...])

    projected_cols = [0]

    def alternate(*gens):
        gens = list(gens)
        while gens:
            for gen in list(gens):
                try:
                    next(gen)
                    yield
                except StopIteration:
                    gens.remove(gen)

    units = alternate(stage_a1(), stage_a2()) if fused_ffn else stage_a2()

    def project_next(count=1):
        for _ in range(count):
            next(units, None)

    lane = lax.broadcasted_iota(jnp.int32, (1, LANES), 1)
    even = lane < A_DH
    if pos0 < ATT_BAND:
        col = lax.broadcasted_iota(jnp.int32, (1, band_rows), 1)
        in_stream = col >= (ATT_BAND - pos0) - t * tile

    def scores(g, head):
        lo = (head // 2) * LANES
        mine = even if head % 2 == 0 else jnp.logical_not(even)
        qh = jnp.where(mine, z_cur[span(g), lo:lo + LANES] * (LOG2E * A_DH ** -0.5), 0.0).astype(BF16)
        if cache_t:
            s_old = _dot(qh, kt_s[g, lo:lo + LANES, :]) + tab_s[head, :, 0:ATT_BAND]
            s_new = _dot_nt(qh, kband[g, ATT_BAND:band_rows, lo:lo + LANES]) + tab_s[head, :, ATT_BAND:band_rows]
            return s_old, s_new
        s = _dot_nt(qh, kband[g, :, lo:lo + LANES]) + tab_s[head]
        if pos0 < ATT_BAND:
            s = jnp.where(in_stream, s, NEG)
        return s

    def attend(g, head, s):
        lo = (head // 2) * LANES
        mine = even if head % 2 == 0 else jnp.logical_not(even)
        if cache_t:
            s_old, s_new = s
            top = jnp.maximum(jnp.max(s_old, axis=-1, keepdims=True), jnp.max(s_new, axis=-1, keepdims=True))
            e_old = jnp.exp2(s_old - top)
            e_new = jnp.exp2(s_new - top)
            o = (_dot_nt(e_old.astype(BF16), vt_s[g, lo:lo + LANES, :])
                 + _dot(e_new.astype(BF16), vband[g, ATT_BAND:band_rows, lo:lo + LANES]))
            total = jnp.sum(e_old, axis=-1, keepdims=True) + jnp.sum(e_new, axis=-1, keepdims=True)
            o = jnp.where(mine, o, 0.0) * (1.0 / total)
        else:
            vp = vband[g, :, lo:lo + LANES]
            e = jnp.exp2(s - jnp.max(s, axis=-1, keepdims=True))
            o = _dot(e.astype(BF16), jnp.where(mine, vp, jnp.zeros_like(vp)))
            o = o * (1.0 / jnp.sum(e, axis=-1, keepdims=True))
        if head % 2 == 0:
            mix_s[span(g), lo:lo + LANES] = o
        else:
            mix_s[span(g), lo:lo + LANES] += o

    pending = []
    for g in streams:
        for head in range(A_HEADS):
            pending.append((g, head, scores(g, head)))
            project_next(lag)
            if len(pending) > ahead:
                attend(*pending.pop(0))
    for item in pending:
        attend(*item)
    assert projected_cols[0] >= OFF_MQK
    ko_ref[...] = z_new[:, OFF_AK:OFF_AV]
    vo_ref[...] = z_new[:, OFF_AV:OFF_MQK]
    for g in streams:
        if n_tiles > 1:
            kband[g, 0:ATT_BAND, :] = kband[g, tile:tile + ATT_BAND, :]
            vband[g, 0:ATT_BAND, :] = vband[g, tile:tile + ATT_BAND, :]
        kband[g, ATT_BAND:ATT_BAND + tile, :] = z_new[span(g), OFF_AK:OFF_AV].astype(BF16)
        vband[g, ATT_BAND:ATT_BAND + tile, :] = z_new[span(g), OFF_AV:OFF_MQK].astype(BF16)
    z_cur[:, 0:OFF_AK] = z_new[:, 0:OFF_AK]

    project_next(UNITS_BEFORE_CONV)
    mq, mk, last_rows = [], [], []
    for g in streams:
        frames = ubuf[g]
        qk = convb_ref[...] + frames[SUBLANES:, :] * convw_ref[CONV_W - 1:CONV_W, :]
        for j in range(CONV_W - 1):
            shifted = pltpu.roll(frames, CONV_W - 1 - j, 0)
            qk = qk + shifted[SUBLANES:, :] * convw_ref[j:j + 1, :]
        qk = _silu(qk)
        mq.append(qk[:, :M_WIDTH])
        mk.append(qk[:, M_WIDTH:] * (M_DH ** -0.5))
        last_rows.append(ubuf[g, tile:tile + SUBLANES, :])
        convo_ref[g] = last_rows[g]

    project_next(UNITS_BEFORE_GATES)
    rows = group * tile
    g_col = z_cur[:, ZC_MG:ZC_MG + LANES] + gbrow_ref[...]
    g_row = g_cur[...] + gbcol_ref[...]
    lf_col = _log_sigmoid(g_col)
    lf_row = _log_sigmoid(g_row)
    ri = lax.broadcasted_iota(jnp.int32, (rows, rows), 0)
    ci = lax.broadcasted_iota(jnp.int32, (rows, rows), 1)
    assert chunk & (chunk - 1) == 0
    chunk_shift = chunk.bit_length() - 1
    same_chunk = (ri >> chunk_shift) == (ci >> chunk_shift)
    tri = jnp.where(jnp.logical_and(same_chunk, ci <= ri), 1.0, 0.0).astype(BF16)
    tri_t = jnp.where(jnp.logical_and(same_chunk, ri <= ci), 1.0, 0.0).astype(BF16)
    b_col = sum(_dot(tri, part) for part in _split3(lf_col))
    b_row = sum(_dot(part, tri_t) for part in _split3(lf_row))

    causal = (lax.broadcasted_iota(jnp.int32, (chunk, chunk), 1)
              <= lax.broadcasted_iota(jnp.int32, (chunk, chunk), 0))
    assert chunk == tile
    first = {}

    def block_scores(g, hd):
        lo, hi = hd * M_DH, (hd + 1) * M_DH
        cmat = c_s[g, hd]
        m_prev = nm_s[g, SUBLANES + hd:SUBLANES + hd + 1, 0:1]
        bc = b_col[span(g), M_HEADS + hd:M_HEADS + hd + 1]
        br = b_row[M_HEADS + hd:M_HEADS + hd + 1, span(g)]
        igr = g_row[hd:hd + 1, span(g)]
        dmat = jnp.where(causal, bc + (igr - br), NEG)
        inter = bc + m_prev
        mt = jnp.maximum(inter, jnp.max(dmat, axis=-1, keepdims=True))
        q = mq[g][:, lo:hi]
        k = mk[g][:, lo:hi]
        vb = z_cur[span(g), ZC_MV + lo:ZC_MV + hi].astype(BF16)
        qb = q.astype(BF16)
        n_cols = jnp.broadcast_to(nm_s[g, hd:hd + 1, :], (M_DH, M_DH)).T
        c_and_n = jnp.concatenate([cmat, n_cols], axis=1).astype(BF16)
        first[g, hd] = dict(bc=bc, mt=mt, inter=inter, m_prev=m_prev, cmat=cmat, k=k, vb=vb, dmat=dmat,
                            qk=_dot_nt(qb, k.astype(BF16)), qcn=_dot(qb, c_and_n))

    def block_output(g, hd):
        lo, hi = hd * M_DH, (hd + 1) * M_DH
        f = first[g, hd]
        s = f["qk"] * jnp.exp(f["dmat"] - f["mt"])
        iw = jnp.exp(f["inter"] - f["mt"])
        v_and_ones = jnp.concatenate([f["vb"], jnp.ones((tile, M_DH), BF16)], axis=1)
        both = iw * f["qcn"] + _dot(s.astype(BF16), v_and_ones)
        num, den = both[:, :M_DH], both[:, M_DH:]
        mh = num / jnp.maximum(jnp.abs(den), jnp.exp(-f["mt"]))
        mh = mh * _sigmoid(z_cur[span(g), ZC_MO + lo:ZC_MO + hi])
        mix_s[span(g), A_WIDTH + lo:A_WIDTH + hi] = _rms(mh, normm_ref[:, lo:hi])

    def block_state(g, hd):
        f = first[g, hd]
        bc, mt, m_prev = f["bc"], f["mt"], f["m_prev"]
        igc = g_col[span(g), hd:hd + 1]
        b_last = bc[tile - 1:tile, :]
        m_new = mt[tile - 1:tile, :]
        kw = f["k"] * jnp.exp(b_last - bc + igc - m_new)
        decay = jnp.exp(b_last + m_prev - m_new)
        c_s[g, hd] = decay * f["cmat"] + _dot(kw.T.astype(BF16), f["vb"])
        nm_s[g, hd:hd + 1, :] = decay * nm_s[g, hd:hd + 1, :] + jnp.sum(kw, axis=0, keepdims=True)
        nm_s[g, SUBLANES + hd:SUBLANES + hd + 1, :] = jnp.broadcast_to(m_new, (1, LANES))

    blocks = [(g, hd) for hd in range(M_HEADS) for g in streams]
    if group == 1:
        for blk in blocks:
            project_next(UNITS_PER_MLSTM_BLOCK)
            block_scores(*blk)
            block_output(*blk)
            block_state(*blk)
    else:
        for phase in (block_scores, block_output, block_state):
            for blk in blocks:
                project_next()
                phase(*blk)

    for _ in units:
        pass
    co_ref[...] = c_s[...]
    no_ref[...] = nm_s[:, n_rows, :]
    mo_ref[...] = nm_s[:, m_rows, :]
    resid = x1_cur[...] if fused_ffn else xb_ref[...]
    y_ref[...] = resid + _dot(mix_s[...].astype(BF16), wout_ref[...])

    if fused_ffn:
        x1_cur[...] = x1_mid[...]
        x1_mid[...] = x1_new[...]
    for g in streams:
        ubuf[g, 0:SUBLANES, :] = last_rows[g]
        ubuf[g, SUBLANES:SUBLANES + tile, :] = z_new[span(g), OFF_MQK:OFF_MV]
    z_cur[:, ZC_MV:ZC_WIDTH] = z_new[:, OFF_MV:D_IN_PAD]
    g_cur[...] = g_new[...]


def _bias_pieces(rel_bias, tile, band_rows, chunked):
    heads = rel_bias.shape[0]
    span = band_rows + tile - 1
    width = -(-span // LANES) * LANES
    n_far = ATT_BAND + tile - REL_CLIP
    n_near = width - n_far - (2 * REL_CLIP - 1)
    diag = jnp.concatenate([jnp.broadcast_to(rel_bias[:, 2 * REL_CLIP:], (heads, n_far)),
                            rel_bias[:, 2 * REL_CLIP - 1:0:-1],
                            jnp.broadcast_to(rel_bias[:, :1], (heads, n_near))], axis=1)
    i = np.arange(tile)[:, None]
    j = np.arange(band_rows)[None, :]
    visible = j < ATT_BAND + tile
    if chunked:
        qc = i // CHUNK
        kc = (j - ATT_BAND) // CHUNK
        visible = visible & (kc <= qc) & (kc >= qc - ATT_BAND // CHUNK)
    mask = jnp.asarray(np.where(visible, 0.0, NEG * LOG2E), F32)
    return diag * LOG2E, mask


def _mixer(x, params, state, *, tile, chunk, pos0, chunked, ffn=None, cache_t=False):
    nb, frames, d = x.shape
    gmix, w_in, conv_w, conv_b, gate_bias, rel_bias, norm_m, w_out = params
    n_tiles = frames // tile
    band_rows = ATT_BAND + -(-tile // LANES) * LANES
    keep_tiles = min(ATT_BAND, frames) // tile
    width2 = 2 * M_WIDTH
    fused_ffn = ffn is not None
    has_state = state is not None
    group = max(1, min(nb, STEP_ROWS // tile)) if n_tiles == 1 else 1
    assert nb % group == 0
    rows = group * tile

    win = jnp.concatenate([w_in, jnp.zeros((d, D_IN_PAD - w_in.shape[1]), F32)], axis=1).astype(BF16)
    wgt = jnp.zeros((GATE_ROWS, d), F32).at[:N_GATES].set(w_in[:, OFF_MG:].T).astype(BF16)
    gb_row = jnp.zeros((1, LANES), F32).at[0, :N_GATES].set(gate_bias)
    gb_col = jnp.zeros((GATE_ROWS, 1), F32).at[:N_GATES, 0].set(gate_bias)
    diag, mask = _bias_pieces(rel_bias, tile, band_rows, chunked)

    lag = 2 if fused_ffn else 1
    n_tiles_all = (nb // group) * n_tiles
    n_steps = n_tiles_all + lag
    entering = lambda p: jnp.minimum(p, n_tiles_all - 1)
    projected = lambda p: jnp.clip(p - (lag - 1), 0, n_tiles_all - 1)
    mixed = lambda p: jnp.maximum(p - lag, 0)
    per_group = lambda *dims: pl.BlockSpec((group,) + dims,
                                           lambda p: (lax.div(mixed(p), n_tiles),) + (0,) * len(dims))
    enter_spec = pl.BlockSpec((rows, d), lambda p: (entering(p), 0))
    mix_spec = pl.BlockSpec((rows, d), lambda p: (mixed(p), 0))

    def kv_index(p):
        q = projected(p)
        return (lax.div(q, n_tiles) * keep_tiles + jnp.maximum(lax.rem(q, n_tiles) - (n_tiles - keep_tiles), 0), 0)

    kv_spec = pl.BlockSpec((rows, A_WIDTH), kv_index)

    x2d = x.reshape(nb * frames, d)
    args, in_specs = [x2d], [enter_spec]
    scratch_ffn = []
    if fused_ffn:
        g1, w1, w3, w2 = ffn
        args += [g1.reshape(1, d), w1, w3, w2]
        in_specs += [_resident((1, d)), _resident(w1.shape), _resident(w3.shape), _resident(w2.shape)]
        scratch_ffn = [pltpu.VMEM((rows, w1.shape[1]), BF16), pltpu.VMEM((rows, d), BF16)] + [pltpu.VMEM((rows, d), F32)] * 3
    else:
        args.append(x2d)
        in_specs.append(mix_spec)
    args += [gmix.reshape(1, d), win, wgt, conv_w, conv_b.reshape(1, width2), gb_row, gb_col, diag, mask,
             norm_m.reshape(1, M_WIDTH), w_out.astype(BF16)]
    in_specs += [_resident((1, d)), _resident(win.shape), _resident(wgt.shape),
                 _resident((CONV_W, width2)), _resident((1, width2)), _resident((1, LANES)),
                 _resident((GATE_ROWS, 1)), _resident(diag.shape), _resident(mask.shape), _resident((1, M_WIDTH)),
                 _resident((d, d))]
    if has_state:
        k0, v0, c0, n0, m0, conv0 = state
        conv0p = jnp.concatenate([jnp.zeros((nb, SUBLANES - (CONV_W - 1), width2), F32), conv0], axis=1)
        m0p = jnp.broadcast_to(m0[:, :, None], (nb, M_HEADS, LANES))
        args += [k0, v0, c0, n0, m0p, conv0p]
        in_specs += [per_group(ATT_BAND, A_WIDTH), per_group(ATT_BAND, A_WIDTH),
                     per_group(M_HEADS, M_DH, M_DH), per_group(M_HEADS, M_DH), per_group(M_HEADS, LANES),
                     per_group(SUBLANES, width2)]
    out_specs = [
        mix_spec, kv_spec, kv_spec, per_group(M_HEADS, M_DH, M_DH), per_group(M_HEADS, M_DH),
        per_group(M_HEADS, LANES), per_group(SUBLANES, width2),
    ]
    keep = keep_tiles * tile
    out_shape = [
        jax.ShapeDtypeStruct((nb * frames, d), F32),
        jax.ShapeDtypeStruct((nb * keep, A_WIDTH), F32),
        jax.ShapeDtypeStruct((nb * keep, A_WIDTH), F32),
        jax.ShapeDtypeStruct((nb, M_HEADS, M_DH, M_DH), F32),
        jax.ShapeDtypeStruct((nb, M_HEADS, M_DH), F32),
        jax.ShapeDtypeStruct((nb, M_HEADS, LANES), F32),
        jax.ShapeDtypeStruct((nb, SUBLANES, width2), F32),
    ]
    scratch = [
        pltpu.VMEM((rows, D_IN_PAD), F32), pltpu.VMEM((rows, ZC_WIDTH), F32),
        pltpu.VMEM((GATE_ROWS, rows), F32), pltpu.VMEM((GATE_ROWS, rows), F32),
        pltpu.VMEM((group, band_rows, A_WIDTH), BF16), pltpu.VMEM((group, band_rows, A_WIDTH), BF16),
        pltpu.VMEM((group, tile + SUBLANES, width2), F32),
        pltpu.VMEM((group, M_HEADS, M_DH, M_DH), F32),
        pltpu.VMEM((group, NM_ROWS, LANES), F32),
        pltpu.VMEM((rows, d), F32), pltpu.VMEM((rows, d), BF16),
        pltpu.VMEM((A_HEADS, tile, band_rows), F32),
    ] + scratch_ffn
    if cache_t:
        scratch += [pltpu.VMEM((group, A_WIDTH, ATT_BAND), BF16)] * 2
    y, ko, vo, c1, n1, m1, conv1 = pl.pallas_call(
        functools.partial(_mixer_kernel, tile=tile, group=group, chunk=chunk, band_rows=band_rows, pos0=pos0,
                          n_tiles=n_tiles, fused_ffn=fused_ffn, has_state=has_state, cache_t=cache_t,
                          ahead=PROMPT_AHEAD if group == 1 else ATTENTION_AHEAD),
        grid=(n_steps,),
        in_specs=in_specs,
        out_specs=out_specs,
        out_shape=out_shape,
        scratch_shapes=scratch,
        compiler_params=pltpu.CompilerParams(dimension_semantics=("arbitrary",),
                                             vmem_limit_bytes=VMEM_LIMIT_BYTES),
        name="mixer_chunked" if chunked else "mixer_step",
    )(*args)
    new_state = (ko.reshape(nb, keep, A_HEADS, A_DH), vo.reshape(nb, keep, A_HEADS, A_DH), c1, n1,
                 m1[:, :, 0], conv1[:, SUBLANES - (CONV_W - 1):, :])
    return y.reshape(nb, frames, d), new_state


def kernel(x_prompt, x_sample, cache_attn_k, cache_attn_v, state_mlstm_C, state_mlstm_n, state_mlstm_m, state_mlstm_conv, norm_ffn1, w1_ffn1, w3_ffn1, w2_ffn1, norm_mix, w_in, conv_w, conv_b, gate_bias, rel_bias, norm_mlstm_out, w_out, norm_ffn2, w1_ffn2, w3_ffn2, w2_ffn2, norm_final):
    depth = norm_ffn1.shape[0]
    nbp, seq, d = x_prompt.shape
    nbs, dec, _ = x_sample.shape
    xp = x_prompt
    xs = x_sample.reshape(nbs * dec, d)
    prompt_tile = min(PROMPT_TILE, seq)
    new_p, new_s = [], []
    for l in range(depth):
        last = l == depth - 1
        ffn1 = _ffn_weights(w1_ffn1[l], w3_ffn1[l], w2_ffn1[l])
        ffn2 = _ffn_weights(w1_ffn2[l], w3_ffn2[l], w2_ffn2[l])
        mix = (norm_mix[l], w_in[l], conv_w[l], conv_b[l], gate_bias[l], rel_bias[l], norm_mlstm_out[l], w_out[l])
        gf = norm_final if last else None

        xp, st = _mixer(xp, mix, None, tile=prompt_tile, chunk=prompt_tile, pos0=0, chunked=True,
                        ffn=(norm_ffn1[l],) + ffn1)
        new_p.append(st)
        xp = _ffn(xp.reshape(nbp * seq, d), norm_ffn2[l], ffn2, gf).reshape(nbp, seq, d)

        xs = _ffn(xs, norm_ffn1[l], ffn1)
        feature_major = lambda c: c.transpose(0, 2, 3, 1).reshape(nbs, A_WIDTH, -1)
        cache = (feature_major(cache_attn_k[l]), feature_major(cache_attn_v[l]),
                 state_mlstm_C[l], state_mlstm_n[l], state_mlstm_m[l], state_mlstm_conv[l])
        xs3, st = _mixer(xs.reshape(nbs, dec, d), mix, cache, tile=dec, chunk=dec, pos0=PAST_LEN, chunked=False,
                         cache_t=True)
        new_s.append(st)
        xs = _ffn(xs3.reshape(nbs * dec, d), norm_ffn2[l], ffn2, gf)

    stack = lambda states, i: jnp.stack([s[i] for s in states])
    return ((xp, xs.reshape(nbs, dec, d))
            + tuple(stack(new_p, i) for i in range(6)) + tuple(stack(new_s, i) for i in range(6)))
```

```python
import functools

import numpy as np
import jax
import jax.numpy as jnp
from jax import lax
from jax.experimental import pallas as pl
from jax.experimental.pallas import tpu as pltpu

F32 = jnp.float32
BF16 = jnp.bfloat16

CHUNK = 64
ATT_BAND = 8 * CHUNK
A_HEADS = 8
A_DH = 64
A_WIDTH = A_HEADS * A_DH
M_HEADS = 4
M_DH = 128
M_WIDTH = M_HEADS * M_DH
REL_CLIP = 128
CONV_W = 4
PAST_LEN = 4096
EPS = 1e-6
NEG = -1e30
LOG2E = 1.4426950408889634

LANES = 128
SUBLANES = 8
MXU_DIM = 256
VMEM_LIMIT_BYTES = 60 * 1024 * 1024

OFF_AK = A_WIDTH
OFF_AV = 2 * A_WIDTH
OFF_MQK = 3 * A_WIDTH
OFF_MV = OFF_MQK + 2 * M_WIDTH
OFF_MO = OFF_MV + M_WIDTH
OFF_MG = OFF_MO + M_WIDTH
N_GATES = 2 * M_HEADS
D_IN_PAD = OFF_MG + LANES
GATE_ROWS = 16
ZC_MV = A_WIDTH
ZC_MO = ZC_MV + M_WIDTH
ZC_MG = ZC_MO + M_WIDTH
ZC_WIDTH = ZC_MG + LANES
NM_ROWS = 32

FFN_ROWS = 512
FFN_CHUNK = MXU_DIM
CAST_STEPS = 8
PROMPT_TILE = 256
STEP_ROWS = 128
ATTENTION_AHEAD = 4
PROMPT_AHEAD = 2
UNITS_BEFORE_CONV = 2
UNITS_BEFORE_GATES = 2
UNITS_PER_MLSTM_BLOCK = 3


def _rms(x, g):
    return x * lax.rsqrt(jnp.mean(x * x, axis=-1, keepdims=True) + EPS) * g


def _sigmoid(x):
    return 1.0 / (1.0 + jnp.exp(-x))


def _silu(x):
    h = 0.5 * x
    return h + h * jnp.tanh(h)


def _log_sigmoid(x):
    return jnp.minimum(x, 0.0) - jnp.log1p(jnp.exp(-jnp.abs(x)))


def _dot(a, b):
    return jnp.dot(a, b, preferred_element_type=F32)


def _dot_nt(a, b):
    return lax.dot_general(a, b, (((1,), (1,)), ((), ())), preferred_element_type=F32)


def _split3(x):
    p1 = x.astype(BF16)
    r1 = x - p1.astype(F32)
    p2 = r1.astype(BF16)
    p3 = (r1 - p2.astype(F32)).astype(BF16)
    return p1, p2, p3


def _resident(shape):
    nd = len(shape)
    return pl.BlockSpec(shape, lambda *_: (0,) * nd, pipeline_mode=pl.Buffered(1))


def _ffn_kernel(x_ref, g_ref, w1_ref, w3_ref, w2_ref, *rest, final_norm):
    if final_norm:
        gf_ref, o_ref, h_ref, u_ref = rest
    else:
        o_ref, h_ref, u_ref = rest
    h_ref[...] = _rms(x_ref[...], g_ref[...]).astype(BF16)
    for c in range(0, u_ref.shape[1], FFN_CHUNK):
        h = h_ref[...]
        a = _dot(h, w1_ref[:, c:c + FFN_CHUNK])
        b = _dot(h, w3_ref[:, c:c + FFN_CHUNK])
        u_ref[:, c:c + FFN_CHUNK] = (_silu(a) * b).astype(BF16)
    y = x_ref[...] + 0.5 * _dot(u_ref[...], w2_ref[...])
    if final_norm:
        y = _rms(y, gf_ref[...])
    o_ref[...] = y


def _cast_kernel(*refs):
    n = len(refs) // 2
    for src, dst in zip(refs[:n], refs[n:]):
        dst[...] = src[...].astype(BF16)


def _ffn_weights(w1, w3, w2):
    mats = (w1, w3, w2)
    assert all(m.shape[0] % (CAST_STEPS * 2 * SUBLANES) == 0 for m in mats)
    specs = [pl.BlockSpec((m.shape[0] // CAST_STEPS, m.shape[1]), lambda i: (i, 0)) for m in mats]
    return tuple(pl.pallas_call(
        _cast_kernel,
        grid=(CAST_STEPS,),
        in_specs=specs,
        out_specs=specs,
        out_shape=[jax.ShapeDtypeStruct(m.shape, BF16) for m in mats],
        compiler_params=pltpu.CompilerParams(dimension_semantics=("arbitrary",)),
        name="cast_weights",
    )(*mats))


def _ffn(x2d, g, weights, gf=None):
    n, d = x2d.shape
    w1, w3, w2 = weights
    f = w1.shape[1]
    assert f % FFN_CHUNK == 0
    rows = min(FFN_ROWS, n)
    final_norm = gf is not None
    row_spec = pl.BlockSpec((rows, d), lambda i: (i, 0))
    in_specs = [row_spec, _resident((1, d)), _resident(w1.shape), _resident(w3.shape), _resident(w2.shape)]
    args = [x2d, g.reshape(1, d), w1, w3, w2]
    if final_norm:
        in_specs.append(_resident((1, d)))
        args.append(gf.reshape(1, d))
    return pl.pallas_call(
        functools.partial(_ffn_kernel, final_norm=final_norm),
        grid=(n // rows,),
        in_specs=in_specs,
        out_specs=row_spec,
        out_shape=jax.ShapeDtypeStruct((n, d), F32),
        scratch_shapes=[pltpu.VMEM((rows, d), BF16), pltpu.VMEM((rows, f), BF16)],
        compiler_params=pltpu.CompilerParams(dimension_semantics=("arbitrary",),
                                             vmem_limit_bytes=VMEM_LIMIT_BYTES),
        name="ffn_final" if final_norm else "ffn",
    )(*args)


def _mixer_kernel(*refs, tile, group, chunk, band_rows, pos0, n_tiles, fused_ffn, has_state, cache_t, ahead):
    refs = iter(refs)
    take = lambda n: [next(refs) for _ in range(n)]
    (xa_ref,) = take(1)
    if fused_ffn:
        g1_ref, w1_ref, w3_ref, w2_ref = take(4)
    else:
        (xb_ref,) = take(1)
    gmix_ref, win_ref, wgt_ref, convw_ref, convb_ref, gbrow_ref, gbcol_ref, diag_ref, mask_ref, normm_ref, wout_ref = take(11)
    if has_state:
        k0_ref, v0_ref, c0_ref, n0_ref, m0_ref, conv0_ref = take(6)
    y_ref, ko_ref, vo_ref, co_ref, no_ref, mo_ref, convo_ref = take(7)
    z_new, z_cur, g_new, g_cur, kband, vband, ubuf, c_s, nm_s, mix_s, h_s, tab_s = take(12)
    n_rows, m_rows = slice(0, M_HEADS), slice(SUBLANES, SUBLANES + M_HEADS)
    if fused_ffn:
        u_s, h1_s, x1_new, x1_mid, x1_cur = take(5)
    if cache_t:
        kt_s, vt_s = take(2)
        assert has_state and n_tiles == 1 and pos0 >= ATT_BAND
    assert group == 1 or n_tiles == 1
    streams = range(group)
    span = lambda g: slice(g * tile, (g + 1) * tile)

    p = pl.program_id(0)
    lag = 2 if fused_ffn else 1
    t = lax.rem(jnp.maximum(p - lag, 0), n_tiles)

    @pl.when(p == 0)
    def _first_step():
        z_cur[...] = jnp.zeros_like(z_cur)
        g_cur[...] = jnp.zeros_like(g_cur)
        for g in streams:
            kband[g, ATT_BAND:band_rows, :] = jnp.zeros((band_rows - ATT_BAND, A_WIDTH), BF16)
            vband[g, ATT_BAND:band_rows, :] = jnp.zeros((band_rows - ATT_BAND, A_WIDTH), BF16)
            ubuf[g, SUBLANES:SUBLANES + tile, :] = jnp.zeros((tile, 2 * M_WIDTH), F32)
        if fused_ffn:
            x1_cur[...] = jnp.zeros_like(x1_cur)
            x1_mid[...] = jnp.zeros_like(x1_mid)
        width = diag_ref.shape[1]
        for head in range(A_HEADS):
            rows_of_diag = jnp.broadcast_to(diag_ref[head:head + 1, :], (tile, width))
            skew = pltpu.roll(rows_of_diag, width - (tile - 1), 1, stride=1, stride_axis=0)
            tab_s[head] = skew[:, :band_rows] + mask_ref[...]

    @pl.when(t == 0)
    def _load_state():
        if has_state:
            for g in streams:
                if cache_t:
                    kt_s[g] = k0_ref[g].astype(BF16)
                    vt_s[g] = v0_ref[g].astype(BF16)
                else:
                    kband[g, 0:ATT_BAND, :] = k0_ref[g].astype(BF16)
                    vband[g, 0:ATT_BAND, :] = v0_ref[g].astype(BF16)
                ubuf[g, 0:SUBLANES, :] = conv0_ref[g]
            c_s[...] = c0_ref[...]
            nm_s[:, n_rows, :] = n0_ref[...]
            nm_s[:, m_rows, :] = m0_ref[...]
        else:
            for g in streams:
                kband[g, 0:ATT_BAND, :] = jnp.zeros((ATT_BAND, A_WIDTH), BF16)
                vband[g, 0:ATT_BAND, :] = jnp.zeros((ATT_BAND, A_WIDTH), BF16)
                ubuf[g, 0:SUBLANES, :] = jnp.zeros((SUBLANES, 2 * M_WIDTH), F32)
            c_s[...] = jnp.zeros_like(c_s)
            nm_s[...] = jnp.zeros_like(nm_s)

    def stage_a1():
        h1_s[...] = _rms(xa_ref[...], g1_ref[...]).astype(BF16)
        yield
        for c in range(0, u_s.shape[1], FFN_CHUNK):
            hf = h1_s[...]
            a = _dot(hf, w1_ref[:, c:c + FFN_CHUNK])
            b = _dot(hf, w3_ref[:, c:c + FFN_CHUNK])
            u_s[:, c:c + FFN_CHUNK] = (_silu(a) * b).astype(BF16)
            yield
        for c in range(0, x1_new.shape[1], MXU_DIM):
            x1_new[:, c:c + MXU_DIM] = (xa_ref[:, c:c + MXU_DIM]
                                        + 0.5 * _dot(u_s[...], w2_ref[:, c:c + MXU_DIM]))
            yield

    def stage_a2():
        x_in = x1_mid[...] if fused_ffn else xa_ref[...]
        h_s[...] = _rms(x_in, gmix_ref[...]).astype(BF16)
        yield
        for c in range(0, D_IN_PAD, MXU_DIM):
            c1 = min(c + MXU_DIM, D_IN_PAD)
            z_new[:, c:c1] = _dot(h_s[...], win_ref[:, c:c1])
            projected_cols[0] = c1
            yield
        g_new[...] = _dot_nt(wgt_ref[...], h_s[...])

    projected_cols = [0]

    def alternate(*gens):
        gens = list(gens)
        while gens:
            for gen in list(gens):
                try:
                    next(gen)
                    yield
                except StopIteration:
                    gens.remove(gen)

    units = alternate(stage_a1(), stage_a2()) if fused_ffn else stage_a2()

    def project_next(count=1):
        for _ in range(count):
            next(units, None)

    lane = lax.broadcasted_iota(jnp.int32, (1, LANES), 1)
    even = lane < A_DH
    if pos0 < ATT_BAND:
        col = lax.broadcasted_iota(jnp.int32, (1, band_rows), 1)
        in_stream = col >= (ATT_BAND - pos0) - t * tile

    def scores(g, head):
        lo = (head // 2) * LANES
        mine = even if head % 2 == 0 else jnp.logical_not(even)
        qh = jnp.where(mine, z_cur[span(g), lo:lo + LANES] * (LOG2E * A_DH ** -0.5), 0.0).astype(BF16)
        if cache_t:
            s_old = _dot(qh, kt_s[g, lo:lo + LANES, :]) + tab_s[head, :, 0:ATT_BAND]
            s_new = _dot_nt(qh, kband[g, ATT_BAND:band_rows, lo:lo + LANES]) + tab_s[head, :, ATT_BAND:band_rows]
            return s_old, s_new
        s = _dot_nt(qh, kband[g, :, lo:lo + LANES]) + tab_s[head]
        if pos0 < ATT_BAND:
            s = jnp.where(in_stream, s, NEG)
        return s

    def attend(g, head, s):
        lo = (head // 2) * LANES
        mine = even if head % 2 == 0 else jnp.logical_not(even)
        if cache_t:
            s_old, s_new = s
            top = jnp.maximum(jnp.max(s_old, axis=-1, keepdims=True), jnp.max(s_new, axis=-1, keepdims=True))
            e_old = jnp.exp2(s_old - top)
            e_new = jnp.exp2(s_new - top)
            o = (_dot_nt(e_old.astype(BF16), vt_s[g, lo:lo + LANES, :])
                 + _dot(e_new.astype(BF16), vband[g, ATT_BAND:band_rows, lo:lo + LANES]))
            total = jnp.sum(e_old, axis=-1, keepdims=True) + jnp.sum(e_new, axis=-1, keepdims=True)
            o = jnp.where(mine, o, 0.0) * (1.0 / total)
        else:
            vp = vband[g, :, lo:lo + LANES]
            e = jnp.exp2(s - jnp.max(s, axis=-1, keepdims=True))
            o = _dot(e.astype(BF16), jnp.where(mine, vp, jnp.zeros_like(vp)))
            o = o * (1.0 / jnp.sum(e, axis=-1, keepdims=True))
        if head % 2 == 0:
            mix_s[span(g), lo:lo + LANES] = o
        else:
            mix_s[span(g), lo:lo + LANES] += o

    pending = []
    for g in streams:
        for head in range(A_HEADS):
            pending.append((g, head, scores(g, head)))
            project_next(lag)
            if len(pending) > ahead:
                attend(*pending.pop(0))
    for item in pending:
        attend(*item)
    assert projected_cols[0] >= OFF_MQK
    ko_ref[...] = z_new[:, OFF_AK:OFF_AV]
    vo_ref[...] = z_new[:, OFF_AV:OFF_MQK]
    for g in streams:
        if n_tiles > 1:
            kband[g, 0:ATT_BAND, :] = kband[g, tile:tile + ATT_BAND, :]
            vband[g, 0:ATT_BAND, :] = vband[g, tile:tile + ATT_BAND, :]
        kband[g, ATT_BAND:ATT_BAND + tile, :] = z_new[span(g), OFF_AK:OFF_AV].astype(BF16)
        vband[g, ATT_BAND:ATT_BAND + tile, :] = z_new[span(g), OFF_AV:OFF_MQK].astype(BF16)
    z_cur[:, 0:OFF_AK] = z_new[:, 0:OFF_AK]

    project_next(UNITS_BEFORE_CONV)
    mq, mk, last_rows = [], [], []
    for g in streams:
        frames = ubuf[g]
        qk = convb_ref[...] + frames[SUBLANES:, :] * convw_ref[CONV_W - 1:CONV_W, :]
        for j in range(CONV_W - 1):
            shifted = pltpu.roll(frames, CONV_W - 1 - j, 0)
            qk = qk + shifted[SUBLANES:, :] * convw_ref[j:j + 1, :]
        qk = _silu(qk)
        mq.append(qk[:, :M_WIDTH])
        mk.append(qk[:, M_WIDTH:] * (M_DH ** -0.5))
        last_rows.append(ubuf[g, tile:tile + SUBLANES, :])
        convo_ref[g] = last_rows[g]

    project_next(UNITS_BEFORE_GATES)
    rows = group * tile
    g_col = z_cur[:, ZC_MG:ZC_MG + LANES] + gbrow_ref[...]
    g_row = g_cur[...] + gbcol_ref[...]
    lf_col = _log_sigmoid(g_col)
    lf_row = _log_sigmoid(g_row)
    ri = lax.broadcasted_iota(jnp.int32, (rows, rows), 0)
    ci = lax.broadcasted_iota(jnp.int32, (rows, rows), 1)
    assert chunk & (chunk - 1) == 0
    chunk_shift = chunk.bit_length() - 1
    same_chunk = (ri >> chunk_shift) == (ci >> chunk_shift)
    tri = jnp.where(jnp.logical_and(same_chunk, ci <= ri), 1.0, 0.0).astype(BF16)
    tri_t = jnp.where(jnp.logical_and(same_chunk, ri <= ci), 1.0, 0.0).astype(BF16)
    b_col = sum(_dot(tri, part) for part in _split3(lf_col))
    b_row = sum(_dot(part, tri_t) for part in _split3(lf_row))

    causal = (lax.broadcasted_iota(jnp.int32, (chunk, chunk), 1)
              <= lax.broadcasted_iota(jnp.int32, (chunk, chunk), 0))
    assert chunk == tile
    first = {}

    def block_scores(g, hd):
        lo, hi = hd * M_DH, (hd + 1) * M_DH
        cmat = c_s[g, hd]
        m_prev = nm_s[g, SUBLANES + hd:SUBLANES + hd + 1, 0:1]
        bc = b_col[span(g), M_HEADS + hd:M_HEADS + hd + 1]
        br = b_row[M_HEADS + hd:M_HEADS + hd + 1, span(g)]
        igr = g_row[hd:hd + 1, span(g)]
        dmat = jnp.where(causal, bc + (igr - br), NEG)
        inter = bc + m_prev
        mt = jnp.maximum(inter, jnp.max(dmat, axis=-1, keepdims=True))
        q = mq[g][:, lo:hi]
        k = mk[g][:, lo:hi]
        vb = z_cur[span(g), ZC_MV + lo:ZC_MV + hi].astype(BF16)
        qb = q.astype(BF16)
        n_cols = jnp.broadcast_to(nm_s[g, hd:hd + 1, :], (M_DH, M_DH)).T
        c_and_n = jnp.concatenate([cmat, n_cols], axis=1).astype(BF16)
        first[g, hd] = dict(bc=bc, mt=mt, inter=inter, m_prev=m_prev, cmat=cmat, k=k, vb=vb, dmat=dmat,
                            qk=_dot_nt(qb, k.astype(BF16)), qcn=_dot(qb, c_and_n))

    def block_output(g, hd):
        lo, hi = hd * M_DH, (hd + 1) * M_DH
        f = first[g, hd]
        s = f["qk"] * jnp.exp(f["dmat"] - f["mt"])
        iw = jnp.exp(f["inter"] - f["mt"])
        v_and_ones = jnp.concatenate([f["vb"], jnp.ones((tile, M_DH), BF16)], axis=1)
        both = iw * f["qcn"] + _dot(s.astype(BF16), v_and_ones)
        num, den = both[:, :M_DH], both[:, M_DH:]
        mh = num / jnp.maximum(jnp.abs(den), jnp.exp(-f["mt"]))
        mh = mh * _sigmoid(z_cur[span(g), ZC_MO + lo:ZC_MO + hi])
        mix_s[span(g), A_WIDTH + lo:A_WIDTH + hi] = _rms(mh, normm_ref[:, lo:hi])

    def block_state(g, hd):
        f = first[g, hd]
        bc, mt, m_prev = f["bc"], f["mt"], f["m_prev"]
        igc = g_col[span(g), hd:hd + 1]
        b_last = bc[tile - 1:tile, :]
        m_new = mt[tile - 1:tile, :]
        kw = f["k"] * jnp.exp(b_last - bc + igc - m_new)
        decay = jnp.exp(b_last + m_prev - m_new)
        c_s[g, hd] = decay * f["cmat"] + _dot(kw.T.astype(BF16), f["vb"])
        nm_s[g, hd:hd + 1, :] = decay * nm_s[g, hd:hd + 1, :] + jnp.sum(kw, axis=0, keepdims=True)
        nm_s[g, SUBLANES + hd:SUBLANES + hd + 1, :] = jnp.broadcast_to(m_new, (1, LANES))

    blocks = [(g, hd) for hd in range(M_HEADS) for g in streams]
    if group == 1:
        for blk in blocks:
            project_next(UNITS_PER_MLSTM_BLOCK)
            block_scores(*blk)
            block_output(*blk)
            block_state(*blk)
    else:
        for phase in (block_scores, block_output, block_state):
            for blk in blocks:
                project_next()
                phase(*blk)

    for _ in units:
        pass
    co_ref[...] = c_s[...]
    no_ref[...] = nm_s[:, n_rows, :]
    mo_ref[...] = nm_s[:, m_rows, :]
    resid = x1_cur[...] if fused_ffn else xb_ref[...]
    y_ref[...] = resid + _dot(mix_s[...].astype(BF16), wout_ref[...])

    if fused_ffn:
        x1_cur[...] = x1_mid[...]
        x1_mid[...] = x1_new[...]
    for g in streams:
        ubuf[g, 0:SUBLANES, :] = last_rows[g]
        ubuf[g, SUBLANES:SUBLANES + tile, :] = z_new[span(g), OFF_MQK:OFF_MV]
    z_cur[:, ZC_MV:ZC_WIDTH] = z_new[:, OFF_MV:D_IN_PAD]
    g_cur[...] = g_new[...]


def _bias_pieces(rel_bias, tile, band_rows, chunked):
    heads = rel_bias.shape[0]
    span = band_rows + tile - 1
    width = -(-span // LANES) * LANES
    n_far = ATT_BAND + tile - REL_CLIP
    n_near = width - n_far - (2 * REL_CLIP - 1)
    diag = jnp.concatenate([jnp.broadcast_to(rel_bias[:, 2 * REL_CLIP:], (heads, n_far)),
                            rel_bias[:, 2 * REL_CLIP - 1:0:-1],
                            jnp.broadcast_to(rel_bias[:, :1], (heads, n_near))], axis=1)
    i = np.arange(tile)[:, None]
    j = np.arange(band_rows)[None, :]
    visible = j < ATT_BAND + tile
    if chunked:
        qc = i // CHUNK
        kc = (j - ATT_BAND) // CHUNK
        visible = visible & (kc <= qc) & (kc >= qc - ATT_BAND // CHUNK)
    mask = jnp.asarray(np.where(visible, 0.0, NEG * LOG2E), F32)
    return diag * LOG2E, mask


def _mixer(x, params, state, *, tile, chunk, pos0, chunked, ffn=None, cache_t=False):
    nb, frames, d = x.shape
    gmix, w_in, conv_w, conv_b, gate_bias, rel_bias, norm_m, w_out = params
    n_tiles = frames // tile
    band_rows = ATT_BAND + -(-tile // LANES) * LANES
    keep_tiles = min(ATT_BAND, frames) // tile
    width2 = 2 * M_WIDTH
    fused_ffn = ffn is not None
    has_state = state is not None
    group = max(1, min(nb, STEP_ROWS // tile)) if n_tiles == 1 else 1
    assert nb % group == 0
    rows = group * tile

    win = jnp.concatenate([w_in, jnp.zeros((d, D_IN_PAD - w_in.shape[1]), F32)], axis=1).astype(BF16)
    wgt = jnp.zeros((GATE_ROWS, d), F32).at[:N_GATES].set(w_in[:, OFF_MG:].T).astype(BF16)
    gb_row = jnp.zeros((1, LANES), F32).at[0, :N_GATES].set(gate_bias)
    gb_col = jnp.zeros((GATE_ROWS, 1), F32).at[:N_GATES, 0].set(gate_bias)
    diag, mask = _bias_pieces(rel_bias, tile, band_rows, chunked)

    lag = 2 if fused_ffn else 1
    n_tiles_all = (nb // group) * n_tiles
    n_steps = n_tiles_all + lag
    entering = lambda p: jnp.minimum(p, n_tiles_all - 1)
    projected = lambda p: jnp.clip(p - (lag - 1), 0, n_tiles_all - 1)
    mixed = lambda p: jnp.maximum(p - lag, 0)
    per_group = lambda *dims: pl.BlockSpec((group,) + dims,
                                           lambda p: (lax.div(mixed(p), n_tiles),) + (0,) * len(dims))
    enter_spec = pl.BlockSpec((rows, d), lambda p: (entering(p), 0))
    mix_spec = pl.BlockSpec((rows, d), lambda p: (mixed(p), 0))

    def kv_index(p):
        q = projected(p)
        return (lax.div(q, n_tiles) * keep_tiles + jnp.maximum(lax.rem(q, n_tiles) - (n_tiles - keep_tiles), 0), 0)

    kv_spec = pl.BlockSpec((rows, A_WIDTH), kv_index)

    x2d = x.reshape(nb * frames, d)
    args, in_specs = [x2d], [enter_spec]
    scratch_ffn = []
    if fused_ffn:
        g1, w1, w3, w2 = ffn
        args += [g1.reshape(1, d), w1, w3, w2]
        in_specs += [_resident((1, d)), _resident(w1.shape), _resident(w3.shape), _resident(w2.shape)]
        scratch_ffn = [pltpu.VMEM((rows, w1.shape[1]), BF16), pltpu.VMEM((rows, d), BF16)] + [pltpu.VMEM((rows, d), F32)] * 3
    else:
        args.append(x2d)
        in_specs.append(mix_spec)
    args += [gmix.reshape(1, d), win, wgt, conv_w, conv_b.reshape(1, width2), gb_row, gb_col, diag, mask,
             norm_m.reshape(1, M_WIDTH), w_out.astype(BF16)]
    in_specs += [_resident((1, d)), _resident(win.shape), _resident(wgt.shape),
                 _resident((CONV_W, width2)), _resident((1, width2)), _resident((1, LANES)),
                 _resident((GATE_ROWS, 1)), _resident(diag.shape), _resident(mask.shape), _resident((1, M_WIDTH)),
                 _resident((d, d))]
    if has_state:
        k0, v0, c0, n0, m0, conv0 = state
        conv0p = jnp.concatenate([jnp.zeros((nb, SUBLANES - (CONV_W - 1), width2), F32), conv0], axis=1)
        m0p = jnp.broadcast_to(m0[:, :, None], (nb, M_HEADS, LANES))
        args += [k0, v0, c0, n0, m0p, conv0p]
        in_specs += [per_group(ATT_BAND, A_WIDTH), per_group(ATT_BAND, A_WIDTH),
                     per_group(M_HEADS, M_DH, M_DH), per_group(M_HEADS, M_DH), per_group(M_HEADS, LANES),
                     per_group(SUBLANES, width2)]
    out_specs = [
        mix_spec, kv_spec, kv_spec, per_group(M_HEADS, M_DH, M_DH), per_group(M_HEADS, M_DH),
        per_group(M_HEADS, LANES), per_group(SUBLANES, width2),
    ]
    keep = keep_tiles * tile
    out_shape = [
        jax.ShapeDtypeStruct((nb * frames, d), F32),
        jax.ShapeDtypeStruct((nb * keep, A_WIDTH), F32),
        jax.ShapeDtypeStruct((nb * keep, A_WIDTH), F32),
        jax.ShapeDtypeStruct((nb, M_HEADS, M_DH, M_DH), F32),
        jax.ShapeDtypeStruct((nb, M_HEADS, M_DH), F32),
        jax.ShapeDtypeStruct((nb, M_HEADS, LANES), F32),
        jax.ShapeDtypeStruct((nb, SUBLANES, width2), F32),
    ]
    scratch = [
        pltpu.VMEM((rows, D_IN_PAD), F32), pltpu.VMEM((rows, ZC_WIDTH), F32),
        pltpu.VMEM((GATE_ROWS, rows), F32), pltpu.VMEM((GATE_ROWS, rows), F32),
        pltpu.VMEM((group, band_rows, A_WIDTH), BF16), pltpu.VMEM((group, band_rows, A_WIDTH), BF16),
        pltpu.VMEM((group, tile + SUBLANES, width2), F32),
        pltpu.VMEM((group, M_HEADS, M_DH, M_DH), F32),
        pltpu.VMEM((group, NM_ROWS, LANES), F32),
        pltpu.VMEM((rows, d), F32), pltpu.VMEM((rows, d), BF16),
        pltpu.VMEM((A_HEADS, tile, band_rows), F32),
    ] + scratch_ffn
    if cache_t:
        scratch += [pltpu.VMEM((group, A_WIDTH, ATT_BAND), BF16)] * 2
    y, ko, vo, c1, n1, m1, conv1 = pl.pallas_call(
        functools.partial(_mixer_kernel, tile=tile, group=group, chunk=chunk, band_rows=band_rows, pos0=pos0,
                          n_tiles=n_tiles, fused_ffn=fused_ffn, has_state=has_state, cache_t=cache_t,
                          ahead=PROMPT_AHEAD if group == 1 else ATTENTION_AHEAD),
        grid=(n_steps,),
        in_specs=in_specs,
        out_specs=out_specs,
        out_shape=out_shape,
        scratch_shapes=scratch,
        compiler_params=pltpu.CompilerParams(dimension_semantics=("arbitrary",),
                                             vmem_limit_bytes=VMEM_LIMIT_BYTES),
        name="mixer_chunked" if chunked else "mixer_step",
    )(*args)
    new_state = (ko.reshape(nb, keep, A_HEADS, A_DH), vo.reshape(nb, keep, A_HEADS, A_DH), c1, n1,
                 m1[:, :, 0], conv1[:, SUBLANES - (CONV_W - 1):, :])
    return y.reshape(nb, frames, d), new_state


def kernel(x_prompt, x_sample, cache_attn_k, cache_attn_v, state_mlstm_C, state_mlstm_n, state_mlstm_m, state_mlstm_conv, norm_ffn1, w1_ffn1, w3_ffn1, w2_ffn1, norm_mix, w_in, conv_w, conv_b, gate_bias, rel_bias, norm_mlstm_out, w_out, norm_ffn2, w1_ffn2, w3_ffn2, w2_ffn2, norm_final):
    depth = norm_ffn1.shape[0]
    nbp, seq, d = x_prompt.shape
    nbs, dec, _ = x_sample.shape
    xp = x_prompt
    xs = x_sample.reshape(nbs * dec, d)
    prompt_tile = min(PROMPT_TILE, seq)
    new_p, new_s = [], []
    for l in range(depth):
        last = l == depth - 1
        ffn1 = _ffn_weights(w1_ffn1[l], w3_ffn1[l], w2_ffn1[l])
        ffn2 = _ffn_weights(w1_ffn2[l], w3_ffn2[l], w2_ffn2[l])
        mix = (norm_mix[l], w_in[l], conv_w[l], conv_b[l], gate_bias[l], rel_bias[l], norm_mlstm_out[l], w_out[l])
        gf = norm_final if last else None

        xp, st = _mixer(xp, mix, None, tile=prompt_tile, chunk=prompt_tile, pos0=0, chunked=True,
                        ffn=(norm_ffn1[l],) + ffn1)
        new_p.append(st)
        xp = _ffn(xp.reshape(nbp * seq, d), norm_ffn2[l], ffn2, gf).reshape(nbp, seq, d)

        xs = _ffn(xs, norm_ffn1[l], ffn1)
        feature_major = lambda c: c.transpose(0, 2, 3, 1).reshape(nbs, A_WIDTH, -1)
        cache = (feature_major(cache_attn_k[l]), feature_major(cache_attn_v[l]),
                 state_mlstm_C[l], state_mlstm_n[l], state_mlstm_m[l], state_mlstm_conv[l])
        xs3, st = _mixer(xs.reshape(nbs, dec, d), mix, cache, tile=dec, chunk=dec, pos0=PAST_LEN, chunked=False,
                         cache_t=True)
        new_s.append(st)
        xs = _ffn(xs3.reshape(nbs * dec, d), norm_ffn2[l], ffn2, gf)

    stack = lambda states, i: jnp.stack([s[i] for s in states])
    return ((xp, xs.reshape(nbs, dec, d))
            + tuple(stack(new_p, i) for i in range(6)) + tuple(stack(new_s, i) for i in range(6)))
```

```python
import functools

import numpy as np
import jax
import jax.numpy as jnp
from jax import lax
from jax.experimental import pallas as pl
from jax.experimental.pallas import tpu as pltpu

F32 = jnp.float32
BF16 = jnp.bfloat16

CHUNK = 64
ATT_BAND = 8 * CHUNK
A_HEADS = 8
A_DH = 64
A_WIDTH = A_HEADS * A_DH
M_HEADS = 4
M_DH = 128
M_WIDTH = M_HEADS * M_DH
REL_CLIP = 128
CONV_W = 4
PAST_LEN = 4096
EPS = 1e-6
NEG = -1e30
LOG2E = 1.4426950408889634

LANES = 128
SUBLANES = 8
MXU_DIM = 256
VMEM_LIMIT_BYTES = 60 * 1024 * 1024

OFF_AK = A_WIDTH
OFF_AV = 2 * A_WIDTH
OFF_MQK = 3 * A_WIDTH
OFF_MV = OFF_MQK + 2 * M_WIDTH
OFF_MO = OFF_MV + M_WIDTH
OFF_MG = OFF_MO + M_WIDTH
N_GATES = 2 * M_HEADS
D_IN_PAD = OFF_MG + LANES
GATE_ROWS = 16
ZC_MV = A_WIDTH
ZC_MO = ZC_MV + M_WIDTH
ZC_MG = ZC_MO + M_WIDTH
ZC_WIDTH = ZC_MG + LANES
NM_ROWS = 32

FFN_ROWS = 512
FFN_CHUNK = MXU_DIM
CAST_STEPS = 8
PROMPT_TILE = 256
STEP_ROWS = 128
ATTENTION_AHEAD = 4
PROMPT_AHEAD = 2
UNITS_BEFORE_CONV = 2
UNITS_BEFORE_GATES = 2
UNITS_PER_MLSTM_BLOCK = 3


def _rms(x, g):
    return x * lax.rsqrt(jnp.mean(x * x, axis=-1, keepdims=True) + EPS) * g


def _sigmoid(x):
    return 1.0 / (1.0 + jnp.exp(-x))


def _silu(x):
    h = 0.5 * x
    return h + h * jnp.tanh(h)


def _log_sigmoid(x):
    return jnp.minimum(x, 0.0) - jnp.log1p(jnp.exp(-jnp.abs(x)))


def _dot(a, b):
    return jnp.dot(a, b, preferred_element_type=F32)


def _dot_nt(a, b):
    return lax.dot_general(a, b, (((1,), (1,)), ((), ())), preferred_element_type=F32)


def _split3(x):
    p1 = x.astype(BF16)
    r1 = x - p1.astype(F32)
    p2 = r1.astype(BF16)
    p3 = (r1 - p2.astype(F32)).astype(BF16)
    return p1, p2, p3


def _resident(shape):
    nd = len(shape)
    return pl.BlockSpec(shape, lambda *_: (0,) * nd, pipeline_mode=pl.Buffered(1))


def _ffn_kernel(x_ref, g_ref, w1_ref, w3_ref, w2_ref, *rest, final_norm):
    if final_norm:
        gf_ref, o_ref, h_ref, u_ref = rest
    else:
        o_ref, h_ref, u_ref = rest
    h_ref[...] = _rms(x_ref[...], g_ref[...]).astype(BF16)
    for c in range(0, u_ref.shape[1], FFN_CHUNK):
        h = h_ref[...]
        a = _dot(h, w1_ref[:, c:c + FFN_CHUNK])
        b = _dot(h, w3_ref[:, c:c + FFN_CHUNK])
        u_ref[:, c:c + FFN_CHUNK] = (_silu(a) * b).astype(BF16)
    y = x_ref[...] + 0.5 * _dot(u_ref[...], w2_ref[...])
    if final_norm:
        y = _rms(y, gf_ref[...])
    o_ref[...] = y


def _cast_kernel(*refs):
    n = len(refs) // 2
    for src, dst in zip(refs[:n], refs[n:]):
        dst[...] = src[...].astype(BF16)


def _ffn_weights(w1, w3, w2):
    mats = (w1, w3, w2)
    assert all(m.shape[0] % (CAST_STEPS * 2 * SUBLANES) == 0 for m in mats)
    specs = [pl.BlockSpec((m.shape[0] // CAST_STEPS, m.shape[1]), lambda i: (i, 0)) for m in mats]
    return tuple(pl.pallas_call(
        _cast_kernel,
        grid=(CAST_STEPS,),
        in_specs=specs,
        out_specs=specs,
        out_shape=[jax.ShapeDtypeStruct(m.shape, BF16) for m in mats],
        compiler_params=pltpu.CompilerParams(dimension_semantics=("arbitrary",)),
        name="cast_weights",
    )(*mats))


def _ffn(x2d, g, weights, gf=None):
    n, d = x2d.shape
    w1, w3, w2 = weights
    f = w1.shape[1]
    assert f % FFN_CHUNK == 0
    rows = min(FFN_ROWS, n)
    final_norm = gf is not None
    row_spec = pl.BlockSpec((rows, d), lambda i: (i, 0))
    in_specs = [row_spec, _resident((1, d)), _resident(w1.shape), _resident(w3.shape), _resident(w2.shape)]
    args = [x2d, g.reshape(1, d), w1, w3, w2]
    if final_norm:
        in_specs.append(_resident((1, d)))
        args.append(gf.reshape(1, d))
    return pl.pallas_call(
        functools.partial(_ffn_kernel, final_norm=final_norm),
        grid=(n // rows,),
        in_specs=in_specs,
        out_specs=row_spec,
        out_shape=jax.ShapeDtypeStruct((n, d), F32),
        scratch_shapes=[pltpu.VMEM((rows, d), BF16), pltpu.VMEM((rows, f), BF16)],
        compiler_params=pltpu.CompilerParams(dimension_semantics=("arbitrary",),
                                             vmem_limit_bytes=VMEM_LIMIT_BYTES),
        name="ffn_final" if final_norm else "ffn",
    )(*args)


def _mixer_kernel(*refs, tile, group, chunk, band_rows, pos0, n_tiles, fused_ffn, has_state, cache_t, ahead):
    refs = iter(refs)
    take = lambda n: [next(refs) for _ in range(n)]
    (xa_ref,) = take(1)
    if fused_ffn:
        g1_ref, w1_ref, w3_ref, w2_ref = take(4)
    else:
        (xb_ref,) = take(1)
    gmix_ref, win_ref, wgt_ref, convw_ref, convb_ref, gbrow_ref, gbcol_ref, diag_ref, mask_ref, normm_ref, wout_ref = take(11)
    tri_ref, trit_ref = take(2)
    if has_state:
        k0_ref, v0_ref, c0_ref, n0_ref, m0_ref, conv0_ref = take(6)
    y_ref, ko_ref, vo_ref, co_ref, no_ref, mo_ref, convo_ref = take(7)
    z_new, z_cur, g_new, g_cur, kband, vband, ubuf, c_s, nm_s, mix_s, h_s, tab_s = take(12)
    n_rows, m_rows = slice(0, M_HEADS), slice(SUBLANES, SUBLANES + M_HEADS)
    if fused_ffn:
        u_s, h1_s, x1_new, x1_mid, x1_cur = take(5)
    if cache_t:
        kt_s, vt_s = take(2)
        assert has_state and n_tiles == 1 and pos0 >= ATT_BAND
    assert group == 1 or n_tiles == 1
    streams = range(group)
    span = lambda g: slice(g * tile, (g + 1) * tile)

    p = pl.program_id(0)
    lag = 2 if fused_ffn else 1
    t = lax.rem(jnp.maximum(p - lag, 0), n_tiles)

    @pl.when(p == 0)
    def _first_step():
        z_cur[...] = jnp.zeros_like(z_cur)
        g_cur[...] = jnp.zeros_like(g_cur)
        for g in streams:
            kband[g, ATT_BAND:band_rows, :] = jnp.zeros((band_rows - ATT_BAND, A_WIDTH), BF16)
            vband[g, ATT_BAND:band_rows, :] = jnp.zeros((band_rows - ATT_BAND, A_WIDTH), BF16)
            ubuf[g, SUBLANES:SUBLANES + tile, :] = jnp.zeros((tile, 2 * M_WIDTH), F32)
        if fused_ffn:
            x1_cur[...] = jnp.zeros_like(x1_cur)
            x1_mid[...] = jnp.zeros_like(x1_mid)
        width = diag_ref.shape[1]
        for head in range(A_HEADS):
            rows_of_diag = jnp.broadcast_to(diag_ref[head:head + 1, :], (tile, width))
            skew = pltpu.roll(rows_of_diag, width - (tile - 1), 1, stride=1, stride_axis=0)
            tab_s[head] = skew[:, :band_rows] + mask_ref[...]

    @pl.when(t == 0)
    def _load_state():
        if has_state:
            for g in streams:
                if cache_t:
                    kt_s[g] = k0_ref[g].astype(BF16)
                    vt_s[g] = v0_ref[g].astype(BF16)
                else:
                    kband[g, 0:ATT_BAND, :] = k0_ref[g].astype(BF16)
                    vband[g, 0:ATT_BAND, :] = v0_ref[g].astype(BF16)
                ubuf[g, 0:SUBLANES, :] = conv0_ref[g]
            c_s[...] = c0_ref[...]
            nm_s[:, n_rows, :] = n0_ref[...]
            nm_s[:, m_rows, :] = m0_ref[...]
        else:
            for g in streams:
                kband[g, 0:ATT_BAND, :] = jnp.zeros((ATT_BAND, A_WIDTH), BF16)
                vband[g, 0:ATT_BAND, :] = jnp.zeros((ATT_BAND, A_WIDTH), BF16)
                ubuf[g, 0:SUBLANES, :] = jnp.zeros((SUBLANES, 2 * M_WIDTH), F32)
            c_s[...] = jnp.zeros_like(c_s)
            nm_s[...] = jnp.zeros_like(nm_s)

    def stage_a1():
        h1_s[...] = _rms(xa_ref[...], g1_ref[...]).astype(BF16)
        yield
        for c in range(0, u_s.shape[1], FFN_CHUNK):
            hf = h1_s[...]
            a = _dot(hf, w1_ref[:, c:c + FFN_CHUNK])
            b = _dot(hf, w3_ref[:, c:c + FFN_CHUNK])
            u_s[:, c:c + FFN_CHUNK] = (_silu(a) * b).astype(BF16)
            yield
        for c in range(0, x1_new.shape[1], MXU_DIM):
            x1_new[:, c:c + MXU_DIM] = (xa_ref[:, c:c + MXU_DIM]
                                        + 0.5 * _dot(u_s[...], w2_ref[:, c:c + MXU_DIM]))
            yield

    def stage_a2():
        x_in = x1_mid[...] if fused_ffn else xa_ref[...]
        h_s[...] = _rms(x_in, gmix_ref[...]).astype(BF16)
        yield
        for c in range(0, D_IN_PAD, MXU_DIM):
            c1 = min(c + MXU_DIM, D_IN_PAD)
            z_new[:, c:c1] = _dot(h_s[...], win_ref[:, c:c1])
            projected_cols[0] = c1
            yield
        g_new[...] = _dot_nt(wgt_ref[...], h_s[...])

    projected_cols = [0]

    def alternate(*gens):
        gens = list(gens)
        while gens:
            for gen in list(gens):
                try:
                    next(gen)
                    yield
                except StopIteration:
                    gens.remove(gen)

    units = alternate(stage_a1(), stage_a2()) if fused_ffn else stage_a2()

    def project_next(count=1):
        for _ in range(count):
            next(units, None)

    lane = lax.broadcasted_iota(jnp.int32, (1, LANES), 1)
    even = lane < A_DH
    if pos0 < ATT_BAND:
        col = lax.broadcasted_iota(jnp.int32, (1, band_rows), 1)
        in_stream = col >= (ATT_BAND - pos0) - t * tile

    def scores(g, head):
        lo = (head // 2) * LANES
        mine = even if head % 2 == 0 else jnp.logical_not(even)
        qh = jnp.where(mine, z_cur[span(g), lo:lo + LANES] * (LOG2E * A_DH ** -0.5), 0.0).astype(BF16)
        if cache_t:
            s_old = _dot(qh, kt_s[g, lo:lo + LANES, :]) + tab_s[head, :, 0:ATT_BAND]
            s_new = _dot_nt(qh, kband[g, ATT_BAND:band_rows, lo:lo + LANES]) + tab_s[head, :, ATT_BAND:band_rows]
            return s_old, s_new
        s = _dot_nt(qh, kband[g, :, lo:lo + LANES]) + tab_s[head]
        if pos0 < ATT_BAND:
            s = jnp.where(in_stream, s, NEG)
        return s

    def attend(g, head, s):
        lo = (head // 2) * LANES
        mine = even if head % 2 == 0 else jnp.logical_not(even)
        if cache_t:
            s_old, s_new = s
            top = jnp.maximum(jnp.max(s_old, axis=-1, keepdims=True), jnp.max(s_new, axis=-1, keepdims=True))
            e_old = jnp.exp2(s_old - top)
            e_new = jnp.exp2(s_new - top)
            o = (_dot_nt(e_old.astype(BF16), vt_s[g, lo:lo + LANES, :])
                 + _dot(e_new.astype(BF16), vband[g, ATT_BAND:band_rows, lo:lo + LANES]))
            total = jnp.sum(e_old, axis=-1, keepdims=True) + jnp.sum(e_new, axis=-1, keepdims=True)
            o = jnp.where(mine, o, 0.0) * (1.0 / total)
        else:
            vp = vband[g, :, lo:lo + LANES]
            e = jnp.exp2(s - jnp.max(s, axis=-1, keepdims=True))
            o = _dot(e.astype(BF16), jnp.where(mine, vp, jnp.zeros_like(vp)))
            o = o * (1.0 / jnp.sum(e, axis=-1, keepdims=True))
        if head % 2 == 0:
            mix_s[span(g), lo:lo + LANES] = o
        else:
            mix_s[span(g), lo:lo + LANES] += o

    pending = []
    for g in streams:
        for head in range(A_HEADS):
            pending.append((g, head, scores(g, head)))
            project_next(lag)
            if len(pending) > ahead:
                attend(*pending.pop(0))
    for item in pending:
        attend(*item)
    assert projected_cols[0] >= OFF_MQK
    ko_ref[...] = z_new[:, OFF_AK:OFF_AV]
    vo_ref[...] = z_new[:, OFF_AV:OFF_MQK]
    for g in streams:
        if n_tiles > 1:
            kband[g, 0:ATT_BAND, :] = kband[g, tile:tile + ATT_BAND, :]
            vband[g, 0:ATT_BAND, :] = vband[g, tile:tile + ATT_BAND, :]
        kband[g, ATT_BAND:ATT_BAND + tile, :] = z_new[span(g), OFF_AK:OFF_AV].astype(BF16)
        vband[g, ATT_BAND:ATT_BAND + tile, :] = z_new[span(g), OFF_AV:OFF_MQK].astype(BF16)
    z_cur[:, 0:OFF_AK] = z_new[:, 0:OFF_AK]

    project_next(UNITS_BEFORE_CONV)
    mq, mk, last_rows = [], [], []
    for g in streams:
        frames = ubuf[g]
        qk = convb_ref[...] + frames[SUBLANES:, :] * convw_ref[CONV_W - 1:CONV_W, :]
        for j in range(CONV_W - 1):
            shifted = pltpu.roll(frames, CONV_W - 1 - j, 0)
            qk = qk + shifted[SUBLANES:, :] * convw_ref[j:j + 1, :]
        qk = _silu(qk)
        mq.append(qk[:, :M_WIDTH])
        mk.append(qk[:, M_WIDTH:] * (M_DH ** -0.5))
        last_rows.append(ubuf[g, tile:tile + SUBLANES, :])
        convo_ref[g] = last_rows[g]

    project_next(UNITS_BEFORE_GATES)
    rows = group * tile
    g_col = z_cur[:, ZC_MG:ZC_MG + LANES] + gbrow_ref[...]
    g_row = g_cur[...] + gbcol_ref[...]
    lf_col = _log_sigmoid(g_col)
    lf_row = _log_sigmoid(g_row)
    b_col = sum(_dot(tri_ref[...], part) for part in _split3(lf_col))
    b_row = sum(_dot(part, trit_ref[...]) for part in _split3(lf_row))

    causal = (lax.broadcasted_iota(jnp.int32, (chunk, chunk), 1)
              <= lax.broadcasted_iota(jnp.int32, (chunk, chunk), 0))
    assert chunk == tile
    first = {}

    def block_scores(g, hd):
        lo, hi = hd * M_DH, (hd + 1) * M_DH
        cmat = c_s[g, hd]
        m_prev = nm_s[g, SUBLANES + hd:SUBLANES + hd + 1, 0:1]
        bc = b_col[span(g), M_HEADS + hd:M_HEADS + hd + 1]
        br = b_row[M_HEADS + hd:M_HEADS + hd + 1, span(g)]
        igr = g_row[hd:hd + 1, span(g)]
        dmat = jnp.where(causal, bc + (igr - br), NEG)
        inter = bc + m_prev
        mt = jnp.maximum(inter, jnp.max(dmat, axis=-1, keepdims=True))
        q = mq[g][:, lo:hi]
        k = mk[g][:, lo:hi]
        vb = z_cur[span(g), ZC_MV + lo:ZC_MV + hi].astype(BF16)
        qb = q.astype(BF16)
        n_cols = jnp.broadcast_to(nm_s[g, hd:hd + 1, :], (M_DH, M_DH)).T
        c_and_n = jnp.concatenate([cmat, n_cols], axis=1).astype(BF16)
        first[g, hd] = dict(bc=bc, mt=mt, inter=inter, m_prev=m_prev, cmat=cmat, k=k, vb=vb, dmat=dmat,
                            qk=_dot_nt(qb, k.astype(BF16)), qcn=_dot(qb, c_and_n))

    def block_output(g, hd):
        lo, hi = hd * M_DH, (hd + 1) * M_DH
        f = first[g, hd]
        s = f["qk"] * jnp.exp(f["dmat"] - f["mt"])
        iw = jnp.exp(f["inter"] - f["mt"])
        v_and_ones = jnp.concatenate([f["vb"], jnp.ones((tile, M_DH), BF16)], axis=1)
        both = iw * f["qcn"] + _dot(s.astype(BF16), v_and_ones)
        num, den = both[:, :M_DH], both[:, M_DH:]
        mh = num / jnp.maximum(jnp.abs(den), jnp.exp(-f["mt"]))
        mh = mh * _sigmoid(z_cur[span(g), ZC_MO + lo:ZC_MO + hi])
        mix_s[span(g), A_WIDTH + lo:A_WIDTH + hi] = _rms(mh, normm_ref[:, lo:hi])

    def block_state(g, hd):
        f = first[g, hd]
        bc, mt, m_prev = f["bc"], f["mt"], f["m_prev"]
        igc = g_col[span(g), hd:hd + 1]
        b_last = bc[tile - 1:tile, :]
        m_new = mt[tile - 1:tile, :]
        kw = f["k"] * jnp.exp(b_last - bc + igc - m_new)
        decay = jnp.exp(b_last + m_prev - m_new)
        c_s[g, hd] = decay * f["cmat"] + _dot(kw.T.astype(BF16), f["vb"])
        nm_s[g, hd:hd + 1, :] = decay * nm_s[g, hd:hd + 1, :] + jnp.sum(kw, axis=0, keepdims=True)
        nm_s[g, SUBLANES + hd:SUBLANES + hd + 1, :] = jnp.broadcast_to(m_new, (1, LANES))

    blocks = [(g, hd) for hd in range(M_HEADS) for g in streams]
    if group == 1:
        for blk in blocks:
            project_next(UNITS_PER_MLSTM_BLOCK)
            block_scores(*blk)
            block_output(*blk)
            block_state(*blk)
    else:
        for phase in (block_scores, block_output, block_state):
            for blk in blocks:
                project_next()
                phase(*blk)

    for _ in units:
        pass
    co_ref[...] = c_s[...]
    no_ref[...] = nm_s[:, n_rows, :]
    mo_ref[...] = nm_s[:, m_rows, :]
    resid = x1_cur[...] if fused_ffn else xb_ref[...]
    y_ref[...] = resid + _dot(mix_s[...].astype(BF16), wout_ref[...])

    if fused_ffn:
        x1_cur[...] = x1_mid[...]
        x1_mid[...] = x1_new[...]
    for g in streams:
        ubuf[g, 0:SUBLANES, :] = last_rows[g]
        ubuf[g, SUBLANES:SUBLANES + tile, :] = z_new[span(g), OFF_MQK:OFF_MV]
    z_cur[:, ZC_MV:ZC_WIDTH] = z_new[:, OFF_MV:D_IN_PAD]
    g_cur[...] = g_new[...]


def _bias_pieces(rel_bias, tile, band_rows, chunked):
    heads = rel_bias.shape[0]
    span = band_rows + tile - 1
    width = -(-span // LANES) * LANES
    n_far = ATT_BAND + tile - REL_CLIP
    n_near = width - n_far - (2 * REL_CLIP - 1)
    diag = jnp.concatenate([jnp.broadcast_to(rel_bias[:, 2 * REL_CLIP:], (heads, n_far)),
                            rel_bias[:, 2 * REL_CLIP - 1:0:-1],
                            jnp.broadcast_to(rel_bias[:, :1], (heads, n_near))], axis=1)
    i = np.arange(tile)[:, None]
    j = np.arange(band_rows)[None, :]
    visible = j < ATT_BAND + tile
    if chunked:
        qc = i // CHUNK
        kc = (j - ATT_BAND) // CHUNK
        visible = visible & (kc <= qc) & (kc >= qc - ATT_BAND // CHUNK)
    mask = jnp.asarray(np.where(visible, 0.0, NEG * LOG2E), F32)
    return diag * LOG2E, mask


def _mixer(x, params, state, *, tile, chunk, pos0, chunked, ffn=None, cache_t=False):
    nb, frames, d = x.shape
    gmix, w_in, conv_w, conv_b, gate_bias, rel_bias, norm_m, w_out = params
    n_tiles = frames // tile
    band_rows = ATT_BAND + -(-tile // LANES) * LANES
    keep_tiles = min(ATT_BAND, frames) // tile
    width2 = 2 * M_WIDTH
    fused_ffn = ffn is not None
    has_state = state is not None
    group = max(1, min(nb, STEP_ROWS // tile)) if n_tiles == 1 else 1
    assert nb % group == 0
    rows = group * tile

    win = jnp.concatenate([w_in, jnp.zeros((d, D_IN_PAD - w_in.shape[1]), F32)], axis=1).astype(BF16)
    wgt = jnp.zeros((GATE_ROWS, d), F32).at[:N_GATES].set(w_in[:, OFF_MG:].T).astype(BF16)
    gb_row = jnp.zeros((1, LANES), F32).at[0, :N_GATES].set(gate_bias)
    gb_col = jnp.zeros((GATE_ROWS, 1), F32).at[:N_GATES, 0].set(gate_bias)
    diag, mask = _bias_pieces(rel_bias, tile, band_rows, chunked)

    lag = 2 if fused_ffn else 1
    n_tiles_all = (nb // group) * n_tiles
    n_steps = n_tiles_all + lag
    entering = lambda p: jnp.minimum(p, n_tiles_all - 1)
    projected = lambda p: jnp.clip(p - (lag - 1), 0, n_tiles_all - 1)
    mixed = lambda p: jnp.maximum(p - lag, 0)
    per_group = lambda *dims: pl.BlockSpec((group,) + dims,
                                           lambda p: (lax.div(mixed(p), n_tiles),) + (0,) * len(dims))
    enter_spec = pl.BlockSpec((rows, d), lambda p: (entering(p), 0))
    mix_spec = pl.BlockSpec((rows, d), lambda p: (mixed(p), 0))

    def kv_index(p):
        q = projected(p)
        return (lax.div(q, n_tiles) * keep_tiles + jnp.maximum(lax.rem(q, n_tiles) - (n_tiles - keep_tiles), 0), 0)

    kv_spec = pl.BlockSpec((rows, A_WIDTH), kv_index)

    x2d = x.reshape(nb * frames, d)
    args, in_specs = [x2d], [enter_spec]
    scratch_ffn = []
    if fused_ffn:
        g1, w1, w3, w2 = ffn
        args += [g1.reshape(1, d), w1, w3, w2]
        in_specs += [_resident((1, d)), _resident(w1.shape), _resident(w3.shape), _resident(w2.shape)]
        scratch_ffn = [pltpu.VMEM((rows, w1.shape[1]), BF16), pltpu.VMEM((rows, d), BF16)] + [pltpu.VMEM((rows, d), F32)] * 3
    else:
        args.append(x2d)
        in_specs.append(mix_spec)
    args += [gmix.reshape(1, d), win, wgt, conv_w, conv_b.reshape(1, width2), gb_row, gb_col, diag, mask,
             norm_m.reshape(1, M_WIDTH), w_out.astype(BF16)]
    in_specs += [_resident((1, d)), _resident(win.shape), _resident(wgt.shape),
                 _resident((CONV_W, width2)), _resident((1, width2)), _resident((1, LANES)),
                 _resident((GATE_ROWS, 1)), _resident(diag.shape), _resident(mask.shape), _resident((1, M_WIDTH)),
                 _resident((d, d))]
    frame = np.arange(rows)
    lower = (frame[:, None] // chunk == frame[None, :] // chunk) & (frame[None, :] <= frame[:, None])
    args += [jnp.asarray(lower, BF16), jnp.asarray(lower.T, BF16)]
    in_specs += [_resident((rows, rows)), _resident((rows, rows))]
    if has_state:
        k0, v0, c0, n0, m0, conv0 = state
        conv0p = jnp.concatenate([jnp.zeros((nb, SUBLANES - (CONV_W - 1), width2), F32), conv0], axis=1)
        m0p = jnp.broadcast_to(m0[:, :, None], (nb, M_HEADS, LANES))
        args += [k0, v0, c0, n0, m0p, conv0p]
        in_specs += [per_group(ATT_BAND, A_WIDTH), per_group(ATT_BAND, A_WIDTH),
                     per_group(M_HEADS, M_DH, M_DH), per_group(M_HEADS, M_DH), per_group(M_HEADS, LANES),
                     per_group(SUBLANES, width2)]
    out_specs = [
        mix_spec, kv_spec, kv_spec, per_group(M_HEADS, M_DH, M_DH), per_group(M_HEADS, M_DH),
        per_group(M_HEADS, LANES), per_group(SUBLANES, width2),
    ]
    keep = keep_tiles * tile
    out_shape = [
        jax.ShapeDtypeStruct((nb * frames, d), F32),
        jax.ShapeDtypeStruct((nb * keep, A_WIDTH), F32),
        jax.ShapeDtypeStruct((nb * keep, A_WIDTH), F32),
        jax.ShapeDtypeStruct((nb, M_HEADS, M_DH, M_DH), F32),
        jax.ShapeDtypeStruct((nb, M_HEADS, M_DH), F32),
        jax.ShapeDtypeStruct((nb, M_HEADS, LANES), F32),
        jax.ShapeDtypeStruct((nb, SUBLANES, width2), F32),
    ]
    scratch = [
        pltpu.VMEM((rows, D_IN_PAD), F32), pltpu.VMEM((rows, ZC_WIDTH), F32),
        pltpu.VMEM((GATE_ROWS, rows), F32), pltpu.VMEM((GATE_ROWS, rows), F32),
        pltpu.VMEM((group, band_rows, A_WIDTH), BF16), pltpu.VMEM((group, band_rows, A_WIDTH), BF16),
        pltpu.VMEM((group, tile + SUBLANES, width2), F32),
        pltpu.VMEM((group, M_HEADS, M_DH, M_DH), F32),
        pltpu.VMEM((group, NM_ROWS, LANES), F32),
        pltpu.VMEM((rows, d), F32), pltpu.VMEM((rows, d), BF16),
        pltpu.VMEM((A_HEADS, tile, band_rows), F32),
    ] + scratch_ffn
    if cache_t:
        scratch += [pltpu.VMEM((group, A_WIDTH, ATT_BAND), BF16)] * 2
    y, ko, vo, c1, n1, m1, conv1 = pl.pallas_call(
        functools.partial(_mixer_kernel, tile=tile, group=group, chunk=chunk, band_rows=band_rows, pos0=pos0,
                          n_tiles=n_tiles, fused_ffn=fused_ffn, has_state=has_state, cache_t=cache_t,
                          ahead=PROMPT_AHEAD if group == 1 else ATTENTION_AHEAD),
        grid=(n_steps,),
        in_specs=in_specs,
        out_specs=out_specs,
        out_shape=out_shape,
        scratch_shapes=scratch,
        compiler_params=pltpu.CompilerParams(dimension_semantics=("arbitrary",),
                                             vmem_limit_bytes=VMEM_LIMIT_BYTES),
        name="mixer_chunked" if chunked else "mixer_step",
    )(*args)
    new_state = (ko.reshape(nb, keep, A_HEADS, A_DH), vo.reshape(nb, keep, A_HEADS, A_DH), c1, n1,
                 m1[:, :, 0], conv1[:, SUBLANES - (CONV_W - 1):, :])
    return y.reshape(nb, frames, d), new_state


def kernel(x_prompt, x_sample, cache_attn_k, cache_attn_v, state_mlstm_C, state_mlstm_n, state_mlstm_m, state_mlstm_conv, norm_ffn1, w1_ffn1, w3_ffn1, w2_ffn1, norm_mix, w_in, conv_w, conv_b, gate_bias, rel_bias, norm_mlstm_out, w_out, norm_ffn2, w1_ffn2, w3_ffn2, w2_ffn2, norm_final):
    depth = norm_ffn1.shape[0]
    nbp, seq, d = x_prompt.shape
    nbs, dec, _ = x_sample.shape
    xp = x_prompt
    xs = x_sample.reshape(nbs * dec, d)
    prompt_tile = min(PROMPT_TILE, seq)
    new_p, new_s = [], []
    for l in range(depth):
        last = l == depth - 1
        ffn1 = _ffn_weights(w1_ffn1[l], w3_ffn1[l], w2_ffn1[l])
        ffn2 = _ffn_weights(w1_ffn2[l], w3_ffn2[l], w2_ffn2[l])
        mix = (norm_mix[l], w_in[l], conv_w[l], conv_b[l], gate_bias[l], rel_bias[l], norm_mlstm_out[l], w_out[l])
        gf = norm_final if last else None

        xp, st = _mixer(xp, mix, None, tile=prompt_tile, chunk=prompt_tile, pos0=0, chunked=True,
                        ffn=(norm_ffn1[l],) + ffn1)
        new_p.append(st)
        xp = _ffn(xp.reshape(nbp * seq, d), norm_ffn2[l], ffn2, gf).reshape(nbp, seq, d)

        xs = _ffn(xs, norm_ffn1[l], ffn1)
        feature_major = lambda c: c.transpose(0, 2, 3, 1).reshape(nbs, A_WIDTH, -1)
        cache = (feature_major(cache_attn_k[l]), feature_major(cache_attn_v[l]),
                 state_mlstm_C[l], state_mlstm_n[l], state_mlstm_m[l], state_mlstm_conv[l])
        xs3, st = _mixer(xs.reshape(nbs, dec, d), mix, cache, tile=dec, chunk=dec, pos0=PAST_LEN, chunked=False,
                         cache_t=True)
        new_s.append(st)
        xs = _ffn(xs3.reshape(nbs * dec, d), norm_ffn2[l], ffn2, gf)

    stack = lambda states, i: jnp.stack([s[i] for s in states])
    return ((xp, xs.reshape(nbs, dec, d))
            + tuple(stack(new_p, i) for i in range(6)) + tuple(stack(new_s, i) for i in range(6)))
```

```python
import functools

import numpy as np
import jax
import jax.numpy as jnp
from jax import lax
from jax.experimental import pallas as pl
from jax.experimental.pallas import tpu as pltpu

F32 = jnp.float32
BF16 = jnp.bfloat16

CHUNK = 64
ATT_BAND = 8 * CHUNK
A_HEADS = 8
A_DH = 64
A_WIDTH = A_HEADS * A_DH
M_HEADS = 4
M_DH = 128
M_WIDTH = M_HEADS * M_DH
REL_CLIP = 128
CONV_W = 4
PAST_LEN = 4096
EPS = 1e-6
NEG = -1e30
LOG2E = 1.4426950408889634

LANES = 128
SUBLANES = 8
MXU_DIM = 256
VMEM_LIMIT_BYTES = 60 * 1024 * 1024

OFF_AK = A_WIDTH
OFF_AV = 2 * A_WIDTH
OFF_MQK = 3 * A_WIDTH
OFF_MV = OFF_MQK + 2 * M_WIDTH
OFF_MO = OFF_MV + M_WIDTH
OFF_MG = OFF_MO + M_WIDTH
N_GATES = 2 * M_HEADS
D_IN_PAD = OFF_MG + LANES
GATE_ROWS = 16
ZC_MV = A_WIDTH
ZC_MO = ZC_MV + M_WIDTH
ZC_MG = ZC_MO + M_WIDTH
ZC_WIDTH = ZC_MG + LANES
NM_ROWS = 32

FFN_ROWS = 512
FFN_CHUNK = MXU_DIM
CAST_STEPS = 8
PROMPT_TILE = 256
STEP_ROWS = 128
ATTENTION_AHEAD = 4
PROMPT_AHEAD = 2
UNITS_BEFORE_CONV = 2
UNITS_BEFORE_GATES = 2
UNITS_PER_MLSTM_BLOCK = 3


def _rms(x, g):
    return x * lax.rsqrt(jnp.mean(x * x, axis=-1, keepdims=True) + EPS) * g


def _sigmoid(x):
    return 1.0 / (1.0 + jnp.exp(-x))


def _silu(x):
    h = 0.5 * x
    return h + h * jnp.tanh(h)


def _log_sigmoid(x):
    return jnp.minimum(x, 0.0) - jnp.log1p(jnp.exp(-jnp.abs(x)))


def _dot(a, b):
    return jnp.dot(a, b, preferred_element_type=F32)


def _dot_nt(a, b):
    return lax.dot_general(a, b, (((1,), (1,)), ((), ())), preferred_element_type=F32)


def _split3(x):
    p1 = x.astype(BF16)
    r1 = x - p1.astype(F32)
    p2 = r1.astype(BF16)
    p3 = (r1 - p2.astype(F32)).astype(BF16)
    return p1, p2, p3


def _resident(shape):
    nd = len(shape)
    return pl.BlockSpec(shape, lambda *_: (0,) * nd, pipeline_mode=pl.Buffered(1))


def _ffn_kernel(x_ref, g_ref, w1_ref, w3_ref, w2_ref, *rest, final_norm):
    if final_norm:
        gf_ref, o_ref, h_ref, u_ref = rest
    else:
        o_ref, h_ref, u_ref = rest
    h_ref[...] = _rms(x_ref[...], g_ref[...]).astype(BF16)
    for c in range(0, u_ref.shape[1], FFN_CHUNK):
        h = h_ref[...]
        a = _dot(h, w1_ref[:, c:c + FFN_CHUNK])
        b = _dot(h, w3_ref[:, c:c + FFN_CHUNK])
        u_ref[:, c:c + FFN_CHUNK] = (_silu(a) * b).astype(BF16)
    y = x_ref[...] + 0.5 * _dot(u_ref[...], w2_ref[...])
    if final_norm:
        y = _rms(y, gf_ref[...])
    o_ref[...] = y


def _cast_kernel(*refs):
    n = len(refs) // 2
    for src, dst in zip(refs[:n], refs[n:]):
        dst[...] = src[...].astype(BF16)


def _ffn_weights(*mats):
    assert all(m.shape[0] % (CAST_STEPS * 2 * SUBLANES) == 0 for m in mats)
    specs = [pl.BlockSpec((m.shape[0] // CAST_STEPS, m.shape[1]), lambda i: (i, 0)) for m in mats]
    return tuple(pl.pallas_call(
        _cast_kernel,
        grid=(CAST_STEPS,),
        in_specs=specs,
        out_specs=specs,
        out_shape=[jax.ShapeDtypeStruct(m.shape, BF16) for m in mats],
        compiler_params=pltpu.CompilerParams(dimension_semantics=("arbitrary",)),
        name="cast_weights",
    )(*mats))


def _ffn(x2d, g, weights, gf=None):
    n, d = x2d.shape
    w1, w3, w2 = weights
    f = w1.shape[1]
    assert f % FFN_CHUNK == 0
    rows = min(FFN_ROWS, n)
    final_norm = gf is not None
    row_spec = pl.BlockSpec((rows, d), lambda i: (i, 0))
    in_specs = [row_spec, _resident((1, d)), _resident(w1.shape), _resident(w3.shape), _resident(w2.shape)]
    args = [x2d, g.reshape(1, d), w1, w3, w2]
    if final_norm:
        in_specs.append(_resident((1, d)))
        args.append(gf.reshape(1, d))
    return pl.pallas_call(
        functools.partial(_ffn_kernel, final_norm=final_norm),
        grid=(n // rows,),
        in_specs=in_specs,
        out_specs=row_spec,
        out_shape=jax.ShapeDtypeStruct((n, d), F32),
        scratch_shapes=[pltpu.VMEM((rows, d), BF16), pltpu.VMEM((rows, f), BF16)],
        compiler_params=pltpu.CompilerParams(dimension_semantics=("arbitrary",),
                                             vmem_limit_bytes=VMEM_LIMIT_BYTES),
        name="ffn_final" if final_norm else "ffn",
    )(*args)


def _mixer_kernel(*refs, tile, group, chunk, band_rows, pos0, n_tiles, fused_ffn, has_state, cache_t, ahead):
    refs = iter(refs)
    take = lambda n: [next(refs) for _ in range(n)]
    (xa_ref,) = take(1)
    if fused_ffn:
        g1_ref, w1_ref, w3_ref, w2_ref = take(4)
    else:
        (xb_ref,) = take(1)
    gmix_ref, win_ref, wgt_ref, convw_ref, convb_ref, gbrow_ref, gbcol_ref, diag_ref, mask_ref, normm_ref, wout_ref = take(11)
    if has_state:
        k0_ref, v0_ref, c0_ref, n0_ref, m0_ref, conv0_ref = take(6)
    y_ref, ko_ref, vo_ref, co_ref, no_ref, mo_ref, convo_ref = take(7)
    z_new, z_cur, g_new, g_cur, kband, vband, ubuf, c_s, nm_s, mix_s, h_s, tab_s = take(12)
    n_rows, m_rows = slice(0, M_HEADS), slice(SUBLANES, SUBLANES + M_HEADS)
    if fused_ffn:
        u_s, h1_s, x1_new, x1_mid, x1_cur = take(5)
    if cache_t:
        kt_s, vt_s = take(2)
        assert has_state and n_tiles == 1 and pos0 >= ATT_BAND
    assert group == 1 or n_tiles == 1
    streams = range(group)
    span = lambda g: slice(g * tile, (g + 1) * tile)

    p = pl.program_id(0)
    lag = 2 if fused_ffn else 1
    t = lax.rem(jnp.maximum(p - lag, 0), n_tiles)

    @pl.when(p == 0)
    def _first_step():
        z_cur[...] = jnp.zeros_like(z_cur)
        g_cur[...] = jnp.zeros_like(g_cur)
        for g in streams:
            kband[g, ATT_BAND:band_rows, :] = jnp.zeros((band_rows - ATT_BAND, A_WIDTH), BF16)
            vband[g, ATT_BAND:band_rows, :] = jnp.zeros((band_rows - ATT_BAND, A_WIDTH), BF16)
            ubuf[g, SUBLANES:SUBLANES + tile, :] = jnp.zeros((tile, 2 * M_WIDTH), F32)
        if fused_ffn:
            x1_cur[...] = jnp.zeros_like(x1_cur)
            x1_mid[...] = jnp.zeros_like(x1_mid)
        width = diag_ref.shape[1]
        for head in range(A_HEADS):
            rows_of_diag = jnp.broadcast_to(diag_ref[head:head + 1, :], (tile, width))
            skew = pltpu.roll(rows_of_diag, width - (tile - 1), 1, stride=1, stride_axis=0)
            tab_s[head] = skew[:, :band_rows] + mask_ref[...]

    @pl.when(t == 0)
    def _load_state():
        if has_state:
            for g in streams:
                if cache_t:
                    kt_s[g] = k0_ref[g].astype(BF16)
                    vt_s[g] = v0_ref[g].astype(BF16)
                else:
                    kband[g, 0:ATT_BAND, :] = k0_ref[g].astype(BF16)
                    vband[g, 0:ATT_BAND, :] = v0_ref[g].astype(BF16)
                ubuf[g, 0:SUBLANES, :] = conv0_ref[g]
            c_s[...] = c0_ref[...]
            nm_s[:, n_rows, :] = n0_ref[...]
            nm_s[:, m_rows, :] = m0_ref[...]
        else:
            for g in streams:
                kband[g, 0:ATT_BAND, :] = jnp.zeros((ATT_BAND, A_WIDTH), BF16)
                vband[g, 0:ATT_BAND, :] = jnp.zeros((ATT_BAND, A_WIDTH), BF16)
                ubuf[g, 0:SUBLANES, :] = jnp.zeros((SUBLANES, 2 * M_WIDTH), F32)
            c_s[...] = jnp.zeros_like(c_s)
            nm_s[...] = jnp.zeros_like(nm_s)

    def stage_a1():
        h1_s[...] = _rms(xa_ref[...], g1_ref[...]).astype(BF16)
        yield
        for c in range(0, u_s.shape[1], FFN_CHUNK):
            hf = h1_s[...]
            a = _dot(hf, w1_ref[:, c:c + FFN_CHUNK])
            b = _dot(hf, w3_ref[:, c:c + FFN_CHUNK])
            u_s[:, c:c + FFN_CHUNK] = (_silu(a) * b).astype(BF16)
            yield
        for c in range(0, x1_new.shape[1], MXU_DIM):
            x1_new[:, c:c + MXU_DIM] = (xa_ref[:, c:c + MXU_DIM]
                                        + 0.5 * _dot(u_s[...], w2_ref[:, c:c + MXU_DIM]))
            yield

    def stage_a2():
        x_in = x1_mid[...] if fused_ffn else xa_ref[...]
        h_s[...] = _rms(x_in, gmix_ref[...]).astype(BF16)
        yield
        for c in range(0, D_IN_PAD, MXU_DIM):
            c1 = min(c + MXU_DIM, D_IN_PAD)
            z_new[:, c:c1] = _dot(h_s[...], win_ref[:, c:c1])
            projected_cols[0] = c1
            yield
        g_new[...] = _dot_nt(wgt_ref[...], h_s[...])

    projected_cols = [0]

    def alternate(*gens):
        gens = list(gens)
        while gens:
            for gen in list(gens):
                try:
                    next(gen)
                    yield
                except StopIteration:
                    gens.remove(gen)

    units = alternate(stage_a1(), stage_a2()) if fused_ffn else stage_a2()

    def project_next(count=1):
        for _ in range(count):
            next(units, None)

    lane = lax.broadcasted_iota(jnp.int32, (1, LANES), 1)
    even = lane < A_DH
    if pos0 < ATT_BAND:
        col = lax.broadcasted_iota(jnp.int32, (1, band_rows), 1)
        in_stream = col >= (ATT_BAND - pos0) - t * tile

    def scores(g, head):
        lo = (head // 2) * LANES
        mine = even if head % 2 == 0 else jnp.logical_not(even)
        qh = jnp.where(mine, z_cur[span(g), lo:lo + LANES] * (LOG2E * A_DH ** -0.5), 0.0).astype(BF16)
        if cache_t:
            s_old = _dot(qh, kt_s[g, lo:lo + LANES, :]) + tab_s[head, :, 0:ATT_BAND]
            s_new = _dot_nt(qh, kband[g, ATT_BAND:band_rows, lo:lo + LANES]) + tab_s[head, :, ATT_BAND:band_rows]
            return s_old, s_new
        s = _dot_nt(qh, kband[g, :, lo:lo + LANES]) + tab_s[head]
        if pos0 < ATT_BAND:
            s = jnp.where(in_stream, s, NEG)
        return s

    def attend(g, head, s):
        lo = (head // 2) * LANES
        mine = even if head % 2 == 0 else jnp.logical_not(even)
        if cache_t:
            s_old, s_new = s
            top = jnp.maximum(jnp.max(s_old, axis=-1, keepdims=True), jnp.max(s_new, axis=-1, keepdims=True))
            e_old = jnp.exp2(s_old - top)
            e_new = jnp.exp2(s_new - top)
            o = (_dot_nt(e_old.astype(BF16), vt_s[g, lo:lo + LANES, :])
                 + _dot(e_new.astype(BF16), vband[g, ATT_BAND:band_rows, lo:lo + LANES]))
            total = jnp.sum(e_old, axis=-1, keepdims=True) + jnp.sum(e_new, axis=-1, keepdims=True)
            o = jnp.where(mine, o, 0.0) * (1.0 / total)
        else:
            vp = vband[g, :, lo:lo + LANES]
            e = jnp.exp2(s - jnp.max(s, axis=-1, keepdims=True))
            o = _dot(e.astype(BF16), jnp.where(mine, vp, jnp.zeros_like(vp)))
            o = o * (1.0 / jnp.sum(e, axis=-1, keepdims=True))
        if head % 2 == 0:
            mix_s[span(g), lo:lo + LANES] = o
        else:
            mix_s[span(g), lo:lo + LANES] += o

    pending = []
    for g in streams:
        for head in range(A_HEADS):
            pending.append((g, head, scores(g, head)))
            project_next(lag)
            if len(pending) > ahead:
                attend(*pending.pop(0))
    for item in pending:
        attend(*item)
    assert projected_cols[0] >= OFF_MQK
    ko_ref[...] = z_new[:, OFF_AK:OFF_AV]
    vo_ref[...] = z_new[:, OFF_AV:OFF_MQK]
    for g in streams:
        if n_tiles > 1:
            kband[g, 0:ATT_BAND, :] = kband[g, tile:tile + ATT_BAND, :]
            vband[g, 0:ATT_BAND, :] = vband[g, tile:tile + ATT_BAND, :]
        kband[g, ATT_BAND:ATT_BAND + tile, :] = z_new[span(g), OFF_AK:OFF_AV].astype(BF16)
        vband[g, ATT_BAND:ATT_BAND + tile, :] = z_new[span(g), OFF_AV:OFF_MQK].astype(BF16)
    z_cur[:, 0:OFF_AK] = z_new[:, 0:OFF_AK]

    project_next(UNITS_BEFORE_CONV)
    mq, mk, last_rows = [], [], []
    for g in streams:
        frames = ubuf[g]
        qk = convb_ref[...] + frames[SUBLANES:, :] * convw_ref[CONV_W - 1:CONV_W, :]
        for j in range(CONV_W - 1):
            shifted = pltpu.roll(frames, CONV_W - 1 - j, 0)
            qk = qk + shifted[SUBLANES:, :] * convw_ref[j:j + 1, :]
        qk = _silu(qk)
        mq.append(qk[:, :M_WIDTH])
        mk.append(qk[:, M_WIDTH:] * (M_DH ** -0.5))
        last_rows.append(ubuf[g, tile:tile + SUBLANES, :])
        convo_ref[g] = last_rows[g]

    project_next(UNITS_BEFORE_GATES)
    rows = group * tile
    g_col = z_cur[:, ZC_MG:ZC_MG + LANES] + gbrow_ref[...]
    g_row = g_cur[...] + gbcol_ref[...]
    lf_col = _log_sigmoid(g_col)
    lf_row = _log_sigmoid(g_row)
    ri = lax.broadcasted_iota(jnp.int32, (rows, rows), 0)
    ci = lax.broadcasted_iota(jnp.int32, (rows, rows), 1)
    assert chunk & (chunk - 1) == 0
    chunk_shift = chunk.bit_length() - 1
    same_chunk = (ri >> chunk_shift) == (ci >> chunk_shift)
    tri = jnp.where(jnp.logical_and(same_chunk, ci <= ri), 1.0, 0.0).astype(BF16)
    tri_t = jnp.where(jnp.logical_and(same_chunk, ri <= ci), 1.0, 0.0).astype(BF16)
    b_col = sum(_dot(tri, part) for part in _split3(lf_col))
    b_row = sum(_dot(part, tri_t) for part in _split3(lf_row))

    causal = (lax.broadcasted_iota(jnp.int32, (chunk, chunk), 1)
              <= lax.broadcasted_iota(jnp.int32, (chunk, chunk), 0))
    assert chunk == tile
    first = {}

    def block_scores(g, hd):
        lo, hi = hd * M_DH, (hd + 1) * M_DH
        cmat = c_s[g, hd]
        m_prev = nm_s[g, SUBLANES + hd:SUBLANES + hd + 1, 0:1]
        bc = b_col[span(g), M_HEADS + hd:M_HEADS + hd + 1]
        br = b_row[M_HEADS + hd:M_HEADS + hd + 1, span(g)]
        igr = g_row[hd:hd + 1, span(g)]
        dmat = jnp.where(causal, bc + (igr - br), NEG)
        inter = bc + m_prev
        mt = jnp.maximum(inter, jnp.max(dmat, axis=-1, keepdims=True))
        q = mq[g][:, lo:hi]
        k = mk[g][:, lo:hi]
        vb = z_cur[span(g), ZC_MV + lo:ZC_MV + hi].astype(BF16)
        qb = q.astype(BF16)
        n_cols = jnp.broadcast_to(nm_s[g, hd:hd + 1, :], (M_DH, M_DH)).T
        c_and_n = jnp.concatenate([cmat, n_cols], axis=1).astype(BF16)
        first[g, hd] = dict(bc=bc, mt=mt, inter=inter, m_prev=m_prev, cmat=cmat, k=k, vb=vb, dmat=dmat,
                            qk=_dot_nt(qb, k.astype(BF16)), qcn=_dot(qb, c_and_n))

    def block_output(g, hd):
        lo, hi = hd * M_DH, (hd + 1) * M_DH
        f = first[g, hd]
        s = f["qk"] * jnp.exp(f["dmat"] - f["mt"])
        iw = jnp.exp(f["inter"] - f["mt"])
        v_and_ones = jnp.concatenate([f["vb"], jnp.ones((tile, M_DH), BF16)], axis=1)
        both = iw * f["qcn"] + _dot(s.astype(BF16), v_and_ones)
        num, den = both[:, :M_DH], both[:, M_DH:]
        mh = num / jnp.maximum(jnp.abs(den), jnp.exp(-f["mt"]))
        mh = mh * _sigmoid(z_cur[span(g), ZC_MO + lo:ZC_MO + hi])
        mix_s[span(g), A_WIDTH + lo:A_WIDTH + hi] = _rms(mh, normm_ref[:, lo:hi])

    def block_state(g, hd):
        f = first[g, hd]
        bc, mt, m_prev = f["bc"], f["mt"], f["m_prev"]
        igc = g_col[span(g), hd:hd + 1]
        b_last = bc[tile - 1:tile, :]
        m_new = mt[tile - 1:tile, :]
        kw = f["k"] * jnp.exp(b_last - bc + igc - m_new)
        decay = jnp.exp(b_last + m_prev - m_new)
        c_s[g, hd] = decay * f["cmat"] + _dot(kw.T.astype(BF16), f["vb"])
        nm_s[g, hd:hd + 1, :] = decay * nm_s[g, hd:hd + 1, :] + jnp.sum(kw, axis=0, keepdims=True)
        nm_s[g, SUBLANES + hd:SUBLANES + hd + 1, :] = jnp.broadcast_to(m_new, (1, LANES))

    blocks = [(g, hd) for hd in range(M_HEADS) for g in streams]
    if group == 1:
        for blk in blocks:
            project_next(UNITS_PER_MLSTM_BLOCK)
            block_scores(*blk)
            block_output(*blk)
            block_state(*blk)
    else:
        for phase in (block_scores, block_output, block_state):
            for blk in blocks:
                project_next()
                phase(*blk)

    for _ in units:
        pass
    co_ref[...] = c_s[...]
    no_ref[...] = nm_s[:, n_rows, :]
    mo_ref[...] = nm_s[:, m_rows, :]
    resid = x1_cur[...] if fused_ffn else xb_ref[...]
    y_ref[...] = resid + _dot(mix_s[...].astype(BF16), wout_ref[...])

    if fused_ffn:
        x1_cur[...] = x1_mid[...]
        x1_mid[...] = x1_new[...]
    for g in streams:
        ubuf[g, 0:SUBLANES, :] = last_rows[g]
        ubuf[g, SUBLANES:SUBLANES + tile, :] = z_new[span(g), OFF_MQK:OFF_MV]
    z_cur[:, ZC_MV:ZC_WIDTH] = z_new[:, OFF_MV:D_IN_PAD]
    g_cur[...] = g_new[...]


def _bias_pieces(rel_bias, tile, band_rows, chunked):
    heads = rel_bias.shape[0]
    span = band_rows + tile - 1
    width = -(-span // LANES) * LANES
    n_far = ATT_BAND + tile - REL_CLIP
    n_near = width - n_far - (2 * REL_CLIP - 1)
    diag = jnp.concatenate([jnp.broadcast_to(rel_bias[:, 2 * REL_CLIP:], (heads, n_far)),
                            rel_bias[:, 2 * REL_CLIP - 1:0:-1],
                            jnp.broadcast_to(rel_bias[:, :1], (heads, n_near))], axis=1)
    i = np.arange(tile)[:, None]
    j = np.arange(band_rows)[None, :]
    visible = j < ATT_BAND + tile
    if chunked:
        qc = i // CHUNK
        kc = (j - ATT_BAND) // CHUNK
        visible = visible & (kc <= qc) & (kc >= qc - ATT_BAND // CHUNK)
    mask = jnp.asarray(np.where(visible, 0.0, NEG * LOG2E), F32)
    return diag * LOG2E, mask


def _mixer(x, params, state, *, tile, chunk, pos0, chunked, ffn=None, cache_t=False):
    nb, frames, d = x.shape
    gmix, w_in, conv_w, conv_b, gate_bias, rel_bias, norm_m, w_in_b, w_out_b = params
    n_tiles = frames // tile
    band_rows = ATT_BAND + -(-tile // LANES) * LANES
    keep_tiles = min(ATT_BAND, frames) // tile
    width2 = 2 * M_WIDTH
    fused_ffn = ffn is not None
    has_state = state is not None
    group = max(1, min(nb, STEP_ROWS // tile)) if n_tiles == 1 else 1
    assert nb % group == 0
    rows = group * tile

    win = jnp.concatenate([w_in_b, jnp.zeros((d, D_IN_PAD - w_in.shape[1]), BF16)], axis=1)
    wgt = jnp.zeros((GATE_ROWS, d), F32).at[:N_GATES].set(w_in[:, OFF_MG:].T).astype(BF16)
    gb_row = jnp.zeros((1, LANES), F32).at[0, :N_GATES].set(gate_bias)
    gb_col = jnp.zeros((GATE_ROWS, 1), F32).at[:N_GATES, 0].set(gate_bias)
    diag, mask = _bias_pieces(rel_bias, tile, band_rows, chunked)

    lag = 2 if fused_ffn else 1
    n_tiles_all = (nb // group) * n_tiles
    n_steps = n_tiles_all + lag
    entering = lambda p: jnp.minimum(p, n_tiles_all - 1)
    projected = lambda p: jnp.clip(p - (lag - 1), 0, n_tiles_all - 1)
    mixed = lambda p: jnp.maximum(p - lag, 0)
    per_group = lambda *dims: pl.BlockSpec((group,) + dims,
                                           lambda p: (lax.div(mixed(p), n_tiles),) + (0,) * len(dims))
    enter_spec = pl.BlockSpec((rows, d), lambda p: (entering(p), 0))
    mix_spec = pl.BlockSpec((rows, d), lambda p: (mixed(p), 0))

    def kv_index(p):
        q = projected(p)
        return (lax.div(q, n_tiles) * keep_tiles + jnp.maximum(lax.rem(q, n_tiles) - (n_tiles - keep_tiles), 0), 0)

    kv_spec = pl.BlockSpec((rows, A_WIDTH), kv_index)

    x2d = x.reshape(nb * frames, d)
    args, in_specs = [x2d], [enter_spec]
    scratch_ffn = []
    if fused_ffn:
        g1, w1, w3, w2 = ffn
        args += [g1.reshape(1, d), w1, w3, w2]
        in_specs += [_resident((1, d)), _resident(w1.shape), _resident(w3.shape), _resident(w2.shape)]
        scratch_ffn = [pltpu.VMEM((rows, w1.shape[1]), BF16), pltpu.VMEM((rows, d), BF16)] + [pltpu.VMEM((rows, d), F32)] * 3
    else:
        args.append(x2d)
        in_specs.append(mix_spec)
    args += [gmix.reshape(1, d), win, wgt, conv_w, conv_b.reshape(1, width2), gb_row, gb_col, diag, mask,
             norm_m.reshape(1, M_WIDTH), w_out_b]
    in_specs += [_resident((1, d)), _resident(win.shape), _resident(wgt.shape),
                 _resident((CONV_W, width2)), _resident((1, width2)), _resident((1, LANES)),
                 _resident((GATE_ROWS, 1)), _resident(diag.shape), _resident(mask.shape), _resident((1, M_WIDTH)),
                 _resident((d, d))]
    if has_state:
        k0, v0, c0, n0, m0, conv0 = state
        conv0p = jnp.concatenate([jnp.zeros((nb, SUBLANES - (CONV_W - 1), width2), F32), conv0], axis=1)
        m0p = jnp.broadcast_to(m0[:, :, None], (nb, M_HEADS, LANES))
        args += [k0, v0, c0, n0, m0p, conv0p]
        in_specs += [per_group(ATT_BAND, A_WIDTH), per_group(ATT_BAND, A_WIDTH),
                     per_group(M_HEADS, M_DH, M_DH), per_group(M_HEADS, M_DH), per_group(M_HEADS, LANES),
                     per_group(SUBLANES, width2)]
    out_specs = [
        mix_spec, kv_spec, kv_spec, per_group(M_HEADS, M_DH, M_DH), per_group(M_HEADS, M_DH),
        per_group(M_HEADS, LANES), per_group(SUBLANES, width2),
    ]
    keep = keep_tiles * tile
    out_shape = [
        jax.ShapeDtypeStruct((nb * frames, d), F32),
        jax.ShapeDtypeStruct((nb * keep, A_WIDTH), F32),
        jax.ShapeDtypeStruct((nb * keep, A_WIDTH), F32),
        jax.ShapeDtypeStruct((nb, M_HEADS, M_DH, M_DH), F32),
        jax.ShapeDtypeStruct((nb, M_HEADS, M_DH), F32),
        jax.ShapeDtypeStruct((nb, M_HEADS, LANES), F32),
        jax.ShapeDtypeStruct((nb, SUBLANES, width2), F32),
    ]
    scratch = [
        pltpu.VMEM((rows, D_IN_PAD), F32), pltpu.VMEM((rows, ZC_WIDTH), F32),
        pltpu.VMEM((GATE_ROWS, rows), F32), pltpu.VMEM((GATE_ROWS, rows), F32),
        pltpu.VMEM((group, band_rows, A_WIDTH), BF16), pltpu.VMEM((group, band_rows, A_WIDTH), BF16),
        pltpu.VMEM((group, tile + SUBLANES, width2), F32),
        pltpu.VMEM((group, M_HEADS, M_DH, M_DH), F32),
        pltpu.VMEM((group, NM_ROWS, LANES), F32),
        pltpu.VMEM((rows, d), F32), pltpu.VMEM((rows, d), BF16),
        pltpu.VMEM((A_HEADS, tile, band_rows), F32),
    ] + scratch_ffn
    if cache_t:
        scratch += [pltpu.VMEM((group, A_WIDTH, ATT_BAND), BF16)] * 2
    y, ko, vo, c1, n1, m1, conv1 = pl.pallas_call(
        functools.partial(_mixer_kernel, tile=tile, group=group, chunk=chunk, band_rows=band_rows, pos0=pos0,
                          n_tiles=n_tiles, fused_ffn=fused_ffn, has_state=has_state, cache_t=cache_t,
                          ahead=PROMPT_AHEAD if group == 1 else ATTENTION_AHEAD),
        grid=(n_steps,),
        in_specs=in_specs,
        out_specs=out_specs,
        out_shape=out_shape,
        scratch_shapes=scratch,
        compiler_params=pltpu.CompilerParams(dimension_semantics=("arbitrary",),
                                             vmem_limit_bytes=VMEM_LIMIT_BYTES),
        name="mixer_chunked" if chunked else "mixer_step",
    )(*args)
    new_state = (ko.reshape(nb, keep, A_HEADS, A_DH), vo.reshape(nb, keep, A_HEADS, A_DH), c1, n1,
                 m1[:, :, 0], conv1[:, SUBLANES - (CONV_W - 1):, :])
    return y.reshape(nb, frames, d), new_state


def kernel(x_prompt, x_sample, cache_attn_k, cache_attn_v, state_mlstm_C, state_mlstm_n, state_mlstm_m, state_mlstm_conv, norm_ffn1, w1_ffn1, w3_ffn1, w2_ffn1, norm_mix, w_in, conv_w, conv_b, gate_bias, rel_bias, norm_mlstm_out, w_out, norm_ffn2, w1_ffn2, w3_ffn2, w2_ffn2, norm_final):
    depth = norm_ffn1.shape[0]
    nbp, seq, d = x_prompt.shape
    nbs, dec, _ = x_sample.shape
    xp = x_prompt
    xs = x_sample.reshape(nbs * dec, d)
    prompt_tile = min(PROMPT_TILE, seq)
    new_p, new_s = [], []
    for l in range(depth):
        last = l == depth - 1
        ffn1 = _ffn_weights(w1_ffn1[l], w3_ffn1[l], w2_ffn1[l])
        ffn2 = _ffn_weights(w1_ffn2[l], w3_ffn2[l], w2_ffn2[l])
        mix = ((norm_mix[l], w_in[l], conv_w[l], conv_b[l], gate_bias[l], rel_bias[l], norm_mlstm_out[l])
               + _ffn_weights(w_in[l], w_out[l]))
        gf = norm_final if last else None

        xp, st = _mixer(xp, mix, None, tile=prompt_tile, chunk=prompt_tile, pos0=0, chunked=True,
                        ffn=(norm_ffn1[l],) + ffn1)
        new_p.append(st)
        xp = _ffn(xp.reshape(nbp * seq, d), norm_ffn2[l], ffn2, gf).reshape(nbp, seq, d)

        xs = _ffn(xs, norm_ffn1[l], ffn1)
        feature_major = lambda c: c.transpose(0, 2, 3, 1).reshape(nbs, A_WIDTH, -1)
        cache = (feature_major(cache_attn_k[l]), feature_major(cache_attn_v[l]),
                 state_mlstm_C[l], state_mlstm_n[l], state_mlstm_m[l], state_mlstm_conv[l])
        xs3, st = _mixer(xs.reshape(nbs, dec, d), mix, cache, tile=dec, chunk=dec, pos0=PAST_LEN, chunked=False,
                         cache_t=True)
        new_s.append(st)
        xs = _ffn(xs3.reshape(nbs * dec, d), norm_ffn2[l], ffn2, gf)

    stack = lambda states, i: jnp.stack([s[i] for s in states])
    return ((xp, xs.reshape(nbs, dec, d))
            + tuple(stack(new_p, i) for i in range(6)) + tuple(stack(new_s, i) for i in range(6)))
```
